```python
import math
import jax
import jax.numpy as jnp
from jax import lax
import numpy as np


D_MODEL = 1024
BATCH = 1
SEQ = 16384
DEPTH = 2

GRID_W = 64
CTX_LEN = 256
HEADS_A = 4
HD_A = 64
VD_A = 2 * HD_A
WIDTH_A = HEADS_A * VD_A
HEADS_B = 4
KV_HEADS_B = 2
REP_B = HEADS_B // KV_HEADS_B
HD_B = 64
WIDTH_B = HEADS_B * HD_B
WINDOW = 128
BLOCK = 128
CONV_CH = 256
CONV_K = 31
MIX_WIDTH = WIDTH_A + WIDTH_B + CONV_CH
SEG_QA = HEADS_A * 2 * HD_A
SEG_KA = HEADS_A * 2 * HD_A
SEG_VA = HEADS_A * VD_A
SEG_QB = HEADS_B * HD_B
SEG_KB = KV_HEADS_B * HD_B
SEG_VB = KV_HEADS_B * HD_B
SEG_UC = 2 * CONV_CH
IN_WIDTH = SEG_QA + SEG_KA + SEG_VA + SEG_QB + SEG_KB + SEG_VB + SEG_UC
SPLIT_POINTS = (SEG_QA,
                SEG_QA + SEG_KA,
                SEG_QA + SEG_KA + SEG_VA,
                SEG_QA + SEG_KA + SEG_VA + SEG_QB,
                SEG_QA + SEG_KA + SEG_VA + SEG_QB + SEG_KB,
                SEG_QA + SEG_KA + SEG_VA + SEG_QB + SEG_KB + SEG_VB)
N_EXPERTS = 16
N_GROUPS = 4
EXPERTS_PER_GROUP = N_EXPERTS // N_GROUPS
TOP_K = 2
EXPERT_FF = 512
ROPE_BASE = 10000.0
EPS = 1e-6
NEG_INF = -1e30

kernel_name = 'hybrid_diffusion_parallel_heads_moe'


def rmsnorm(x, g):
    xf = x.astype(jnp.float32)
    y = xf * lax.rsqrt(jnp.mean(xf * xf, axis=-1, keepdims=True) + EPS)
    return (y * g.astype(jnp.float32)).astype(x.dtype)


def modulate(h, shift, scale):
    return h * (1 + scale) + shift


def axial_rope_tables(n_tok, head_dim):
    rows = n_tok // GRID_W
    row = jnp.repeat(jnp.arange(rows, dtype=jnp.float32), GRID_W)
    col = jnp.tile(jnp.arange(GRID_W, dtype=jnp.float32), rows)
    pos = jnp.stack([row, col], axis=-1)
    n_freq = head_dim // 4
    inv = ROPE_BASE ** (-jnp.arange(n_freq, dtype=jnp.float32) / n_freq)
    ang = pos[:, :, None] * inv[None, None, :]
    return jnp.cos(ang), jnp.sin(ang)


def apply_axial_rope(x, cos, sin):
    shp = x.shape
    nf = shp[-1] // 4
    xr = x.reshape(shp[:-1] + (2, 2, nf))
    x1 = xr[..., 0, :]
    x2 = xr[..., 1, :]
    bshape = (1, shp[1]) + (1,) * (x.ndim - 3) + (2, nf)
    c = cos.reshape(bshape).astype(x.dtype)
    s = sin.reshape(bshape).astype(x.dtype)
    out = jnp.stack([x1 * c - x2 * s, x2 * c + x1 * s], axis=-2)
    return out.reshape(shp)


def diff_softmax_attend(q, k, v, lam):
    s = jnp.einsum('bqhmd,bkhmd->bhmqk', q, k).astype(jnp.float32) * (HD_A ** -0.5)
    p = jax.nn.softmax(s, axis=-1)
    a = p[:, :, 0] - lam * p[:, :, 1]
    return jnp.einsum('bhqk,bkhe->bqhe', a.astype(v.dtype), v)


def diff_attention_latent(q, k_all, v_all, lam):
    B, S = q.shape[:2]
    nb = S // BLOCK
    qb = jnp.moveaxis(q.reshape((B, nb, BLOCK) + q.shape[2:]), 1, 0)
    ob = lax.map(lambda qi: diff_softmax_attend(qi, k_all, v_all, lam), qb)
    return jnp.moveaxis(ob, 0, 1).reshape(B, S, HEADS_A, VD_A)


def diff_head_norm(o, g, lambda_init):
    of = o.astype(jnp.float32)
    y = of * lax.rsqrt(jnp.mean(of * of, axis=-1, keepdims=True) + EPS) * g.astype(jnp.float32)
    y = y * (1.0 - lambda_init)
    return y.reshape(o.shape[:2] + (WIDTH_A,)).astype(o.dtype)


def sink_softmax(scores, sink):
    full = jnp.concatenate([scores, jnp.broadcast_to(sink, scores.shape[:-1] + (1,))], axis=-1)
    return jax.nn.softmax(full, axis=-1)[..., :-1]


def window_sink_gqa_latent(q, k, v, kc, vc, sink):
    B, S = q.shape[:2]
    C = kc.shape[1]
    nb = S // BLOCK
    scale = HD_B ** -0.5
    qb = q.reshape(B, nb, BLOCK, KV_HEADS_B, REP_B, HD_B)
    pad = ((0, 0), (BLOCK, BLOCK), (0, 0), (0, 0))
    kp = jnp.pad(k, pad).reshape(B, nb + 2, BLOCK, KV_HEADS_B, HD_B)
    vp = jnp.pad(v, pad).reshape(B, nb + 2, BLOCK, KV_HEADS_B, HD_B)
    kband = jnp.concatenate([kp[:, :-2], kp[:, 1:-1], kp[:, 2:]], axis=2)
    vband = jnp.concatenate([vp[:, :-2], vp[:, 1:-1], vp[:, 2:]], axis=2)
    s_loc = jnp.einsum('bnqgrd,bnkgd->bngrqk', qb, kband).astype(jnp.float32) * scale
    blk = jnp.arange(nb)[:, None]
    qpos = blk * BLOCK + jnp.arange(BLOCK)[None, :]
    kpos = (blk - 1) * BLOCK + jnp.arange(3 * BLOCK)[None, :]
    valid = (jnp.abs(qpos[:, :, None] - kpos[:, None, :]) <= WINDOW) & (kpos >= 0)[:, None, :] & (kpos < S)[:, None, :]
    s_loc = jnp.where(valid[None, :, None, None], s_loc, NEG_INF)
    s_ctx = jnp.einsum('bnqgrd,bkgd->bngrqk', qb, kc).astype(jnp.float32) * scale
    sk = sink.astype(jnp.float32).reshape(KV_HEADS_B, REP_B)[None, None, :, :, None, None]
    p = sink_softmax(jnp.concatenate([s_ctx, s_loc], axis=-1), sk)
    p_ctx = p[..., :C].astype(v.dtype)
    p_loc = p[..., C:].astype(v.dtype)
    o = jnp.einsum('bngrqk,bkgd->bnqgrd', p_ctx, vc) + jnp.einsum('bngrqk,bnkgd->bnqgrd', p_loc, vband)
    return o.reshape(B, S, WIDTH_B)


def sink_gqa_context(q, k, v, sink):
    B, C = q.shape[:2]
    qg = q.reshape(B, C, KV_HEADS_B, REP_B, HD_B)
    s = jnp.einsum('bqgrd,bkgd->bgrqk', qg, k).astype(jnp.float32) * (HD_B ** -0.5)
    sk = sink.astype(jnp.float32).reshape(KV_HEADS_B, REP_B)[None, :, :, None, None]
    p = sink_softmax(s, sk)
    o = jnp.einsum('bgrqk,bkgd->bqgrd', p.astype(v.dtype), v)
    return o.reshape(B, C, WIDTH_B)


def conv_module(u, w, b, ln_g, ln_b):
    a, g = jnp.split(u, 2, axis=-1)
    z = a * jax.nn.sigmoid(g)
    z = lax.conv_general_dilated(z, w[:, None, :], window_strides=(1,),
                                 padding=[(CONV_K // 2, CONV_K // 2)],
                                 dimension_numbers=('NWC', 'WIO', 'NWC'),
                                 feature_group_count=CONV_CH) + b
    zf = z.astype(jnp.float32)
    mu = jnp.mean(zf, axis=-1, keepdims=True)
    var = jnp.mean(jnp.square(zf - mu), axis=-1, keepdims=True)
    zn = (zf - mu) * lax.rsqrt(var + EPS) * ln_g.astype(jnp.float32) + ln_b.astype(jnp.float32)
    return jax.nn.silu(zn).astype(u.dtype)


def grouped_moe(h, router_w, router_bias, w1, w3, w2):
    n = h.shape[0]
    s = jax.nn.sigmoid((h @ router_w).astype(jnp.float32))
    biased = s + router_bias.astype(jnp.float32)
    grp = lax.top_k(biased.reshape(n, N_GROUPS, EXPERTS_PER_GROUP), TOP_K)[0].sum(-1)
    g_sel = jnp.argmax(grp, axis=-1)
    in_grp = (jnp.arange(N_EXPERTS) // EXPERTS_PER_GROUP)[None, :] == g_sel[:, None]
    _, idx = lax.top_k(jnp.where(in_grp, biased, -jnp.inf), TOP_K)
    sel = jnp.take_along_axis(s, idx, axis=-1)
    wts = sel / jnp.sum(sel, axis=-1, keepdims=True)
    combine = jnp.sum(jax.nn.one_hot(idx, N_EXPERTS, dtype=jnp.float32) * wts[..., None], axis=1).astype(h.dtype)
    out = jnp.zeros_like(h)
    for e in range(N_EXPERTS):
        he = (jax.nn.silu(h @ w1[e]) * (h @ w3[e])) @ w2[e]
        out = out + combine[:, e:e + 1] * he
    return out


def setup_inputs(seed: int = 0) -> dict:
    key = jax.random.key(seed)
    ks = jax.random.split(key, 24)
    D = D_MODEL

    def nrm(k, shape, s):
        return jax.random.normal(k, shape, jnp.float32) * s

    return {
        'x': nrm(ks[0], (BATCH, SEQ, D), 1.0),
        'c': nrm(ks[1], (BATCH, D), 1.0),
        'ctx': nrm(ks[2], (BATCH, CTX_LEN, D), 1.0),
        'c_ctx': nrm(ks[3], (D,), 1.0),
        'w_ada': nrm(ks[4], (DEPTH, D, 6 * D), 0.5 * D ** -0.5),
        'b_ada': nrm(ks[5], (DEPTH, 6 * D), 0.02),
        'g_mix': 1.0 + nrm(ks[6], (DEPTH, D), 0.05),
        'w_in': nrm(ks[7], (DEPTH, D, IN_WIDTH), D ** -0.5),
        'diff_lambda': nrm(ks[8], (DEPTH, 4, HD_A), 0.1),
        'diff_norm_g': 1.0 + nrm(ks[9], (DEPTH, VD_A), 0.05),
        'attn_sink': nrm(ks[10], (DEPTH, HEADS_B), 0.5),
        'conv_w': nrm(ks[11], (DEPTH, CONV_K, CONV_CH), CONV_K ** -0.5),
        'conv_b': nrm(ks[12], (DEPTH, CONV_CH), 0.02),
        'conv_ln_g': 1.0 + nrm(ks[13], (DEPTH, CONV_CH), 0.05),
        'conv_ln_b': nrm(ks[14], (DEPTH, CONV_CH), 0.02),
        'w_out': nrm(ks[15], (DEPTH, MIX_WIDTH, D), MIX_WIDTH ** -0.5),
        'g_ffn': 1.0 + nrm(ks[16], (DEPTH, D), 0.05),
        'router_w': nrm(ks[17], (D, N_EXPERTS), D ** -0.5),
        'router_bias': nrm(ks[18], (N_EXPERTS,), 0.01),
        'w1': nrm(ks[19], (DEPTH, N_EXPERTS, D, EXPERT_FF), D ** -0.5),
        'w3': nrm(ks[20], (DEPTH, N_EXPERTS, D, EXPERT_FF), D ** -0.5),
        'w2': nrm(ks[21], (DEPTH, N_EXPERTS, EXPERT_FF, D), EXPERT_FF ** -0.5),
        'g_final': 1.0 + nrm(ks[22], (D,), 0.05),
    }


def reference(x, c, ctx, c_ctx, w_ada, b_ada, g_mix, w_in, diff_lambda, diff_norm_g, attn_sink,
              conv_w, conv_b, conv_ln_g, conv_ln_b, w_out, g_ffn, router_w, router_bias, w1, w3, w2, g_final):
    B, S, D = x.shape
    C = ctx.shape[1]
    cos, sin = axial_rope_tables(S, HD_A)
    xc = ctx
    silu_c = jax.nn.silu(c)
    silu_cc = jax.nn.silu(c_ctx)
    for l in range(DEPTH):
        last = l == DEPTH - 1
        mod = (silu_c @ w_ada[l] + b_ada[l]).reshape(B, 6, D)
        modc = (silu_cc @ w_ada[l] + b_ada[l]).reshape(6, D)

        h = modulate(rmsnorm(x, g_mix[l]), mod[:, 0, None], mod[:, 1, None])
        hc = modulate(rmsnorm(xc, g_mix[l]), modc[0], modc[1])
        qa, ka, va, qb, kb, vb, uc = jnp.split(h @ w_in[l], SPLIT_POINTS, axis=-1)
        qa_c, ka_c, va_c, qb_c, kb_c, vb_c, uc_c = jnp.split(hc @ w_in[l], SPLIT_POINTS, axis=-1)

        qa = apply_axial_rope(qa.reshape(B, S, HEADS_A, 2, HD_A), cos, sin)
        ka = apply_axial_rope(ka.reshape(B, S, HEADS_A, 2, HD_A), cos, sin)
        va = va.reshape(B, S, HEADS_A, VD_A)
        ka_c = ka_c.reshape(B, C, HEADS_A, 2, HD_A)
        va_c = va_c.reshape(B, C, HEADS_A, VD_A)
        qb = apply_axial_rope(qb.reshape(B, S, HEADS_B, HD_B), cos, sin)
        kb = apply_axial_rope(kb.reshape(B, S, KV_HEADS_B, HD_B), cos, sin)
        vb = vb.reshape(B, S, KV_HEADS_B, HD_B)
        kb_c = kb_c.reshape(B, C, KV_HEADS_B, HD_B)
        vb_c = vb_c.reshape(B, C, KV_HEADS_B, HD_B)

        lambda_init = 0.8 - 0.6 * math.exp(-0.3 * l)
        dl = diff_lambda[l].astype(jnp.float32)
        lam = jnp.exp(jnp.sum(dl[0] * dl[1])) - jnp.exp(jnp.sum(dl[2] * dl[3])) + lambda_init

        k_all = jnp.concatenate([ka_c, ka], axis=1)
        v_all = jnp.concatenate([va_c, va], axis=1)
        o_a = diff_head_norm(diff_attention_latent(qa, k_all, v_all, lam), diff_norm_g[l], lambda_init)
        o_b = window_sink_gqa_latent(qb, kb, vb, kb_c, vb_c, attn_sink[l])
        o_c = conv_module(uc, conv_w[l], conv_b[l], conv_ln_g[l], conv_ln_b[l])
        x = x + mod[:, 2, None] * (jnp.concatenate([o_a, o_b, o_c], axis=-1) @ w_out[l])

        if not last:
            o_ac = diff_head_norm(diff_softmax_attend(qa_c.reshape(B, C, HEADS_A, 2, HD_A), ka_c, va_c, lam),
                                  diff_norm_g[l], lambda_init)
            o_bc = sink_gqa_context(qb_c.reshape(B, C, HEADS_B, HD_B), kb_c, vb_c, attn_sink[l])
            o_cc = conv_module(uc_c, conv_w[l], conv_b[l], conv_ln_g[l], conv_ln_b[l])
            xc = xc + modc[2] * (jnp.concatenate([o_ac, o_bc, o_cc], axis=-1) @ w_out[l])

        h2 = modulate(rmsnorm(x, g_ffn[l]), mod[:, 3, None], mod[:, 4, None])
        if last:
            tokens = h2.reshape(B * S, D)
        else:
            h2c = modulate(rmsnorm(xc, g_ffn[l]), modc[3], modc[4])
            tokens = jnp.concatenate([h2.reshape(B * S, D), h2c.reshape(B * C, D)], axis=0)
        y = grouped_moe(tokens, router_w, router_bias, w1[l], w3[l], w2[l])
        x = x + mod[:, 5, None] * y[:B * S].reshape(B, S, D)
        if not last:
            xc = xc + modc[5] * y[B * S:].reshape(B, C, D)
    return rmsnorm(x, g_final)
```

```python
import functools
import math

import jax
import jax.numpy as jnp
from jax import lax
from jax.experimental import pallas as pl
from jax.experimental.pallas import tpu as pltpu

f32 = jnp.float32
bf16 = jnp.bfloat16

D_MODEL = 1024
GRID_W = 64
HEADS_A = 4
HD = 64
VD_A = 2 * HD
WIDTH_A = HEADS_A * VD_A
HEADS_B = 4
KV_HEADS_B = 2
WIDTH_B = HEADS_B * HD
WINDOW = 128
CONV_CH = 256
CONV_K = 31
CONV_HALO = 16
IN_WIDTH = 2560
OFF_QA, OFF_KA, OFF_VA, OFF_QB, OFF_KB, OFF_VB, OFF_UC = 0, 512, 1024, 1536, 1792, 1920, 2048
N_EXPERTS = 16
N_GROUPS = 4
EXPERTS_PER_GROUP = 4
EXPERT_FF = 512
ROPE_BASE = 10000.0
EPS = 1e-6
NEG_INF = -1e30
LANES = 128
QK_SCALE = HD ** -0.5

VMEM_LIMIT = 56 * 1024 * 1024


def _cparams(sem):
    return pltpu.CompilerParams(dimension_semantics=sem, vmem_limit_bytes=VMEM_LIMIT)


def _mod_kernel(ct_ref, w_ref, b_ref, o_ref):
    tn = w_ref.shape[1]

    def body(i, carry):
        a0, a1 = carry
        r = pl.multiple_of(i * 8, 8)
        cv = ct_ref[pl.ds(r, 8), :]
        sv = cv * jax.nn.sigmoid(cv)
        w8 = w_ref[pl.ds(r, 8), :]
        return a0 + w8 * sv[:, 0:1], a1 + w8 * sv[:, 1:2]

    z = jnp.zeros((8, tn), f32)
    a0, a1 = lax.fori_loop(0, w_ref.shape[0] // 8, body, (z, z))
    r0 = jnp.sum(a0, axis=0, keepdims=True) + b_ref[...]
    r1 = jnp.sum(a1, axis=0, keepdims=True) + b_ref[...]
    o_ref[...] = jnp.concatenate([r0, r1, jnp.zeros((6, tn), f32)], axis=0)


def _modulation(ct, w_ada, b_ada):
    depth, d, n = w_ada.shape
    tn = 1536
    return pl.pallas_call(
        _mod_kernel,
        grid=(depth, n // tn),
        in_specs=[
            pl.BlockSpec((d, LANES), lambda l, j: (0, 0)),
            pl.BlockSpec((None, d, tn), lambda l, j: (l, 0, j)),
            pl.BlockSpec((None, 1, tn), lambda l, j: (l, 0, j)),
        ],
        out_specs=pl.BlockSpec((None, 8, tn), lambda l, j: (l, 0, j)),
        out_shape=jax.ShapeDtypeStruct((depth, 8, n), f32),
        compiler_params=_cparams(("arbitrary", "arbitrary")),
        name="modulation",
    )(ct, w_ada, b_ada.reshape(depth, 1, n))


def _rmsnorm_mod(xf, g, shift, scale):
    y = xf * lax.rsqrt(jnp.mean(xf * xf, axis=-1, keepdims=True) + EPS) * g
    return y * (1.0 + scale) + shift


def _inproj_kernel(*refs, rope):
    if rope:
        x_ref, sh_ref, sc_ref, g_ref, w_ref, cos_ref, sin_ref = refs[:7]
        outs = refs[7:]
    else:
        x_ref, sh_ref, sc_ref, g_ref, w_ref = refs[:5]
        outs = refs[5:]
    qa_ref, ka_ref, va_ref, qb_ref, kb_ref, vb_ref, z_ref = outs

    h = _rmsnorm_mod(x_ref[...], g_ref[...], sh_ref[...], sc_ref[...])
    p = jnp.dot(h.astype(bf16), w_ref[...], preferred_element_type=f32)
    tm = p.shape[0]
    lane = lax.broadcasted_iota(jnp.int32, (tm, LANES), 1)
    lo_half = lane < HD

    def rot(xc, scale):
        if rope:
            first = (lane % 32) < 16
            partner = jnp.where(first, pltpu.roll(xc, LANES - 16, 1), pltpu.roll(xc, 16, 1))
            xc = xc * cos_ref[...] + partner * sin_ref[...]
        return xc * scale if scale != 1.0 else xc

    def chunk(off, j):
        return p[:, off + j * LANES: off + (j + 1) * LANES]

    ones = jnp.ones((tm, LANES), bf16)
    for hh in range(HEADS_A):
        qa_ref[hh] = rot(chunk(OFF_QA, hh), QK_SCALE).astype(bf16)
        ka_ref[hh] = rot(chunk(OFF_KA, hh), 1.0).astype(bf16)
        va_ref[hh] = jnp.concatenate([chunk(OFF_VA, hh).astype(bf16), ones], axis=1)
    for g in range(KV_HEADS_B):
        qb_ref[g] = rot(chunk(OFF_QB, g), QK_SCALE).astype(bf16)
    kb = rot(chunk(OFF_KB, 0), 1.0)
    kb_sw = pltpu.roll(kb, HD, 1)
    kb_ref[0] = jnp.where(lo_half, kb, kb_sw).astype(bf16)
    kb_ref[1] = jnp.where(lo_half, kb_sw, kb).astype(bf16)
    vb = chunk(OFF_VB, 0)
    vb_sw = pltpu.roll(vb, HD, 1)
    zero = jnp.zeros_like(vb)
    vb_ref[0, 0] = jnp.where(lo_half, vb, zero).astype(bf16)
    vb_ref[0, 1] = jnp.where(lo_half, zero, vb_sw).astype(bf16)
    vb_ref[1, 0] = jnp.where(lo_half, vb_sw, zero).astype(bf16)
    vb_ref[1, 1] = jnp.where(lo_half, zero, vb).astype(bf16)
    a = p[:, OFF_UC: OFF_UC + CONV_CH]
    gt = p[:, OFF_UC + CONV_CH: OFF_UC + 2 * CONV_CH]
    z_ref[...] = a * jax.nn.sigmoid(gt)


def _inproj(x2, shift, scale, g, w_bf, cos_t, sin_t, *, tm):
    n, d = x2.shape
    rope = cos_t is not None
    row = lambda i: (i, 0)
    const = lambda i: (0, 0)
    in_specs = [
        pl.BlockSpec((tm, d), row),
        pl.BlockSpec((1, d), const),
        pl.BlockSpec((1, d), const),
        pl.BlockSpec((1, d), const),
        pl.BlockSpec((d, IN_WIDTH), const),
    ]
    args = [x2, shift, scale, g, w_bf]
    if rope:
        in_specs += [pl.BlockSpec((tm, LANES), row), pl.BlockSpec((tm, LANES), row)]
        args += [cos_t, sin_t]
    out_shape = (
        jax.ShapeDtypeStruct((HEADS_A, n, LANES), bf16),
        jax.ShapeDtypeStruct((HEADS_A, n, LANES), bf16),
        jax.ShapeDtypeStruct((HEADS_A, n, 2 * LANES), bf16),
        jax.ShapeDtypeStruct((KV_HEADS_B, n, LANES), bf16),
        jax.ShapeDtypeStruct((KV_HEADS_B, n, LANES), bf16),
        jax.ShapeDtypeStruct((KV_HEADS_B, 2, n, LANES), bf16),
        jax.ShapeDtypeStruct((n, CONV_CH), f32),
    )
    out_specs = (
        pl.BlockSpec((HEADS_A, tm, LANES), lambda i: (0, i, 0)),
        pl.BlockSpec((HEADS_A, tm, LANES), lambda i: (0, i, 0)),
        pl.BlockSpec((HEADS_A, tm, 2 * LANES), lambda i: (0, i, 0)),
        pl.BlockSpec((KV_HEADS_B, tm, LANES), lambda i: (0, i, 0)),
        pl.BlockSpec((KV_HEADS_B, tm, LANES), lambda i: (0, i, 0)),
        pl.BlockSpec((KV_HEADS_B, 2, tm, LANES), lambda i: (0, 0, i, 0)),
        pl.BlockSpec((tm, CONV_CH), row),
    )
    return pl.pallas_call(
        functools.partial(_inproj_kernel, rope=rope),
        grid=(n // tm,),
        in_specs=in_specs,
        out_specs=out_specs,
        out_shape=out_shape,
        compiler_params=_cparams(("parallel",)),
        name="inproj_rope" if rope else "inproj_ctx",
    )(*args)


def _diff_attn_kernel(*refs, tq, tk, n_main, lambda_init):
    if n_main:
        dl_ref, g_ref, q_ref, kl_ref, vl_ref, kc_ref, vc_ref, o_ref, m_scr, acc_scr = refs
    else:
        dl_ref, g_ref, q_ref, kc_ref, vc_ref, o_ref, m_scr, acc_scr = refs

    q = q_ref[...]
    lane = lax.broadcasted_iota(jnp.int32, q.shape, 1)
    zero = jnp.zeros_like(q)
    qq = jnp.concatenate([jnp.where(lane < HD, q, zero), jnp.where(lane < HD, zero, q)], axis=0)
    m_scr[...] = jnp.full(m_scr.shape, NEG_INF, f32)
    acc_scr[...] = jnp.zeros(acc_scr.shape, f32)

    def step(k, v):
        s = lax.dot_general(qq, k, (((1,), (1,)), ((), ())), preferred_element_type=f32)
        m_old = m_scr[...]
        m_new = jnp.maximum(m_old, jnp.max(s, axis=1, keepdims=True))
        alpha = jnp.exp(m_old - m_new)
        p = jnp.exp(s - m_new).astype(bf16)
        acc_scr[...] = alpha * acc_scr[...] + jnp.dot(p, v, preferred_element_type=f32)
        m_scr[...] = m_new

    if n_main:
        def body(c, carry):
            r = pl.multiple_of(c * tk, tk)
            step(kl_ref[pl.ds(r, tk), :], vl_ref[pl.ds(r, tk), :])
            return carry
        lax.fori_loop(0, n_main, body, 0)
    step(kc_ref[...], vc_ref[...])

    acc = acc_scr[...]
    o0 = acc[:tq, :VD_A] / acc[:tq, VD_A:VD_A + 1]
    o1 = acc[tq:, :VD_A] / acc[tq:, VD_A:VD_A + 1]
    dl = dl_ref[...]
    lam = (jnp.exp(jnp.sum(dl[0:1] * dl[1:2], axis=1, keepdims=True))
           - jnp.exp(jnp.sum(dl[2:3] * dl[3:4], axis=1, keepdims=True)) + lambda_init)
    o = o0 - lam * o1
    y = o * lax.rsqrt(jnp.mean(o * o, axis=-1, keepdims=True) + EPS) * g_ref[...]
    o_ref[...] = (y * (1.0 - lambda_init)).astype(o_ref.dtype)


def _diff_attention(dl, g, qa, ka_lat, va_lat, ka_ctx, va_ctx, *, lambda_init, tq, tk):
    nq = qa.shape[1]
    c = ka_ctx.shape[1]
    n_main = 0 if ka_lat is None else ka_lat.shape[1] // tk
    in_specs = [
        pl.BlockSpec((4, HD), lambda h, i: (0, 0)),
        pl.BlockSpec((1, VD_A), lambda h, i: (0, 0)),
        pl.BlockSpec((None, tq, LANES), lambda h, i: (h, i, 0)),
    ]
    args = [dl, g, qa]
    if n_main:
        s_keys = ka_lat.shape[1]
        in_specs += [pl.BlockSpec((None, s_keys, LANES), lambda h, i: (h, 0, 0)),
                     pl.BlockSpec((None, s_keys, 2 * LANES), lambda h, i: (h, 0, 0))]
        args += [ka_lat, va_lat]
    in_specs += [pl.BlockSpec((None, c, LANES), lambda h, i: (h, 0, 0)),
                 pl.BlockSpec((None, c, 2 * LANES), lambda h, i: (h, 0, 0))]
    args += [ka_ctx, va_ctx]
    return pl.pallas_call(
        functools.partial(_diff_attn_kernel, tq=tq, tk=tk, n_main=n_main, lambda_init=lambda_init),
        grid=(HEADS_A, nq // tq),
        in_specs=in_specs,
        out_specs=pl.BlockSpec((tq, VD_A), lambda h, i: (i, h)),
        out_shape=jax.ShapeDtypeStruct((nq, WIDTH_A), bf16),
        scratch_shapes=[pltpu.VMEM((2 * tq, 1), f32), pltpu.VMEM((2 * tq, 2 * LANES), f32)],
        compiler_params=_cparams(("arbitrary", "arbitrary")),
        name="diff_attn" if n_main else "diff_attn_ctx",
    )(*args)


def _win_attn_kernel(*refs, tq, band, s_len):
    if band:
        (sink_ref, q_ref, kp_ref, ks_ref, kn_ref, kc_ref,
         vp_ref, vs_ref, vn_ref, vc_ref, o_ref) = refs
    else:
        sink_ref, q_ref, kc_ref, vc_ref, o_ref = refs
    i = pl.program_id(0)
    lane = lax.broadcasted_iota(jnp.int32, (tq, LANES), 1)
    if band:
        nloc = tq + 2 * WINDOW
        qpos = i * tq + lax.broadcasted_iota(jnp.int32, (tq, nloc), 0)
        kpos = i * tq - WINDOW + lax.broadcasted_iota(jnp.int32, (tq, nloc), 1)
        valid = (jnp.abs(qpos - kpos) <= WINDOW) & (kpos >= 0) & (kpos < s_len)
    nt = (((1,), (1,)), ((), ()))
    for g in range(KV_HEADS_B):
        q = q_ref[g]
        kc = kc_ref[g]
        if band:
            kloc = jnp.concatenate([kp_ref[g], ks_ref[g], kn_ref[g]], axis=0)
        o = jnp.zeros((tq, LANES), f32)
        for r in range(2):
            sink = sink_ref[2 * g + r]
            qm = jnp.where((lane < HD) if r == 0 else (lane >= HD), q, jnp.zeros_like(q))
            s_ctx = lax.dot_general(qm, kc, nt, preferred_element_type=f32)
            m = jnp.maximum(jnp.max(s_ctx, axis=1, keepdims=True), sink)
            if band:
                s_loc = lax.dot_general(qm, kloc, nt, preferred_element_type=f32)
                s_loc = jnp.where(valid, s_loc, NEG_INF)
                m = jnp.maximum(m, jnp.max(s_loc, axis=1, keepdims=True))
            e_ctx = jnp.exp(s_ctx - m)
            l = jnp.sum(e_ctx, axis=1, keepdims=True) + jnp.exp(sink - m)
            if band:
                e_loc = jnp.exp(s_loc - m)
                l = l + jnp.sum(e_loc, axis=1, keepdims=True)
            o = o + jnp.dot((e_ctx / l).astype(bf16), vc_ref[g, r], preferred_element_type=f32)
            if band:
                vloc = jnp.concatenate([vp_ref[g, r], vs_ref[g, r], vn_ref[g, r]], axis=0)
                o = o + jnp.dot((e_loc / l).astype(bf16), vloc, preferred_element_type=f32)
        o_ref[:, g * LANES:(g + 1) * LANES] = o.astype(o_ref.dtype)


def _win_attention(sink, qb, kb_lat, vb_lat, kb_ctx, vb_ctx, *, tq):
    nq = qb.shape[1]
    c = kb_ctx.shape[1]
    band = kb_lat is not None
    smem = pl.BlockSpec(memory_space=pltpu.SMEM)
    q_spec = pl.BlockSpec((KV_HEADS_B, tq, LANES), lambda i: (0, i, 0))
    kc_spec = pl.BlockSpec((KV_HEADS_B, c, LANES), lambda i: (0, 0, 0))
    vc_spec = pl.BlockSpec((KV_HEADS_B, 2, c, LANES), lambda i: (0, 0, 0, 0))
    if band:
        per = tq // WINDOW
        last = nq // WINDOW - 1
        prev = lambda i: jnp.maximum(i * per - 1, 0)
        nxt = lambda i: jnp.minimum((i + 1) * per, last)
        in_specs = [
            smem, q_spec,
            pl.BlockSpec((KV_HEADS_B, WINDOW, LANES), lambda i: (0, prev(i), 0)),
            pl.BlockSpec((KV_HEADS_B, tq, LANES), lambda i: (0, i, 0)),
            pl.BlockSpec((KV_HEADS_B, WINDOW, LANES), lambda i: (0, nxt(i), 0)),
            kc_spec,
            pl.BlockSpec((KV_HEADS_B, 2, WINDOW, LANES), lambda i: (0, 0, prev(i), 0)),
            pl.BlockSpec((KV_HEADS_B, 2, tq, LANES), lambda i: (0, 0, i, 0)),
            pl.BlockSpec((KV_HEADS_B, 2, WINDOW, LANES), lambda i: (0, 0, nxt(i), 0)),
            vc_spec,
        ]
        args = [sink, qb, kb_lat, kb_lat, kb_lat, kb_ctx, vb_lat, vb_lat, vb_lat, vb_ctx]
    else:
        in_specs = [smem, q_spec, kc_spec, vc_spec]
        args = [sink, qb, kb_ctx, vb_ctx]
    return pl.pallas_call(
        functools.partial(_win_attn_kernel, tq=tq, band=band, s_len=nq),
        grid=(nq // tq,),
        in_specs=in_specs,
        out_specs=pl.BlockSpec((tq, WIDTH_B), lambda i: (i, 0)),
        out_shape=jax.ShapeDtypeStruct((nq, WIDTH_B), bf16),
        compiler_params=_cparams(("parallel",)),
        name="win_attn" if band else "win_attn_ctx",
    )(*args)


def _conv_kernel(zp_ref, zs_ref, zn_ref, w_ref, b_ref, lg_ref, lb_ref, o_ref, zbuf, *, tm, rc):
    i = pl.program_id(0)
    n = pl.num_programs(0)
    halo = CONV_HALO
    zero = jnp.zeros((halo, CONV_CH), f32)
    zbuf[0:halo, :] = jnp.where(i > 0, zp_ref[...], zero)
    zbuf[halo:halo + tm, :] = zs_ref[...]
    zbuf[halo + tm:2 * halo + tm, :] = jnp.where(i < n - 1, zn_ref[...], zero)
    w = w_ref[...]
    base = halo - CONV_K // 2

    for r0 in range(0, tm, rc):
        acc = jnp.zeros((rc, CONV_CH), f32)
        for k in range(CONV_K):
            acc = acc + zbuf[r0 + base + k: r0 + base + k + rc, :] * w[k:k + 1, :]
        zc = acc + b_ref[...]
        mu = jnp.mean(zc, axis=-1, keepdims=True)
        dz = zc - mu
        var = jnp.mean(dz * dz, axis=-1, keepdims=True)
        zn = dz * lax.rsqrt(var + EPS) * lg_ref[...] + lb_ref[...]
        o_ref[r0:r0 + rc, :] = (zn * jax.nn.sigmoid(zn)).astype(o_ref.dtype)


def _conv_module(z, w, b, ln_g, ln_b, *, tm):
    n = z.shape[0]
    per = tm // CONV_HALO
    last = n // CONV_HALO - 1
    vec = lambda i: (0, 0)
    return pl.pallas_call(
        functools.partial(_conv_kernel, tm=tm, rc=32),
        grid=(n // tm,),
        in_specs=[
            pl.BlockSpec((CONV_HALO, CONV_CH), lambda i: (jnp.maximum(i * per - 1, 0), 0)),
            pl.BlockSpec((tm, CONV_CH), lambda i: (i, 0)),
            pl.BlockSpec((CONV_HALO, CONV_CH), lambda i: (jnp.minimum((i + 1) * per, last), 0)),
            pl.BlockSpec((CONV_K, CONV_CH), vec),
            pl.BlockSpec((1, CONV_CH), vec),
            pl.BlockSpec((1, CONV_CH), vec),
            pl.BlockSpec((1, CONV_CH), vec),
        ],
        out_specs=pl.BlockSpec((tm, CONV_CH), lambda i: (i, 0)),
        out_shape=jax.ShapeDtypeStruct((n, CONV_CH), bf16),
        scratch_shapes=[pltpu.VMEM((tm + 2 * CONV_HALO, CONV_CH), f32)],
        compiler_params=_cparams(("parallel",)),
        name="conv_module",
    )(z, z, z, w, b, ln_g, ln_b)


def _split_bf16(a):
    hi = a.astype(bf16)
    lo = (a - hi.astype(f32)).astype(bf16)
    return hi, lo


def _route(lt, bias_ref):
    s = [jax.nn.sigmoid(lt[e:e + 1, :]) for e in range(N_EXPERTS)]
    b = [s[e] + bias_ref[e] for e in range(N_EXPERTS)]
    grp = []
    for g in range(N_GROUPS):
        v = b[4 * g:4 * g + 4]
        best = None
        for a_i in range(4):
            for b_i in range(a_i + 1, 4):
                t = v[a_i] + v[b_i]
                best = t if best is None else jnp.maximum(best, t)
        grp.append(best)
    sel = jnp.zeros_like(lt[0:1, :], dtype=jnp.int32)
    gbest = grp[0]
    for g in range(1, N_GROUPS):
        better = grp[g] > gbest
        sel = jnp.where(better, g, sel)
        gbest = jnp.where(better, grp[g], gbest)

    def pick(rows, j):
        out = rows[j]
        for g in range(1, N_GROUPS):
            out = jnp.where(sel == g, rows[4 * g + j], out)
        return out

    vb = [pick(b, j) for j in range(4)]
    vs = [pick(s, j) for j in range(4)]
    i1 = jnp.zeros_like(sel)
    b1 = vb[0]
    for j in range(1, 4):
        better = vb[j] > b1
        i1 = jnp.where(better, j, i1)
        b1 = jnp.where(better, vb[j], b1)
    i2 = jnp.full_like(sel, -1)
    b2 = jnp.full_like(b1, -jnp.inf)
    for j in range(4):
        better = (i1 != j) & ((i2 < 0) | (vb[j] > b2))
        i2 = jnp.where(better, j, i2)
        b2 = jnp.where(better, vb[j], b2)
    s1 = vs[0]
    s2 = vs[0]
    for j in range(1, 4):
        s1 = jnp.where(i1 == j, vs[j], s1)
        s2 = jnp.where(i2 == j, vs[j], s2)
    tot = s1 + s2
    w1 = s1 / tot
    w2 = s2 / tot
    e1 = sel * EXPERTS_PER_GROUP + i1
    e2 = sel * EXPERTS_PER_GROUP + i2
    rows = []
    zero = jnp.zeros_like(w1)
    for e in range(N_EXPERTS):
        rows.append(jnp.where(e1 == e, w1, zero) + jnp.where(e2 == e, w2, zero))
    return jnp.concatenate(rows, axis=0)


def _outproj_kernel(bias_ref, x_ref, oa_ref, ob_ref, oc_ref, w_ref, gate_ref, sh_ref, sc_ref, g_ref,
                    rwt_ref, xo_ref, h2_ref, comb_ref):
    y = jnp.dot(oa_ref[...], w_ref[0:WIDTH_A, :], preferred_element_type=f32)
    y = y + jnp.dot(ob_ref[...], w_ref[WIDTH_A:WIDTH_A + WIDTH_B, :], preferred_element_type=f32)
    y = y + jnp.dot(oc_ref[...], w_ref[WIDTH_A + WIDTH_B:, :], preferred_element_type=f32)
    xn = x_ref[...] + gate_ref[...] * y
    xo_ref[...] = xn
    h2 = _rmsnorm_mod(xn, g_ref[...], sh_ref[...], sc_ref[...])
    h2_ref[...] = h2.astype(bf16)
    nt = (((1,), (1,)), ((), ()))
    h_hi, h_lo = _split_bf16(h2)
    r_hi, r_lo = _split_bf16(rwt_ref[...])
    lt = (lax.dot_general(r_hi, h_hi, nt, preferred_element_type=f32)
          + lax.dot_general(r_hi, h_lo, nt, preferred_element_type=f32)
          + lax.dot_general(r_lo, h_hi, nt, preferred_element_type=f32))
    comb_ref[...] = _route(lt, bias_ref)


def _outproj(router_bias, x2, oa, ob, oc, w_bf, gate, shift, scale, g, rwt, *, tm):
    n, d = x2.shape
    row = lambda i: (i, 0)
    const = lambda i: (0, 0)
    return pl.pallas_call(
        _outproj_kernel,
        grid=(n // tm,),
        in_specs=[
            pl.BlockSpec(memory_space=pltpu.SMEM),
            pl.BlockSpec((tm, d), row),
            pl.BlockSpec((tm, WIDTH_A), row),
            pl.BlockSpec((tm, WIDTH_B), row),
            pl.BlockSpec((tm, CONV_CH), row),
            pl.BlockSpec((d, d), const),
            pl.BlockSpec((1, d), const),
            pl.BlockSpec((1, d), const),
            pl.BlockSpec((1, d), const),
            pl.BlockSpec((1, d), const),
            pl.BlockSpec((N_EXPERTS, d), const),
        ],
        out_specs=(
            pl.BlockSpec((tm, d), row),
            pl.BlockSpec((tm, d), row),
            pl.BlockSpec((N_EXPERTS, tm), lambda i: (0, i)),
        ),
        out_shape=(
            jax.ShapeDtypeStruct((n, d), f32),
            jax.ShapeDtypeStruct((n, d), bf16),
            jax.ShapeDtypeStruct((N_EXPERTS, n), f32),
        ),
        compiler_params=_cparams(("parallel",)),
        name="outproj_router",
    )(router_bias, x2, oa, ob, oc, w_bf, gate, shift, scale, g, rwt)


def _moe_kernel(*refs, final):
    if final:
        x_ref, h_ref, comb_ref, w1_ref, w3_ref, w2_ref, gate_ref, gf_ref, o_ref, acc = refs
    else:
        x_ref, h_ref, comb_ref, w1_ref, w3_ref, w2_ref, gate_ref, o_ref, acc = refs
    e = pl.program_id(1)

    @pl.when(e == 0)
    def _():
        acc[...] = jnp.zeros(acc.shape, f32)

    h = h_ref[...]
    a = jnp.dot(h, w1_ref[...], preferred_element_type=f32)
    b = jnp.dot(h, w3_ref[...], preferred_element_type=f32)
    u = (a * jax.nn.sigmoid(a) * b).astype(bf16)
    y = jnp.dot(u, w2_ref[...], preferred_element_type=f32)
    comb = comb_ref[...]
    lane = lax.broadcasted_iota(jnp.int32, comb.shape, 1)
    cw = jnp.sum(jnp.where(lane == e, comb, 0.0), axis=1, keepdims=True)
    acc[...] += cw * y

    @pl.when(e == pl.num_programs(1) - 1)
    def _():
        xn = x_ref[...] + gate_ref[...] * acc[...]
        if final:
            xn = xn * lax.rsqrt(jnp.mean(xn * xn, axis=-1, keepdims=True) + EPS) * gf_ref[...]
        o_ref[...] = xn


def _moe(x2, h2, comb, w1, w3, w2, gate, g_final, *, tm):
    n, d = x2.shape
    final = g_final is not None
    row = lambda i, e: (i, 0)
    const = lambda i, e: (0, 0)
    in_specs = [
        pl.BlockSpec((tm, d), row),
        pl.BlockSpec((tm, d), row),
        pl.BlockSpec((tm, N_EXPERTS), row),
        pl.BlockSpec((None, d, EXPERT_FF), lambda i, e: (e, 0, 0)),
        pl.BlockSpec((None, d, EXPERT_FF), lambda i, e: (e, 0, 0)),
        pl.BlockSpec((None, EXPERT_FF, d), lambda i, e: (e, 0, 0)),
        pl.BlockSpec((1, d), const),
    ]
    args = [x2, h2, comb, w1, w3, w2, gate]
    if final:
        in_specs.append(pl.BlockSpec((1, d), const))
        args.append(g_final)
    return pl.pallas_call(
        functools.partial(_moe_kernel, final=final),
        grid=(n // tm, N_EXPERTS),
        in_specs=in_specs,
        out_specs=pl.BlockSpec((tm, d), row),
        out_shape=jax.ShapeDtypeStruct((n, d), f32),
        scratch_shapes=[pltpu.VMEM((tm, d), f32)],
        compiler_params=_cparams(("parallel", "arbitrary")),
        name="moe_final" if final else "moe",
    )(*args)


def _rope_tables(n_tok):
    rows = n_tok // GRID_W
    row = jnp.repeat(jnp.arange(rows, dtype=f32), GRID_W)
    col = jnp.tile(jnp.arange(GRID_W, dtype=f32), rows)
    n_freq = HD // 4
    inv = ROPE_BASE ** (-jnp.arange(n_freq, dtype=f32) / n_freq)
    lane = jnp.arange(LANES)
    j = lane % HD
    use_col = (j // 32) == 1
    freq = inv[j % n_freq]
    pos = jnp.where(use_col[None, :], col[:, None], row[:, None])
    ang = pos * freq[None, :]
    sign = jnp.where((lane % 32) < 16, -1.0, 1.0).astype(f32)
    return jnp.cos(ang), jnp.sin(ang) * sign[None, :]


def kernel(x, c, ctx, c_ctx, w_ada, b_ada, g_mix, w_in, diff_lambda, diff_norm_g, attn_sink, conv_w, conv_b,
           conv_ln_g, conv_ln_b, w_out, g_ffn, router_w, router_bias, w1, w3, w2, g_final):
    bsz, s_len, d = x.shape
    c_len = ctx.shape[1]
    depth = w_ada.shape[0]
    assert bsz == 1 and d == D_MODEL and s_len % 1024 == 0 and c_len % 256 == 0
    tm_lat = 256
    tm_ctx = min(c_len, 256)
    tm_moe = 512

    xl = x.reshape(s_len, d)
    xc = ctx.reshape(c_len, d)
    ct = jnp.zeros((d, LANES), f32).at[:, 0].set(c[0]).at[:, 1].set(c_ctx)
    mod_all = _modulation(ct, w_ada, b_ada)
    cos_t, sin_t = _rope_tables(s_len)
    rwt = router_w.T
    w_in_bf = w_in.astype(bf16)
    w_out_bf = w_out.astype(bf16)
    w1_bf, w3_bf, w2_bf = w1.astype(bf16), w3.astype(bf16), w2.astype(bf16)
    vec = lambda a: a.reshape(1, -1)

    for l in range(depth):
        last = l == depth - 1
        lambda_init = 0.8 - 0.6 * math.exp(-0.3 * l)
        ml = [mod_all[l, 0:1, k * d:(k + 1) * d] for k in range(6)]
        mc = [mod_all[l, 1:2, k * d:(k + 1) * d] for k in range(6)]
        g_mix_l, g_ffn_l = vec(g_mix[l]), vec(g_ffn[l])
        dl, dng = diff_lambda[l], vec(diff_norm_g[l])

        qa, ka, va, qb, kb, vb, z = _inproj(xl, ml[0], ml[1], g_mix_l, w_in_bf[l], cos_t, sin_t, tm=tm_lat)
        qa_c, ka_c, va_c, qb_c, kb_c, vb_c, z_c = _inproj(xc, mc[0], mc[1], g_mix_l, w_in_bf[l], None, None,
                                                          tm=tm_ctx)

        o_a = _diff_attention(dl, dng, qa, ka, va, ka_c, va_c, lambda_init=lambda_init, tq=256, tk=1024)
        o_b = _win_attention(attn_sink[l], qb, kb, vb, kb_c, vb_c, tq=256)
        o_c = _conv_module(z, conv_w[l], vec(conv_b[l]), vec(conv_ln_g[l]), vec(conv_ln_b[l]), tm=tm_lat)
        xl, h2, comb = _outproj(router_bias, xl, o_a, o_b, o_c, w_out_bf[l], ml[2], ml[3], ml[4], g_ffn_l, rwt,
                                tm=tm_lat)
        xl = _moe(xl, h2, comb.T, w1_bf[l], w3_bf[l], w2_bf[l], ml[5], vec(g_final) if last else None,
                  tm=tm_moe)

        if not last:
            o_ac = _diff_attention(dl, dng, qa_c, None, None, ka_c, va_c, lambda_init=lambda_init,
                                   tq=tm_ctx, tk=1024)
            o_bc = _win_attention(attn_sink[l], qb_c, None, None, kb_c, vb_c, tq=tm_ctx)
            o_cc = _conv_module(z_c, conv_w[l], vec(conv_b[l]), vec(conv_ln_g[l]), vec(conv_ln_b[l]), tm=tm_ctx)
            xc, h2c, comb_c = _outproj(router_bias, xc, o_ac, o_bc, o_cc, w_out_bf[l], mc[2], mc[3], mc[4],
                                       g_ffn_l, rwt, tm=tm_ctx)
            xc = _moe(xc, h2c, comb_c.T, w1_bf[l], w3_bf[l], w2_bf[l], mc[5], None, tm=tm_ctx)

    return xl.reshape(bsz, s_len, d)
```

```python
import functools
import math

import jax
import jax.numpy as jnp
from jax import lax
from jax.experimental import pallas as pl
from jax.experimental.pallas import tpu as pltpu

f32 = jnp.float32
bf16 = jnp.bfloat16

D_MODEL = 1024
GRID_W = 64
HEADS_A = 4
HD = 64
VD_A = 2 * HD
VT_ROWS = VD_A + 16
WIDTH_A = HEADS_A * VD_A
HEADS_B = 4
KV_HEADS_B = 2
WIDTH_B = HEADS_B * HD
WINDOW = 128
CONV_CH = 256
CONV_K = 31
CONV_HALO = 16
IN_WIDTH = 2560
OFF_QA, OFF_KA, OFF_VA, OFF_QB, OFF_KB, OFF_VB, OFF_UC = 0, 512, 1024, 1536, 1792, 1920, 2048
N_EXPERTS = 16
N_GROUPS = 4
EXPERTS_PER_GROUP = 4
EXPERT_FF = 512
ROPE_BASE = 10000.0
EPS = 1e-6
NEG_INF = -1e30
LANES = 128
QK_SCALE = HD ** -0.5
QK_SCALE_LOG2 = QK_SCALE * math.log2(math.e)

VMEM_LIMIT = 56 * 1024 * 1024


def _cparams(sem):
    return pltpu.CompilerParams(dimension_semantics=sem, vmem_limit_bytes=VMEM_LIMIT)


def _mod_kernel(ct_ref, w_ref, b_ref, o_ref):
    tn = w_ref.shape[1]

    def body(i, carry):
        a0, a1 = carry
        r = pl.multiple_of(i * 8, 8)
        cv = ct_ref[pl.ds(r, 8), :]
        sv = cv * jax.nn.sigmoid(cv)
        w8 = w_ref[pl.ds(r, 8), :]
        return a0 + w8 * sv[:, 0:1], a1 + w8 * sv[:, 1:2]

    z = jnp.zeros((8, tn), f32)
    a0, a1 = lax.fori_loop(0, w_ref.shape[0] // 8, body, (z, z))
    r0 = jnp.sum(a0, axis=0, keepdims=True) + b_ref[...]
    r1 = jnp.sum(a1, axis=0, keepdims=True) + b_ref[...]
    o_ref[...] = jnp.concatenate([r0, r1, jnp.zeros((6, tn), f32)], axis=0)


def _modulation(ct, w_ada, b_ada):
    depth, d, n = w_ada.shape
    tn = 1536
    return pl.pallas_call(
        _mod_kernel,
        grid=(depth, n // tn),
        in_specs=[
            pl.BlockSpec((d, LANES), lambda l, j: (0, 0)),
            pl.BlockSpec((None, d, tn), lambda l, j: (l, 0, j)),
            pl.BlockSpec((None, 1, tn), lambda l, j: (l, 0, j)),
        ],
        out_specs=pl.BlockSpec((None, 8, tn), lambda l, j: (l, 0, j)),
        out_shape=jax.ShapeDtypeStruct((depth, 8, n), f32),
        compiler_params=_cparams(("arbitrary", "arbitrary")),
        name="modulation",
    )(ct, w_ada, b_ada.reshape(depth, 1, n))


def _rmsnorm_mod(xf, g, shift, scale):
    y = xf * lax.rsqrt(jnp.mean(xf * xf, axis=-1, keepdims=True) + EPS) * g
    return y * (1.0 + scale) + shift


def _inproj_kernel(*refs, rope):
    if rope:
        x_ref, sh_ref, sc_ref, g_ref, w_ref, cos_ref, sin_ref = refs[:7]
        outs = refs[7:]
    else:
        x_ref, sh_ref, sc_ref, g_ref, w_ref = refs[:5]
        outs = refs[5:]
    qa_ref, ka_ref, va_ref, qb_ref, kb_ref, vb_ref, z_ref = outs

    h = _rmsnorm_mod(x_ref[...], g_ref[...], sh_ref[...], sc_ref[...])
    p = jnp.dot(h.astype(bf16), w_ref[...], preferred_element_type=f32)
    tm = p.shape[0]
    lane = lax.broadcasted_iota(jnp.int32, (tm, LANES), 1)
    lo_half = lane < HD

    def rot(xc, scale):
        if rope:
            first = (lane % 32) < 16
            partner = jnp.where(first, pltpu.roll(xc, LANES - 16, 1), pltpu.roll(xc, 16, 1))
            xc = xc * cos_ref[...] + partner * sin_ref[...]
        return xc * scale if scale != 1.0 else xc

    def chunk(off, j):
        return p[:, off + j * LANES: off + (j + 1) * LANES]

    ones = jnp.ones((VT_ROWS - VD_A, tm), f32)
    for hh in range(HEADS_A):
        qa_ref[hh] = rot(chunk(OFF_QA, hh), QK_SCALE_LOG2).T.astype(bf16)
        ka_ref[hh] = rot(chunk(OFF_KA, hh), 1.0).astype(bf16)
        va_ref[hh] = jnp.concatenate([chunk(OFF_VA, hh).T, ones], axis=0).astype(bf16)
    for g in range(KV_HEADS_B):
        qb_ref[g] = rot(chunk(OFF_QB, g), QK_SCALE).astype(bf16)
    kb = rot(chunk(OFF_KB, 0), 1.0)
    kb_sw = pltpu.roll(kb, HD, 1)
    kb_ref[0] = jnp.where(lo_half, kb, kb_sw).astype(bf16)
    kb_ref[1] = jnp.where(lo_half, kb_sw, kb).astype(bf16)
    vb = chunk(OFF_VB, 0)
    vb_sw = pltpu.roll(vb, HD, 1)
    zero = jnp.zeros_like(vb)
    vb_ref[0, 0] = jnp.where(lo_half, vb, zero).astype(bf16)
    vb_ref[0, 1] = jnp.where(lo_half, zero, vb_sw).astype(bf16)
    vb_ref[1, 0] = jnp.where(lo_half, vb_sw, zero).astype(bf16)
    vb_ref[1, 1] = jnp.where(lo_half, zero, vb).astype(bf16)
    a = p[:, OFF_UC: OFF_UC + CONV_CH]
    gt = p[:, OFF_UC + CONV_CH: OFF_UC + 2 * CONV_CH]
    z_ref[...] = a * jax.nn.sigmoid(gt)


def _inproj(x2, shift, scale, g, w_bf, cos_t, sin_t, *, tm, tk):
    n, d = x2.shape
    rope = cos_t is not None
    per = tk // tm
    row = lambda i: (i, 0)
    const = lambda i: (0, 0)
    in_specs = [
        pl.BlockSpec((tm, d), row),
        pl.BlockSpec((1, d), const),
        pl.BlockSpec((1, d), const),
        pl.BlockSpec((1, d), const),
        pl.BlockSpec((d, IN_WIDTH), const),
    ]
    args = [x2, shift, scale, g, w_bf]
    if rope:
        in_specs += [pl.BlockSpec((tm, LANES), row), pl.BlockSpec((tm, LANES), row)]
        args += [cos_t, sin_t]
    out_shape = (
        jax.ShapeDtypeStruct((HEADS_A, LANES, n), bf16),
        jax.ShapeDtypeStruct((HEADS_A, n, LANES), bf16),
        jax.ShapeDtypeStruct((HEADS_A, n // tk, VT_ROWS, tk), bf16),
        jax.ShapeDtypeStruct((KV_HEADS_B, n, LANES), bf16),
        jax.ShapeDtypeStruct((KV_HEADS_B, n, LANES), bf16),
        jax.ShapeDtypeStruct((KV_HEADS_B, 2, n, LANES), bf16),
        jax.ShapeDtypeStruct((n, CONV_CH), f32),
    )
    out_specs = (
        pl.BlockSpec((HEADS_A, LANES, tm), lambda i: (0, 0, i)),
        pl.BlockSpec((HEADS_A, tm, LANES), lambda i: (0, i, 0)),
        pl.BlockSpec((HEADS_A, None, VT_ROWS, tm), lambda i: (0, i // per, 0, i % per)),
        pl.BlockSpec((KV_HEADS_B, tm, LANES), lambda i: (0, i, 0)),
        pl.BlockSpec((KV_HEADS_B, tm, LANES), lambda i: (0, i, 0)),
        pl.BlockSpec((KV_HEADS_B, 2, tm, LANES), lambda i: (0, 0, i, 0)),
        pl.BlockSpec((tm, CONV_CH), row),
    )
    return pl.pallas_call(
        functools.partial(_inproj_kernel, rope=rope),
        grid=(n // tm,),
        in_specs=in_specs,
        out_specs=out_specs,
        out_shape=out_shape,
        compiler_params=_cparams(("parallel",)),
        name="inproj_rope" if rope else "inproj_ctx",
    )(*args)


def _diff_attn_kernel(*refs, tq, tk, n_main, lambda_init):
    if n_main:
        (dl_ref, g_ref, q_ref, kl_ref, vl_ref, kc_ref, vc_ref, o_ref,
         m_scr, acc_scr, s_buf0, s_buf1, mc_buf0, mc_buf1) = refs
    else:
        dl_ref, g_ref, q_ref, kc_ref, vc_ref, o_ref, m_scr, acc_scr = refs

    qt = q_ref[...]
    sub = lax.broadcasted_iota(jnp.int32, qt.shape, 0)
    zero = jnp.zeros_like(qt)
    qq = jnp.concatenate([jnp.where(sub < HD, qt, zero), jnp.where(sub < HD, zero, qt)], axis=1)
    m_scr[...] = jnp.full(m_scr.shape, NEG_INF, f32)
    acc_scr[...] = jnp.zeros(acc_scr.shape, f32)

    def scores(k):
        return jnp.dot(k, qq, preferred_element_type=f32)

    def softmax_pv(s, m_chunk, vt):
        m_old = m_scr[...]
        m_new = jnp.maximum(m_old, m_chunk)
        alpha = jnp.exp2(m_old - m_new)
        p = jnp.exp2(s - m_new).astype(bf16)
        acc_scr[...] = alpha * acc_scr[...] + jnp.dot(vt, p, preferred_element_type=f32)
        m_scr[...] = m_new

    s_ctx = scores(kc_ref[...])
    softmax_pv(s_ctx, jnp.max(s_ctx, axis=0, keepdims=True), vc_ref[...])

    if n_main:
        def stage_scores(c, s_buf, mc_buf):
            r = pl.multiple_of(c * tk, tk)
            s = scores(kl_ref[pl.ds(r, tk), :])
            s_buf[...] = s
            mc_buf[...] = jnp.max(s, axis=0, keepdims=True)

        def stage_softmax(c, s_buf, mc_buf):
            softmax_pv(s_buf[...], mc_buf[...], vl_ref[c])

        stage_scores(0, s_buf0, mc_buf0)

        def body(j, carry):
            c = 2 * j
            stage_scores(c + 1, s_buf1, mc_buf1)
            stage_softmax(c, s_buf0, mc_buf0)
            stage_scores(c + 2, s_buf0, mc_buf0)
            stage_softmax(c + 1, s_buf1, mc_buf1)
            return carry

        lax.fori_loop(0, n_main // 2 - 1, body, 0)
        stage_scores(n_main - 1, s_buf1, mc_buf1)
        stage_softmax(n_main - 2, s_buf0, mc_buf0)
        stage_softmax(n_main - 1, s_buf1, mc_buf1)

    acc = acc_scr[...]
    o0 = acc[:VD_A, :tq] / acc[VD_A:VD_A + 1, :tq]
    o1 = acc[:VD_A, tq:] / acc[VD_A:VD_A + 1, tq:]
    dl = dl_ref[...]
    lam = (jnp.exp(jnp.sum(dl[0:1] * dl[1:2], axis=1, keepdims=True))
           - jnp.exp(jnp.sum(dl[2:3] * dl[3:4], axis=1, keepdims=True)) + lambda_init)
    o = o0 - lam * o1
    y = o * lax.rsqrt(jnp.mean(o * o, axis=0, keepdims=True) + EPS) * g_ref[...]
    o_ref[...] = (y * (1.0 - lambda_init)).T.astype(o_ref.dtype)


def _diff_attention(dl, g_col, qa_t, ka_lat, va_lat, ka_ctx, va_ctx, *, lambda_init, tq, tk):
    nq = qa_t.shape[2]
    c = ka_ctx.shape[1]
    assert va_ctx.shape[1] == 1
    n_main = 0 if ka_lat is None else ka_lat.shape[1] // tk
    in_specs = [
        pl.BlockSpec((4, HD), lambda h, i: (0, 0)),
        pl.BlockSpec((VD_A, 1), lambda h, i: (0, 0)),
        pl.BlockSpec((None, LANES, tq), lambda h, i: (h, 0, i)),
    ]
    args = [dl, g_col, qa_t]
    if n_main:
        s_keys = ka_lat.shape[1]
        assert va_lat.shape[1:] == (n_main, VT_ROWS, tk)
        in_specs += [pl.BlockSpec((None, s_keys, LANES), lambda h, i: (h, 0, 0)),
                     pl.BlockSpec((None, n_main, VT_ROWS, tk), lambda h, i: (h, 0, 0, 0))]
        args += [ka_lat, va_lat]
    in_specs += [pl.BlockSpec((None, c, LANES), lambda h, i: (h, 0, 0)),
                 pl.BlockSpec((None, None, VT_ROWS, c), lambda h, i: (h, 0, 0, 0))]
    args += [ka_ctx, va_ctx]
    scratch = [pltpu.VMEM((1, 2 * tq), f32), pltpu.VMEM((VT_ROWS, 2 * tq), f32)]
    if n_main:
        assert n_main % 2 == 0 and n_main >= 2
        scratch += [pltpu.VMEM((tk, 2 * tq), f32), pltpu.VMEM((tk, 2 * tq), f32),
                    pltpu.VMEM((1, 2 * tq), f32), pltpu.VMEM((1, 2 * tq), f32)]
    return pl.pallas_call(
        functools.partial(_diff_attn_kernel, tq=tq, tk=tk, n_main=n_main, lambda_init=lambda_init),
        grid=(HEADS_A, nq // tq),
        in_specs=in_specs,
        out_specs=pl.BlockSpec((tq, VD_A), lambda h, i: (i, h)),
        out_shape=jax.ShapeDtypeStruct((nq, WIDTH_A), bf16),
        scratch_shapes=scratch,
        compiler_params=_cparams(("arbitrary", "arbitrary")),
        name="diff_attn" if n_main else "diff_attn_ctx",
    )(*args)


def _win_attn_kernel(*refs, tq, band, s_len):
    if band:
        (sink_ref, q_ref, kp_ref, ks_ref, kn_ref, kc_ref,
         vp_ref, vs_ref, vn_ref, vc_ref, o_ref) = refs
    else:
        sink_ref, q_ref, kc_ref, vc_ref, o_ref = refs
    i = pl.program_id(0)
    lane = lax.broadcasted_iota(jnp.int32, (tq, LANES), 1)
    if band:
        nloc = tq + 2 * WINDOW
        qpos = i * tq + lax.broadcasted_iota(jnp.int32, (tq, nloc), 0)
        kpos = i * tq - WINDOW + lax.broadcasted_iota(jnp.int32, (tq, nloc), 1)
        valid = (jnp.abs(qpos - kpos) <= WINDOW) & (kpos >= 0) & (kpos < s_len)
    nt = (((1,), (1,)), ((), ()))
    for g in range(KV_HEADS_B):
        q = q_ref[g]
        kc = kc_ref[g]
        if band:
            kloc = jnp.concatenate([kp_ref[g], ks_ref[g], kn_ref[g]], axis=0)
        o = jnp.zeros((tq, LANES), f32)
        for r in range(2):
            sink = sink_ref[2 * g + r]
            qm = jnp.where((lane < HD) if r == 0 else (lane >= HD), q, jnp.zeros_like(q))
            s_ctx = lax.dot_general(qm, kc, nt, preferred_element_type=f32)
            m = jnp.maximum(jnp.max(s_ctx, axis=1, keepdims=True), sink)
            if band:
                s_loc = lax.dot_general(qm, kloc, nt, preferred_element_type=f32)
                s_loc = jnp.where(valid, s_loc, NEG_INF)
                m = jnp.maximum(m, jnp.max(s_loc, axis=1, keepdims=True))
            e_ctx = jnp.exp(s_ctx - m)
            l = jnp.sum(e_ctx, axis=1, keepdims=True) + jnp.exp(sink - m)
            if band:
                e_loc = jnp.exp(s_loc - m)
                l = l + jnp.sum(e_loc, axis=1, keepdims=True)
            o = o + jnp.dot((e_ctx / l).astype(bf16), vc_ref[g, r], preferred_element_type=f32)
            if band:
                vloc = jnp.concatenate([vp_ref[g, r], vs_ref[g, r], vn_ref[g, r]], axis=0)
                o = o + jnp.dot((e_loc / l).astype(bf16), vloc, preferred_element_type=f32)
        o_ref[:, g * LANES:(g + 1) * LANES] = o.astype(o_ref.dtype)


def _win_attention(sink, qb, kb_lat, vb_lat, kb_ctx, vb_ctx, *, tq):
    nq = qb.shape[1]
    c = kb_ctx.shape[1]
    band = kb_lat is not None
    smem = pl.BlockSpec(memory_space=pltpu.SMEM)
    q_spec = pl.BlockSpec((KV_HEADS_B, tq, LANES), lambda i: (0, i, 0))
    kc_spec = pl.BlockSpec((KV_HEADS_B, c, LANES), lambda i: (0, 0, 0))
    vc_spec = pl.BlockSpec((KV_HEADS_B, 2, c, LANES), lambda i: (0, 0, 0, 0))
    if band:
        per = tq // WINDOW
        last = nq // WINDOW - 1
        prev = lambda i: jnp.maximum(i * per - 1, 0)
        nxt = lambda i: jnp.minimum((i + 1) * per, last)
        in_specs = [
            smem, q_spec,
            pl.BlockSpec((KV_HEADS_B, WINDOW, LANES), lambda i: (0, prev(i), 0)),
            pl.BlockSpec((KV_HEADS_B, tq, LANES), lambda i: (0, i, 0)),
            pl.BlockSpec((KV_HEADS_B, WINDOW, LANES), lambda i: (0, nxt(i), 0)),
            kc_spec,
            pl.BlockSpec((KV_HEADS_B, 2, WINDOW, LANES), lambda i: (0, 0, prev(i), 0)),
            pl.BlockSpec((KV_HEADS_B, 2, tq, LANES), lambda i: (0, 0, i, 0)),
            pl.BlockSpec((KV_HEADS_B, 2, WINDOW, LANES), lambda i: (0, 0, nxt(i), 0)),
            vc_spec,
        ]
        args = [sink, qb, kb_lat, kb_lat, kb_lat, kb_ctx, vb_lat, vb_lat, vb_lat, vb_ctx]
    else:
        in_specs = [smem, q_spec, kc_spec, vc_spec]
        args = [sink, qb, kb_ctx, vb_ctx]
    return pl.pallas_call(
        functools.partial(_win_attn_kernel, tq=tq, band=band, s_len=nq),
        grid=(nq // tq,),
        in_specs=in_specs,
        out_specs=pl.BlockSpec((tq, WIDTH_B), lambda i: (i, 0)),
        out_shape=jax.ShapeDtypeStruct((nq, WIDTH_B), bf16),
        compiler_params=_cparams(("parallel",)),
        name="win_attn" if band else "win_attn_ctx",
    )(*args)


def _conv_kernel(zp_ref, zs_ref, zn_ref, w_ref, b_ref, lg_ref, lb_ref, o_ref, zbuf, *, tm, rc):
    i = pl.program_id(0)
    n = pl.num_programs(0)
    halo = CONV_HALO
    zero = jnp.zeros((halo, CONV_CH), f32)
    zbuf[0:halo, :] = jnp.where(i > 0, zp_ref[...], zero)
    zbuf[halo:halo + tm, :] = zs_ref[...]
    zbuf[halo + tm:2 * halo + tm, :] = jnp.where(i < n - 1, zn_ref[...], zero)
    w = w_ref[...]
    base = halo - CONV_K // 2

    for r0 in range(0, tm, rc):
        acc = jnp.zeros((rc, CONV_CH), f32)
        for k in range(CONV_K):
            acc = acc + zbuf[r0 + base + k: r0 + base + k + rc, :] * w[k:k + 1, :]
        zc = acc + b_ref[...]
        mu = jnp.mean(zc, axis=-1, keepdims=True)
        dz = zc - mu
        var = jnp.mean(dz * dz, axis=-1, keepdims=True)
        zn = dz * lax.rsqrt(var + EPS) * lg_ref[...] + lb_ref[...]
        o_ref[r0:r0 + rc, :] = (zn * jax.nn.sigmoid(zn)).astype(o_ref.dtype)


def _conv_module(z, w, b, ln_g, ln_b, *, tm):
    n = z.shape[0]
    per = tm // CONV_HALO
    last = n // CONV_HALO - 1
    vec = lambda i: (0, 0)
    return pl.pallas_call(
        functools.partial(_conv_kernel, tm=tm, rc=32),
        grid=(n // tm,),
        in_specs=[
            pl.BlockSpec((CONV_HALO, CONV_CH), lambda i: (jnp.maximum(i * per - 1, 0), 0)),
            pl.BlockSpec((tm, CONV_CH), lambda i: (i, 0)),
            pl.BlockSpec((CONV_HALO, CONV_CH), lambda i: (jnp.minimum((i + 1) * per, last), 0)),
            pl.BlockSpec((CONV_K, CONV_CH), vec),
            pl.BlockSpec((1, CONV_CH), vec),
            pl.BlockSpec((1, CONV_CH), vec),
            pl.BlockSpec((1, CONV_CH), vec),
        ],
        out_specs=pl.BlockSpec((tm, CONV_CH), lambda i: (i, 0)),
        out_shape=jax.ShapeDtypeStruct((n, CONV_CH), bf16),
        scratch_shapes=[pltpu.VMEM((tm + 2 * CONV_HALO, CONV_CH), f32)],
        compiler_params=_cparams(("parallel",)),
        name="conv_module",
    )(z, z, z, w, b, ln_g, ln_b)


def _split_bf16(a):
    hi = a.astype(bf16)
    lo = (a - hi.astype(f32)).astype(bf16)
    return hi, lo


def _route(lt, bias_ref):
    s = [jax.nn.sigmoid(lt[e:e + 1, :]) for e in range(N_EXPERTS)]
    b = [s[e] + bias_ref[e] for e in range(N_EXPERTS)]
    grp = []
    for g in range(N_GROUPS):
        v = b[4 * g:4 * g + 4]
        best = None
        for a_i in range(4):
            for b_i in range(a_i + 1, 4):
                t = v[a_i] + v[b_i]
                best = t if best is None else jnp.maximum(best, t)
        grp.append(best)
    sel = jnp.zeros_like(lt[0:1, :], dtype=jnp.int32)
    gbest = grp[0]
    for g in range(1, N_GROUPS):
        better = grp[g] > gbest
        sel = jnp.where(better, g, sel)
        gbest = jnp.where(better, grp[g], gbest)

    def pick(rows, j):
        out = rows[j]
        for g in range(1, N_GROUPS):
            out = jnp.where(sel == g, rows[4 * g + j], out)
        return out

    vb = [pick(b, j) for j in range(4)]
    vs = [pick(s, j) for j in range(4)]
    i1 = jnp.zeros_like(sel)
    b1 = vb[0]
    for j in range(1, 4):
        better = vb[j] > b1
        i1 = jnp.where(better, j, i1)
        b1 = jnp.where(better, vb[j], b1)
    i2 = jnp.full_like(sel, -1)
    b2 = jnp.full_like(b1, -jnp.inf)
    for j in range(4):
        better = (i1 != j) & ((i2 < 0) | (vb[j] > b2))
        i2 = jnp.where(better, j, i2)
        b2 = jnp.where(better, vb[j], b2)
    s1 = vs[0]
    s2 = vs[0]
    for j in range(1, 4):
        s1 = jnp.where(i1 == j, vs[j], s1)
        s2 = jnp.where(i2 == j, vs[j], s2)
    tot = s1 + s2
    w1 = s1 / tot
    w2 = s2 / tot
    e1 = sel * EXPERTS_PER_GROUP + i1
    e2 = sel * EXPERTS_PER_GROUP + i2
    rows = []
    zero = jnp.zeros_like(w1)
    for e in range(N_EXPERTS):
        rows.append(jnp.where(e1 == e, w1, zero) + jnp.where(e2 == e, w2, zero))
    return jnp.concatenate(rows, axis=0)


def _outproj_kernel(bias_ref, x_ref, oa_ref, ob_ref, oc_ref, w_ref, gate_ref, sh_ref, sc_ref, g_ref,
                    rwt_ref, xo_ref, h2_ref, comb_ref):
    y = jnp.dot(oa_ref[...], w_ref[0:WIDTH_A, :], preferred_element_type=f32)
    y = y + jnp.dot(ob_ref[...], w_ref[WIDTH_A:WIDTH_A + WIDTH_B, :], preferred_element_type=f32)
    y = y + jnp.dot(oc_ref[...], w_ref[WIDTH_A + WIDTH_B:, :], preferred_element_type=f32)
    xn = x_ref[...] + gate_ref[...] * y
    xo_ref[...] = xn
    h2 = _rmsnorm_mod(xn, g_ref[...], sh_ref[...], sc_ref[...])
    h2_ref[...] = h2.astype(bf16)
    nt = (((1,), (1,)), ((), ()))
    h_hi, h_lo = _split_bf16(h2)
    r_hi, r_lo = _split_bf16(rwt_ref[...])
    lt = (lax.dot_general(r_hi, h_hi, nt, preferred_element_type=f32)
          + lax.dot_general(r_hi, h_lo, nt, preferred_element_type=f32)
          + lax.dot_general(r_lo, h_hi, nt, preferred_element_type=f32))
    comb_ref[...] = _route(lt, bias_ref)


def _outproj(router_bias, x2, oa, ob, oc, w_bf, gate, shift, scale, g, rwt, *, tm):
    n, d = x2.shape
    row = lambda i: (i, 0)
    const = lambda i: (0, 0)
    return pl.pallas_call(
        _outproj_kernel,
        grid=(n // tm,),
        in_specs=[
            pl.BlockSpec(memory_space=pltpu.SMEM),
            pl.BlockSpec((tm, d), row),
            pl.BlockSpec((tm, WIDTH_A), row),
            pl.BlockSpec((tm, WIDTH_B), row),
            pl.BlockSpec((tm, CONV_CH), row),
            pl.BlockSpec((d, d), const),
            pl.BlockSpec((1, d), const),
            pl.BlockSpec((1, d), const),
            pl.BlockSpec((1, d), const),
            pl.BlockSpec((1, d), const),
            pl.BlockSpec((N_EXPERTS, d), const),
        ],
        out_specs=(
            pl.BlockSpec((tm, d), row),
            pl.BlockSpec((tm, d), row),
            pl.BlockSpec((N_EXPERTS, tm), lambda i: (0, i)),
        ),
        out_shape=(
            jax.ShapeDtypeStruct((n, d), f32),
            jax.ShapeDtypeStruct((n, d), bf16),
            jax.ShapeDtypeStruct((N_EXPERTS, n), f32),
        ),
        compiler_params=_cparams(("parallel",)),
        name="outproj_router",
    )(router_bias, x2, oa, ob, oc, w_bf, gate, shift, scale, g, rwt)


def _moe_kernel(*refs, final):
    if final:
        x_ref, h_ref, comb_ref, w1_ref, w3_ref, w2_ref, gate_ref, gf_ref, o_ref, acc = refs
    else:
        x_ref, h_ref, comb_ref, w1_ref, w3_ref, w2_ref, gate_ref, o_ref, acc = refs
    e = pl.program_id(1)

    @pl.when(e == 0)
    def _():
        acc[...] = jnp.zeros(acc.shape, f32)

    h = h_ref[...]
    a = jnp.dot(h, w1_ref[...], preferred_element_type=f32)
    b = jnp.dot(h, w3_ref[...], preferred_element_type=f32)
    u = (a * jax.nn.sigmoid(a) * b).astype(bf16)
    y = jnp.dot(u, w2_ref[...], preferred_element_type=f32)
    comb = comb_ref[...]
    lane = lax.broadcasted_iota(jnp.int32, comb.shape, 1)
    cw = jnp.sum(jnp.where(lane == e, comb, 0.0), axis=1, keepdims=True)
    acc[...] += cw * y

    @pl.when(e == pl.num_programs(1) - 1)
    def _():
        xn = x_ref[...] + gate_ref[...] * acc[...]
        if final:
            xn = xn * lax.rsqrt(jnp.mean(xn * xn, axis=-1, keepdims=True) + EPS) * gf_ref[...]
        o_ref[...] = xn


def _moe(x2, h2, comb, w1, w3, w2, gate, g_final, *, tm):
    n, d = x2.shape
    final = g_final is not None
    row = lambda i, e: (i, 0)
    const = lambda i, e: (0, 0)
    in_specs = [
        pl.BlockSpec((tm, d), row),
        pl.BlockSpec((tm, d), row),
        pl.BlockSpec((tm, N_EXPERTS), row),
        pl.BlockSpec((None, d, EXPERT_FF), lambda i, e: (e, 0, 0)),
        pl.BlockSpec((None, d, EXPERT_FF), lambda i, e: (e, 0, 0)),
        pl.BlockSpec((None, EXPERT_FF, d), lambda i, e: (e, 0, 0)),
        pl.BlockSpec((1, d), const),
    ]
    args = [x2, h2, comb, w1, w3, w2, gate]
    if final:
        in_specs.append(pl.BlockSpec((1, d), const))
        args.append(g_final)
    return pl.pallas_call(
        functools.partial(_moe_kernel, final=final),
        grid=(n // tm, N_EXPERTS),
        in_specs=in_specs,
        out_specs=pl.BlockSpec((tm, d), row),
        out_shape=jax.ShapeDtypeStruct((n, d), f32),
        scratch_shapes=[pltpu.VMEM((tm, d), f32)],
        compiler_params=_cparams(("parallel", "arbitrary")),
        name="moe_final" if final else "moe",
    )(*args)


def _rope_tables(n_tok):
    rows = n_tok // GRID_W
    row = jnp.repeat(jnp.arange(rows, dtype=f32), GRID_W)
    col = jnp.tile(jnp.arange(GRID_W, dtype=f32), rows)
    n_freq = HD // 4
    inv = ROPE_BASE ** (-jnp.arange(n_freq, dtype=f32) / n_freq)
    lane = jnp.arange(LANES)
    j = lane % HD
    use_col = (j // 32) == 1
    freq = inv[j % n_freq]
    pos = jnp.where(use_col[None, :], col[:, None], row[:, None])
    ang = pos * freq[None, :]
    sign = jnp.where((lane % 32) < 16, -1.0, 1.0).astype(f32)
    return jnp.cos(ang), jnp.sin(ang) * sign[None, :]


def kernel(x, c, ctx, c_ctx, w_ada, b_ada, g_mix, w_in, diff_lambda, diff_norm_g, attn_sink, conv_w, conv_b,
           conv_ln_g, conv_ln_b, w_out, g_ffn, router_w, router_bias, w1, w3, w2, g_final):
    bsz, s_len, d = x.shape
    c_len = ctx.shape[1]
    depth = w_ada.shape[0]
    assert bsz == 1 and d == D_MODEL and s_len % 1024 == 0 and c_len % 256 == 0
    tm_lat = 256
    tm_ctx = min(c_len, 256)
    tm_moe = 512
    tq_a, tk_a = 256, 1024

    xl = x.reshape(s_len, d)
    xc = ctx.reshape(c_len, d)
    ct = jnp.zeros((d, LANES), f32).at[:, 0].set(c[0]).at[:, 1].set(c_ctx)
    mod_all = _modulation(ct, w_ada, b_ada)
    cos_t, sin_t = _rope_tables(s_len)
    rwt = router_w.T
    w_in_bf = w_in.astype(bf16)
    w_out_bf = w_out.astype(bf16)
    w1_bf, w3_bf, w2_bf = w1.astype(bf16), w3.astype(bf16), w2.astype(bf16)
    vec = lambda a: a.reshape(1, -1)

    for l in range(depth):
        last = l == depth - 1
        lambda_init = 0.8 - 0.6 * math.exp(-0.3 * l)
        ml = [mod_all[l, 0:1, k * d:(k + 1) * d] for k in range(6)]
        mc = [mod_all[l, 1:2, k * d:(k + 1) * d] for k in range(6)]
        g_mix_l, g_ffn_l = vec(g_mix[l]), vec(g_ffn[l])
        dl, dng = diff_lambda[l], diff_norm_g[l].reshape(-1, 1)

        qa, ka, va, qb, kb, vb, z = _inproj(xl, ml[0], ml[1], g_mix_l, w_in_bf[l], cos_t, sin_t, tm=tm_lat,
                                            tk=tk_a)
        qa_c, ka_c, va_c, qb_c, kb_c, vb_c, z_c = _inproj(xc, mc[0], mc[1], g_mix_l, w_in_bf[l], None, None,
                                                          tm=tm_ctx, tk=c_len)

        o_a = _diff_attention(dl, dng, qa, ka, va, ka_c, va_c, lambda_init=lambda_init, tq=tq_a, tk=tk_a)
        o_b = _win_attention(attn_sink[l], qb, kb, vb, kb_c, vb_c, tq=256)
        o_c = _conv_module(z, conv_w[l], vec(conv_b[l]), vec(conv_ln_g[l]), vec(conv_ln_b[l]), tm=tm_lat)
        xl, h2, comb = _outproj(router_bias, xl, o_a, o_b, o_c, w_out_bf[l], ml[2], ml[3], ml[4], g_ffn_l, rwt,
                                tm=tm_lat)
        xl = _moe(xl, h2, comb.T, w1_bf[l], w3_bf[l], w2_bf[l], ml[5], vec(g_final) if last else None,
                  tm=tm_moe)

        if not last:
            o_ac = _diff_attention(dl, dng, qa_c, None, None, ka_c, va_c, lambda_init=lambda_init,
                                   tq=tm_ctx, tk=tk_a)
            o_bc = _win_attention(attn_sink[l], qb_c, None, None, kb_c, vb_c, tq=tm_ctx)
            o_cc = _conv_module(z_c, conv_w[l], vec(conv_b[l]), vec(conv_ln_g[l]), vec(conv_ln_b[l]), tm=tm_ctx)
            xc, h2c, comb_c = _outproj(router_bias, xc, o_ac, o_bc, o_cc, w_out_bf[l], mc[2], mc[3], mc[4],
                                       g_ffn_l, rwt, tm=tm_ctx)
            xc = _moe(xc, h2c, comb_c.T, w1_bf[l], w3_bf[l], w2_bf[l], mc[5], None, tm=tm_ctx)

    return xl.reshape(bsz, s_len, d)
```

```python
import functools
import math

import jax
import jax.numpy as jnp
from jax import lax
from jax.experimental import pallas as pl
from jax.experimental.pallas import tpu as pltpu

f32 = jnp.float32
bf16 = jnp.bfloat16

D_MODEL = 1024
GRID_W = 64
HEADS_A = 4
HD = 64
VD_A = 2 * HD
VT_ROWS = VD_A + 16
WIDTH_A = HEADS_A * VD_A
HEADS_B = 4
KV_HEADS_B = 2
WIDTH_B = HEADS_B * HD
WINDOW = 128
CONV_CH = 256
CONV_K = 31
CONV_HALO = 16
IN_WIDTH = 2560
OFF_QA, OFF_KA, OFF_VA, OFF_QB, OFF_KB, OFF_VB, OFF_UC = 0, 512, 1024, 1536, 1792, 1920, 2048
N_EXPERTS = 16
N_GROUPS = 4
EXPERTS_PER_GROUP = 4
EXPERT_FF = 512
ROPE_BASE = 10000.0
EPS = 1e-6
NEG_INF = -1e30
LANES = 128
QK_SCALE = HD ** -0.5
QK_SCALE_LOG2 = QK_SCALE * math.log2(math.e)

MOE_WINDOW = 1024
MOE_BLOCK_ROWS = 160
PIPE_UNROLL = 4
VMEM_LIMIT =56 * 1024 * 1024


def _cparams(sem):
    return pltpu.CompilerParams(dimension_semantics=sem, vmem_limit_bytes=VMEM_LIMIT)


def _mod_kernel(ct_ref, w_ref, b_ref, o_ref):
    tn = w_ref.shape[1]

    def body(i, carry):
        a0, a1 = carry
        r = pl.multiple_of(i * 8, 8)
        cv = ct_ref[pl.ds(r, 8), :]
        sv = cv * jax.nn.sigmoid(cv)
        w8 = w_ref[pl.ds(r, 8), :]
        return a0 + w8 * sv[:, 0:1], a1 + w8 * sv[:, 1:2]

    z = jnp.zeros((8, tn), f32)
    a0, a1 = lax.fori_loop(0, w_ref.shape[0] // 8, body, (z, z))
    r0 = jnp.sum(a0, axis=0, keepdims=True) + b_ref[...]
    r1 = jnp.sum(a1, axis=0, keepdims=True) + b_ref[...]
    o_ref[...] = jnp.concatenate([r0, r1, jnp.zeros((6, tn), f32)], axis=0)


def _modulation(ct, w_ada, b_ada):
    depth, d, n = w_ada.shape
    tn = 1536
    return pl.pallas_call(
        _mod_kernel,
        grid=(depth, n // tn),
        in_specs=[
            pl.BlockSpec((d, LANES), lambda l, j: (0, 0)),
            pl.BlockSpec((None, d, tn), lambda l, j: (l, 0, j)),
            pl.BlockSpec((None, 1, tn), lambda l, j: (l, 0, j)),
        ],
        out_specs=pl.BlockSpec((None, 8, tn), lambda l, j: (l, 0, j)),
        out_shape=jax.ShapeDtypeStruct((depth, 8, n), f32),
        compiler_params=_cparams(("arbitrary", "arbitrary")),
        name="modulation",
    )(ct, w_ada, b_ada.reshape(depth, 1, n))


def _rmsnorm_mod(xf, g, shift, scale):
    y = xf * lax.rsqrt(jnp.mean(xf * xf, axis=-1, keepdims=True) + EPS) * g
    return y * (1.0 + scale) + shift


def _inproj_kernel(*refs, rope):
    if rope:
        x_ref, sh_ref, sc_ref, g_ref, w_ref, cos_ref, sin_ref = refs[:7]
        outs = refs[7:]
    else:
        x_ref, sh_ref, sc_ref, g_ref, w_ref = refs[:5]
        outs = refs[5:]
    qa_ref, ka_ref, va_ref, qb_ref, kb_ref, vb_ref, z_ref = outs

    h = _rmsnorm_mod(x_ref[...], g_ref[...], sh_ref[...], sc_ref[...])
    p = jnp.dot(h.astype(bf16), w_ref[...], preferred_element_type=f32)
    tm = p.shape[0]
    lane = lax.broadcasted_iota(jnp.int32, (tm, LANES), 1)
    lo_half = lane < HD

    def rot(xc, scale):
        if rope:
            first = (lane % 32) < 16
            partner = jnp.where(first, pltpu.roll(xc, LANES - 16, 1), pltpu.roll(xc, 16, 1))
            xc = xc * cos_ref[...] + partner * sin_ref[...]
        return xc * scale if scale != 1.0 else xc

    def chunk(off, j):
        return p[:, off + j * LANES: off + (j + 1) * LANES]

    ones = jnp.ones((VT_ROWS - VD_A, tm), f32)
    for hh in range(HEADS_A):
        qa_ref[hh] = rot(chunk(OFF_QA, hh), QK_SCALE_LOG2).T.astype(bf16)
        ka_ref[hh] = rot(chunk(OFF_KA, hh), 1.0).astype(bf16)
        va_ref[hh] = jnp.concatenate([chunk(OFF_VA, hh).T, ones], axis=0).astype(bf16)
    for g in range(KV_HEADS_B):
        qb_ref[g] = rot(chunk(OFF_QB, g), QK_SCALE).astype(bf16)
    kb = rot(chunk(OFF_KB, 0), 1.0)
    kb_sw = pltpu.roll(kb, HD, 1)
    kb_ref[0] = jnp.where(lo_half, kb, kb_sw).astype(bf16)
    kb_ref[1] = jnp.where(lo_half, kb_sw, kb).astype(bf16)
    vb = chunk(OFF_VB, 0)
    vb_sw = pltpu.roll(vb, HD, 1)
    zero = jnp.zeros_like(vb)
    vb_ref[0, 0] = jnp.where(lo_half, vb, zero).astype(bf16)
    vb_ref[0, 1] = jnp.where(lo_half, zero, vb_sw).astype(bf16)
    vb_ref[1, 0] = jnp.where(lo_half, vb_sw, zero).astype(bf16)
    vb_ref[1, 1] = jnp.where(lo_half, zero, vb).astype(bf16)
    a = p[:, OFF_UC: OFF_UC + CONV_CH]
    gt = p[:, OFF_UC + CONV_CH: OFF_UC + 2 * CONV_CH]
    z_ref[...] = a * jax.nn.sigmoid(gt)


def _inproj(x2, shift, scale, g, w_bf, cos_t, sin_t, *, tm, tk):
    n, d = x2.shape
    rope = cos_t is not None
    per = tk // tm
    row = lambda i: (i, 0)
    const = lambda i: (0, 0)
    in_specs = [
        pl.BlockSpec((tm, d), row),
        pl.BlockSpec((1, d), const),
        pl.BlockSpec((1, d), const),
        pl.BlockSpec((1, d), const),
        pl.BlockSpec((d, IN_WIDTH), const),
    ]
    args = [x2, shift, scale, g, w_bf]
    if rope:
        in_specs += [pl.BlockSpec((tm, LANES), row), pl.BlockSpec((tm, LANES), row)]
        args += [cos_t, sin_t]
    out_shape = (
        jax.ShapeDtypeStruct((HEADS_A, LANES, n), bf16),
        jax.ShapeDtypeStruct((HEADS_A, n, LANES), bf16),
        jax.ShapeDtypeStruct((HEADS_A, n // tk, VT_ROWS, tk), bf16),
        jax.ShapeDtypeStruct((KV_HEADS_B, n, LANES), bf16),
        jax.ShapeDtypeStruct((KV_HEADS_B, n, LANES), bf16),
        jax.ShapeDtypeStruct((KV_HEADS_B, 2, n, LANES), bf16),
        jax.ShapeDtypeStruct((n, CONV_CH), f32),
    )
    out_specs = (
        pl.BlockSpec((HEADS_A, LANES, tm), lambda i: (0, 0, i)),
        pl.BlockSpec((HEADS_A, tm, LANES), lambda i: (0, i, 0)),
        pl.BlockSpec((HEADS_A, None, VT_ROWS, tm), lambda i: (0, i // per, 0, i % per)),
        pl.BlockSpec((KV_HEADS_B, tm, LANES), lambda i: (0, i, 0)),
        pl.BlockSpec((KV_HEADS_B, tm, LANES), lambda i: (0, i, 0)),
        pl.BlockSpec((KV_HEADS_B, 2, tm, LANES), lambda i: (0, 0, i, 0)),
        pl.BlockSpec((tm, CONV_CH), row),
    )
    return pl.pallas_call(
        functools.partial(_inproj_kernel, rope=rope),
        grid=(n // tm,),
        in_specs=in_specs,
        out_specs=out_specs,
        out_shape=out_shape,
        compiler_params=_cparams(("parallel",)),
        name="inproj_rope" if rope else "inproj_ctx",
    )(*args)


def _diff_attn_kernel(*refs, tq, tk, n_main, lambda_init):
    if n_main:
        (dl_ref, g_ref, q_ref, kl_ref, vl_ref, kc_ref, vc_ref, o_ref,
         m_scr, acc_scr, s_buf0, s_buf1, mc_buf0, mc_buf1) = refs
    else:
        dl_ref, g_ref, q_ref, kc_ref, vc_ref, o_ref, m_scr, acc_scr = refs

    qt = q_ref[...]
    sub = lax.broadcasted_iota(jnp.int32, qt.shape, 0)
    zero = jnp.zeros_like(qt)
    qq = jnp.concatenate([jnp.where(sub < HD, qt, zero), jnp.where(sub < HD, zero, qt)], axis=1)
    m_scr[...] = jnp.full(m_scr.shape, NEG_INF, f32)
    acc_scr[...] = jnp.zeros(acc_scr.shape, f32)

    def scores(k):
        return jnp.dot(k, qq, preferred_element_type=f32)

    def softmax_pv(s, m_chunk, vt):
        m_old = m_scr[...]
        m_new = jnp.maximum(m_old, m_chunk)
        alpha = jnp.exp2(m_old - m_new)
        p = jnp.exp2(s - m_new).astype(bf16)
        acc_scr[...] = alpha * acc_scr[...] + jnp.dot(vt, p, preferred_element_type=f32)
        m_scr[...] = m_new

    s_ctx = scores(kc_ref[...])
    if not n_main:
        softmax_pv(s_ctx, jnp.max(s_ctx, axis=0, keepdims=True), vc_ref[...])
    else:
        def stage_scores(c, s_buf, mc_buf):
            r = pl.multiple_of(c * tk, tk)
            s = scores(kl_ref[pl.ds(r, tk), :])
            s_buf[...] = s
            mc_buf[...] = jnp.max(s, axis=0, keepdims=True)

        def stage_softmax(c, s_buf, mc_buf):
            softmax_pv(s_buf[...], mc_buf[...], vl_ref[c])

        bufs = ((s_buf0, mc_buf0), (s_buf1, mc_buf1))
        unroll = PIPE_UNROLL
        stage_scores(0, *bufs[0])
        softmax_pv(s_ctx, jnp.max(s_ctx, axis=0, keepdims=True), vc_ref[...])

        def body(j, carry):
            c = unroll * j
            for u in range(unroll):
                stage_scores(c + u + 1, *bufs[(u + 1) % 2])
                stage_softmax(c + u, *bufs[u % 2])
            return carry

        lax.fori_loop(0, n_main // unroll - 1, body, 0)
        for c in range(n_main - unroll, n_main):
            if c + 1 < n_main:
                stage_scores(c + 1, *bufs[(c + 1) % 2])
            stage_softmax(c, *bufs[c % 2])

    acc = acc_scr[...]
    o0 = acc[:VD_A, :tq] / acc[VD_A:VD_A + 1, :tq]
    o1 = acc[:VD_A, tq:] / acc[VD_A:VD_A + 1, tq:]
    dl = dl_ref[...]
    lam = (jnp.exp(jnp.sum(dl[0:1] * dl[1:2], axis=1, keepdims=True))
           - jnp.exp(jnp.sum(dl[2:3] * dl[3:4], axis=1, keepdims=True)) + lambda_init)
    o = o0 - lam * o1
    y = o * lax.rsqrt(jnp.mean(o * o, axis=0, keepdims=True) + EPS) * g_ref[...]
    o_ref[...] = (y * (1.0 - lambda_init)).T.astype(o_ref.dtype)


def _diff_attention(dl, g_col, qa_t, ka_lat, va_lat, ka_ctx, va_ctx, *, lambda_init, tq, tk):
    nq = qa_t.shape[2]
    c = ka_ctx.shape[1]
    assert va_ctx.shape[1] == 1
    n_main = 0 if ka_lat is None else ka_lat.shape[1] // tk
    in_specs = [
        pl.BlockSpec((4, HD), lambda h, i: (0, 0)),
        pl.BlockSpec((VD_A, 1), lambda h, i: (0, 0)),
        pl.BlockSpec((None, LANES, tq), lambda h, i: (h, 0, i)),
    ]
    args = [dl, g_col, qa_t]
    if n_main:
        s_keys = ka_lat.shape[1]
        assert va_lat.shape[1:] == (n_main, VT_ROWS, tk)
        in_specs += [pl.BlockSpec((None, s_keys, LANES), lambda h, i: (h, 0, 0)),
                     pl.BlockSpec((None, n_main, VT_ROWS, tk), lambda h, i: (h, 0, 0, 0))]
        args += [ka_lat, va_lat]
    in_specs += [pl.BlockSpec((None, c, LANES), lambda h, i: (h, 0, 0)),
                 pl.BlockSpec((None, None, VT_ROWS, c), lambda h, i: (h, 0, 0, 0))]
    args += [ka_ctx, va_ctx]
    scratch = [pltpu.VMEM((1, 2 * tq), f32), pltpu.VMEM((VT_ROWS, 2 * tq), f32)]
    if n_main:
        assert n_main % PIPE_UNROLL == 0
        scratch += [pltpu.VMEM((tk, 2 * tq), f32), pltpu.VMEM((tk, 2 * tq), f32),
                    pltpu.VMEM((1, 2 * tq), f32), pltpu.VMEM((1, 2 * tq), f32)]
    return pl.pallas_call(
        functools.partial(_diff_attn_kernel, tq=tq, tk=tk, n_main=n_main, lambda_init=lambda_init),
        grid=(HEADS_A, nq // tq),
        in_specs=in_specs,
        out_specs=pl.BlockSpec((tq, VD_A), lambda h, i: (i, h)),
        out_shape=jax.ShapeDtypeStruct((nq, WIDTH_A), bf16),
        scratch_shapes=scratch,
        compiler_params=_cparams(("arbitrary", "arbitrary")),
        name="diff_attn" if n_main else "diff_attn_ctx",
    )(*args)


def _win_attn_kernel(*refs, tq, band, s_len):
    if band:
        (sink_ref, q_ref, kp_ref, ks_ref, kn_ref, kc_ref,
         vp_ref, vs_ref, vn_ref, vc_ref, o_ref) = refs
    else:
        sink_ref, q_ref, kc_ref, vc_ref, o_ref = refs
    i = pl.program_id(0)
    lane = lax.broadcasted_iota(jnp.int32, (tq, LANES), 1)
    if band:
        nloc = tq + 2 * WINDOW
        qpos = i * tq + lax.broadcasted_iota(jnp.int32, (tq, nloc), 0)
        kpos = i * tq - WINDOW + lax.broadcasted_iota(jnp.int32, (tq, nloc), 1)
        valid = (jnp.abs(qpos - kpos) <= WINDOW) & (kpos >= 0) & (kpos < s_len)
    nt = (((1,), (1,)), ((), ()))
    for g in range(KV_HEADS_B):
        q = q_ref[g]
        kc = kc_ref[g]
        if band:
            kloc = jnp.concatenate([kp_ref[g], ks_ref[g], kn_ref[g]], axis=0)
        o = jnp.zeros((tq, LANES), f32)
        for r in range(2):
            sink = sink_ref[2 * g + r]
            qm = jnp.where((lane < HD) if r == 0 else (lane >= HD), q, jnp.zeros_like(q))
            s_ctx = lax.dot_general(qm, kc, nt, preferred_element_type=f32)
            m = jnp.maximum(jnp.max(s_ctx, axis=1, keepdims=True), sink)
            if band:
                s_loc = lax.dot_general(qm, kloc, nt, preferred_element_type=f32)
                s_loc = jnp.where(valid, s_loc, NEG_INF)
                m = jnp.maximum(m, jnp.max(s_loc, axis=1, keepdims=True))
            e_ctx = jnp.exp(s_ctx - m)
            l = jnp.sum(e_ctx, axis=1, keepdims=True) + jnp.exp(sink - m)
            if band:
                e_loc = jnp.exp(s_loc - m)
                l = l + jnp.sum(e_loc, axis=1, keepdims=True)
            o = o + jnp.dot((e_ctx / l).astype(bf16), vc_ref[g, r], preferred_element_type=f32)
            if band:
                vloc = jnp.concatenate([vp_ref[g, r], vs_ref[g, r], vn_ref[g, r]], axis=0)
                o = o + jnp.dot((e_loc / l).astype(bf16), vloc, preferred_element_type=f32)
        o_ref[:, g * LANES:(g + 1) * LANES] = o.astype(o_ref.dtype)


def _win_attention(sink, qb, kb_lat, vb_lat, kb_ctx, vb_ctx, *, tq):
    nq = qb.shape[1]
    c = kb_ctx.shape[1]
    band = kb_lat is not None
    smem = pl.BlockSpec(memory_space=pltpu.SMEM)
    q_spec = pl.BlockSpec((KV_HEADS_B, tq, LANES), lambda i: (0, i, 0))
    kc_spec = pl.BlockSpec((KV_HEADS_B, c, LANES), lambda i: (0, 0, 0))
    vc_spec = pl.BlockSpec((KV_HEADS_B, 2, c, LANES), lambda i: (0, 0, 0, 0))
    if band:
        per = tq // WINDOW
        last = nq // WINDOW - 1
        prev = lambda i: jnp.maximum(i * per - 1, 0)
        nxt = lambda i: jnp.minimum((i + 1) * per, last)
        in_specs = [
            smem, q_spec,
            pl.BlockSpec((KV_HEADS_B, WINDOW, LANES), lambda i: (0, prev(i), 0)),
            pl.BlockSpec((KV_HEADS_B, tq, LANES), lambda i: (0, i, 0)),
            pl.BlockSpec((KV_HEADS_B, WINDOW, LANES), lambda i: (0, nxt(i), 0)),
            kc_spec,
            pl.BlockSpec((KV_HEADS_B, 2, WINDOW, LANES), lambda i: (0, 0, prev(i), 0)),
            pl.BlockSpec((KV_HEADS_B, 2, tq, LANES), lambda i: (0, 0, i, 0)),
            pl.BlockSpec((KV_HEADS_B, 2, WINDOW, LANES), lambda i: (0, 0, nxt(i), 0)),
            vc_spec,
        ]
        args = [sink, qb, kb_lat, kb_lat, kb_lat, kb_ctx, vb_lat, vb_lat, vb_lat, vb_ctx]
    else:
        in_specs = [smem, q_spec, kc_spec, vc_spec]
        args = [sink, qb, kb_ctx, vb_ctx]
    return pl.pallas_call(
        functools.partial(_win_attn_kernel, tq=tq, band=band, s_len=nq),
        grid=(nq // tq,),
        in_specs=in_specs,
        out_specs=pl.BlockSpec((tq, WIDTH_B), lambda i: (i, 0)),
        out_shape=jax.ShapeDtypeStruct((nq, WIDTH_B), bf16),
        compiler_params=_cparams(("parallel",)),
        name="win_attn" if band else "win_attn_ctx",
    )(*args)


def _conv_kernel(zp_ref, zs_ref, zn_ref, w_ref, b_ref, lg_ref, lb_ref, o_ref, zbuf, *, tm, rc):
    i = pl.program_id(0)
    n = pl.num_programs(0)
    halo = CONV_HALO
    zero = jnp.zeros((halo, CONV_CH), f32)
    zbuf[0:halo, :] = jnp.where(i > 0, zp_ref[...], zero)
    zbuf[halo:halo + tm, :] = zs_ref[...]
    zbuf[halo + tm:2 * halo + tm, :] = jnp.where(i < n - 1, zn_ref[...], zero)
    w = w_ref[...]
    base = halo - CONV_K // 2

    for r0 in range(0, tm, rc):
        acc = jnp.zeros((rc, CONV_CH), f32)
        for k in range(CONV_K):
            acc = acc + zbuf[r0 + base + k: r0 + base + k + rc, :] * w[k:k + 1, :]
        zc = acc + b_ref[...]
        mu = jnp.mean(zc, axis=-1, keepdims=True)
        dz = zc - mu
        var = jnp.mean(dz * dz, axis=-1, keepdims=True)
        zn = dz * lax.rsqrt(var + EPS) * lg_ref[...] + lb_ref[...]
        o_ref[r0:r0 + rc, :] = (zn * jax.nn.sigmoid(zn)).astype(o_ref.dtype)


def _conv_module(z, w, b, ln_g, ln_b, *, tm):
    n = z.shape[0]
    per = tm // CONV_HALO
    last = n // CONV_HALO - 1
    vec = lambda i: (0, 0)
    return pl.pallas_call(
        functools.partial(_conv_kernel, tm=tm, rc=32),
        grid=(n // tm,),
        in_specs=[
            pl.BlockSpec((CONV_HALO, CONV_CH), lambda i: (jnp.maximum(i * per - 1, 0), 0)),
            pl.BlockSpec((tm, CONV_CH), lambda i: (i, 0)),
            pl.BlockSpec((CONV_HALO, CONV_CH), lambda i: (jnp.minimum((i + 1) * per, last), 0)),
            pl.BlockSpec((CONV_K, CONV_CH), vec),
            pl.BlockSpec((1, CONV_CH), vec),
            pl.BlockSpec((1, CONV_CH), vec),
            pl.BlockSpec((1, CONV_CH), vec),
        ],
        out_specs=pl.BlockSpec((tm, CONV_CH), lambda i: (i, 0)),
        out_shape=jax.ShapeDtypeStruct((n, CONV_CH), bf16),
        scratch_shapes=[pltpu.VMEM((tm + 2 * CONV_HALO, CONV_CH), f32)],
        compiler_params=_cparams(("parallel",)),
        name="conv_module",
    )(z, z, z, w, b, ln_g, ln_b)


def _split_bf16(a):
    hi = a.astype(bf16)
    lo = (a - hi.astype(f32)).astype(bf16)
    return hi, lo


def _route(lt, bias_ref):
    s = [jax.nn.sigmoid(lt[e:e + 1, :]) for e in range(N_EXPERTS)]
    b = [s[e] + bias_ref[e] for e in range(N_EXPERTS)]
    grp = []
    for g in range(N_GROUPS):
        v = b[4 * g:4 * g + 4]
        best = None
        for a_i in range(4):
            for b_i in range(a_i + 1, 4):
                t = v[a_i] + v[b_i]
                best = t if best is None else jnp.maximum(best, t)
        grp.append(best)
    sel = jnp.zeros_like(lt[0:1, :], dtype=jnp.int32)
    gbest = grp[0]
    for g in range(1, N_GROUPS):
        better = grp[g] > gbest
        sel = jnp.where(better, g, sel)
        gbest = jnp.where(better, grp[g], gbest)

    def pick(rows, j):
        out = rows[j]
        for g in range(1, N_GROUPS):
            out = jnp.where(sel == g, rows[4 * g + j], out)
        return out

    vb = [pick(b, j) for j in range(4)]
    vs = [pick(s, j) for j in range(4)]
    i1 = jnp.zeros_like(sel)
    b1 = vb[0]
    for j in range(1, 4):
        better = vb[j] > b1
        i1 = jnp.where(better, j, i1)
        b1 = jnp.where(better, vb[j], b1)
    i2 = jnp.full_like(sel, -1)
    b2 = jnp.full_like(b1, -jnp.inf)
    for j in range(4):
        better = (i1 != j) & ((i2 < 0) | (vb[j] > b2))
        i2 = jnp.where(better, j, i2)
        b2 = jnp.where(better, vb[j], b2)
    s1 = vs[0]
    s2 = vs[0]
    for j in range(1, 4):
        s1 = jnp.where(i1 == j, vs[j], s1)
        s2 = jnp.where(i2 == j, vs[j], s2)
    tot = s1 + s2
    w1 = s1 / tot
    w2 = s2 / tot
    e1 = sel * EXPERTS_PER_GROUP + i1
    e2 = sel * EXPERTS_PER_GROUP + i2
    rows, asg = [], []
    zero = jnp.zeros_like(w1)
    one = jnp.ones_like(w1)
    for e in range(N_EXPERTS):
        rows.append(jnp.where(e1 == e, w1, zero) + jnp.where(e2 == e, w2, zero))
        asg.append(jnp.where((e1 == e) | (e2 == e), one, zero))
    return jnp.concatenate(rows, axis=0), jnp.concatenate(asg, axis=0)


def _outproj_kernel(bias_ref, x_ref, oa_ref, ob_ref, oc_ref, w_ref, gate_ref, sh_ref, sc_ref, g_ref,
                    rwt_ref, xo_ref, h2_ref, comb_ref, rank_ref, cnt_ref):
    y = jnp.dot(oa_ref[...], w_ref[0:WIDTH_A, :], preferred_element_type=f32)
    y = y + jnp.dot(ob_ref[...], w_ref[WIDTH_A:WIDTH_A + WIDTH_B, :], preferred_element_type=f32)
    y = y + jnp.dot(oc_ref[...], w_ref[WIDTH_A + WIDTH_B:, :], preferred_element_type=f32)
    xn = x_ref[...] + gate_ref[...] * y
    xo_ref[...] = xn
    h2 = _rmsnorm_mod(xn, g_ref[...], sh_ref[...], sc_ref[...])
    h2_ref[...] = h2.astype(bf16)
    nt = (((1,), (1,)), ((), ()))
    h_hi, h_lo = _split_bf16(h2)
    r_hi, r_lo = _split_bf16(rwt_ref[...])
    lt = (lax.dot_general(r_hi, h_hi, nt, preferred_element_type=f32)
          + lax.dot_general(r_hi, h_lo, nt, preferred_element_type=f32)
          + lax.dot_general(r_lo, h_hi, nt, preferred_element_type=f32))
    comb, asg = _route(lt, bias_ref)
    tm = lt.shape[1]
    upper = (lax.broadcasted_iota(jnp.int32, (tm, tm), 0) < lax.broadcasted_iota(jnp.int32, (tm, tm), 1))
    rank = jnp.dot(asg.astype(bf16), upper.astype(bf16), preferred_element_type=f32)
    rank = jnp.where(asg > 0.0, rank, -1.0)
    for e in range(N_EXPERTS):
        comb_ref[e] = comb[e:e + 1, :]
        rank_ref[e] = rank[e:e + 1, :]
    cnt = jnp.sum(asg, axis=1, keepdims=True).astype(jnp.int32)
    cnt_ref[...] = jnp.broadcast_to(cnt, cnt_ref.shape)


def _outproj(router_bias, x2, oa, ob, oc, w_bf, gate, shift, scale, g, rwt, *, tm):
    n, d = x2.shape
    row = lambda i: (i, 0)
    const = lambda i: (0, 0)
    return pl.pallas_call(
        _outproj_kernel,
        grid=(n // tm,),
        in_specs=[
            pl.BlockSpec(memory_space=pltpu.SMEM),
            pl.BlockSpec((tm, d), row),
            pl.BlockSpec((tm, WIDTH_A), row),
            pl.BlockSpec((tm, WIDTH_B), row),
            pl.BlockSpec((tm, CONV_CH), row),
            pl.BlockSpec((d, d), const),
            pl.BlockSpec((1, d), const),
            pl.BlockSpec((1, d), const),
            pl.BlockSpec((1, d), const),
            pl.BlockSpec((1, d), const),
            pl.BlockSpec((N_EXPERTS, d), const),
        ],
        out_specs=(
            pl.BlockSpec((tm, d), row),
            pl.BlockSpec((tm, d), row),
            pl.BlockSpec((N_EXPERTS, 1, tm), lambda i: (0, 0, i)),
            pl.BlockSpec((N_EXPERTS, 1, tm), lambda i: (0, 0, i)),
            pl.BlockSpec((None, N_EXPERTS, LANES), lambda i: (i, 0, 0)),
        ),
        out_shape=(
            jax.ShapeDtypeStruct((n, d), f32),
            jax.ShapeDtypeStruct((n, d), bf16),
            jax.ShapeDtypeStruct((N_EXPERTS, 1, n), f32),
            jax.ShapeDtypeStruct((N_EXPERTS, 1, n), f32),
            jax.ShapeDtypeStruct((n // tm, N_EXPERTS, LANES), jnp.int32),
        ),
        compiler_params=_cparams(("parallel",)),
        name="outproj_router",
    )(router_bias, x2, oa, ob, oc, w_bf, gate, shift, scale, g, rwt)


def _moe_kernel(*refs, final, rows):
    if final:
        nblk_ref, x_ref, h_ref, comb_ref, rank_ref, w1_ref, w3_ref, w2_ref, gate_ref, gf_ref, o_ref, acc = refs
    else:
        nblk_ref, x_ref, h_ref, comb_ref, rank_ref, w1_ref, w3_ref, w2_ref, gate_ref, o_ref, acc = refs
    w = pl.program_id(0)
    e = pl.program_id(1)

    @pl.when(e == 0)
    def _():
        acc[...] = jnp.zeros(acc.shape, f32)

    comb = comb_ref[...]
    rank = rank_ref[...]
    win = comb.shape[1]
    slot0 = lax.broadcasted_iota(jnp.int32, (rows, win), 0).astype(f32)

    def block(b, carry):
        match = rank == slot0 + (b * rows).astype(f32)
        onehot = jnp.where(match, 1.0, 0.0).astype(bf16)
        cw = jnp.sum(jnp.where(match, comb, 0.0), axis=1, keepdims=True)
        xe = jnp.dot(onehot, h_ref[...], preferred_element_type=f32).astype(bf16)
        a = jnp.dot(xe, w1_ref[...], preferred_element_type=f32)
        g = jnp.dot(xe, w3_ref[...], preferred_element_type=f32)
        u = (a * jax.nn.sigmoid(a) * g).astype(bf16)
        y = jnp.dot(u, w2_ref[...], preferred_element_type=f32)
        yw = (cw * y).astype(bf16)
        acc[...] += lax.dot_general(onehot, yw, (((0,), (0,)), ((), ())), preferred_element_type=f32)
        return carry

    lax.fori_loop(0, nblk_ref[w * N_EXPERTS + e], block, 0)

    @pl.when(e == pl.num_programs(1) - 1)
    def _():
        xn = x_ref[...] + gate_ref[...] * acc[...]
        if final:
            xn = xn * lax.rsqrt(jnp.mean(xn * xn, axis=-1, keepdims=True) + EPS) * gf_ref[...]
        o_ref[...] = xn


def _moe(x2, h2, comb, rank, counts, w1, w3, w2, gate, g_final, *, win):
    n, d = x2.shape
    final = g_final is not None
    rows = MOE_BLOCK_ROWS
    nblk = ((counts + (rows - 1)) // rows).reshape(-1).astype(jnp.int32)
    row = lambda i, e, nb: (i, 0)
    const = lambda i, e, nb: (0, 0)
    per_expert = lambda i, e, nb: (e, 0, i)
    weight = lambda i, e, nb: (e, 0, 0)
    in_specs = [
        pl.BlockSpec((win, d), row),
        pl.BlockSpec((win, d), row),
        pl.BlockSpec((None, 1, win), per_expert),
        pl.BlockSpec((None, 1, win), per_expert),
        pl.BlockSpec((None, d, EXPERT_FF), weight),
        pl.BlockSpec((None, d, EXPERT_FF), weight),
        pl.BlockSpec((None, EXPERT_FF, d), weight),
        pl.BlockSpec((1, d), const),
    ]
    args = [x2, h2, comb, rank, w1, w3, w2, gate]
    if final:
        in_specs.append(pl.BlockSpec((1, d), const))
        args.append(g_final)
    return pl.pallas_call(
        functools.partial(_moe_kernel, final=final, rows=rows),
        grid_spec=pltpu.PrefetchScalarGridSpec(
            num_scalar_prefetch=1,
            grid=(n // win, N_EXPERTS),
            in_specs=in_specs,
            out_specs=pl.BlockSpec((win, d), row),
            scratch_shapes=[pltpu.VMEM((win, d), f32)],
        ),
        out_shape=jax.ShapeDtypeStruct((n, d), f32),
        compiler_params=_cparams(("parallel", "arbitrary")),
        name="moe_final" if final else "moe",
    )(nblk, *args)


def _rope_tables(n_tok):
    rows = n_tok // GRID_W
    row = jnp.repeat(jnp.arange(rows, dtype=f32), GRID_W)
    col = jnp.tile(jnp.arange(GRID_W, dtype=f32), rows)
    n_freq = HD // 4
    inv = ROPE_BASE ** (-jnp.arange(n_freq, dtype=f32) / n_freq)
    lane = jnp.arange(LANES)
    j = lane % HD
    use_col = (j // 32) == 1
    freq = inv[j % n_freq]
    pos = jnp.where(use_col[None, :], col[:, None], row[:, None])
    ang = pos * freq[None, :]
    sign = jnp.where((lane % 32) < 16, -1.0, 1.0).astype(f32)
    return jnp.cos(ang), jnp.sin(ang) * sign[None, :]


def kernel(x, c, ctx, c_ctx, w_ada, b_ada, g_mix, w_in, diff_lambda, diff_norm_g, attn_sink, conv_w, conv_b,
           conv_ln_g, conv_ln_b, w_out, g_ffn, router_w, router_bias, w1, w3, w2, g_final):
    bsz, s_len, d = x.shape
    c_len = ctx.shape[1]
    depth = w_ada.shape[0]
    assert bsz == 1 and d == D_MODEL and s_len % 1024 == 0 and c_len % 256 == 0
    tm_lat = 256
    tm_ctx = min(c_len, 256)
    win_lat = MOE_WINDOW
    tq_a, tk_a = 256, 1024

    xl = x.reshape(s_len, d)
    xc = ctx.reshape(c_len, d)
    ct = jnp.zeros((d, LANES), f32).at[:, 0].set(c[0]).at[:, 1].set(c_ctx)
    mod_all = _modulation(ct, w_ada, b_ada)
    cos_t, sin_t = _rope_tables(s_len)
    rwt = router_w.T
    w_in_bf = w_in.astype(bf16)
    w_out_bf = w_out.astype(bf16)
    w1_bf, w3_bf, w2_bf = w1.astype(bf16), w3.astype(bf16), w2.astype(bf16)
    vec = lambda a: a.reshape(1, -1)

    for l in range(depth):
        last = l == depth - 1
        lambda_init = 0.8 - 0.6 * math.exp(-0.3 * l)
        ml = [mod_all[l, 0:1, k * d:(k + 1) * d] for k in range(6)]
        mc = [mod_all[l, 1:2, k * d:(k + 1) * d] for k in range(6)]
        g_mix_l, g_ffn_l = vec(g_mix[l]), vec(g_ffn[l])
        dl, dng = diff_lambda[l], diff_norm_g[l].reshape(-1, 1)

        qa, ka, va, qb, kb, vb, z = _inproj(xl, ml[0], ml[1], g_mix_l, w_in_bf[l], cos_t, sin_t, tm=tm_lat,
                                            tk=tk_a)
        qa_c, ka_c, va_c, qb_c, kb_c, vb_c, z_c = _inproj(xc, mc[0], mc[1], g_mix_l, w_in_bf[l], None, None,
                                                          tm=tm_ctx, tk=c_len)

        o_a = _diff_attention(dl, dng, qa, ka, va, ka_c, va_c, lambda_init=lambda_init, tq=tq_a, tk=tk_a)
        o_b = _win_attention(attn_sink[l], qb, kb, vb, kb_c, vb_c, tq=256)
        o_c = _conv_module(z, conv_w[l], vec(conv_b[l]), vec(conv_ln_g[l]), vec(conv_ln_b[l]), tm=tm_lat)
        xl, h2, comb, rank, cnt = _outproj(router_bias, xl, o_a, o_b, o_c, w_out_bf[l], ml[2], ml[3], ml[4],
                                           g_ffn_l, rwt, tm=win_lat)
        xl = _moe(xl, h2, comb, rank, cnt[:, :, 0], w1_bf[l], w3_bf[l], w2_bf[l], ml[5],
                  vec(g_final) if last else None, win=win_lat)

        if not last:
            o_ac = _diff_attention(dl, dng, qa_c, None, None, ka_c, va_c, lambda_init=lambda_init,
                                   tq=tm_ctx, tk=tk_a)
            o_bc = _win_attention(attn_sink[l], qb_c, None, None, kb_c, vb_c, tq=tm_ctx)
            o_cc = _conv_module(z_c, conv_w[l], vec(conv_b[l]), vec(conv_ln_g[l]), vec(conv_ln_b[l]), tm=tm_ctx)
            xc, h2c, comb_c, rank_c, cnt_c = _outproj(router_bias, xc, o_ac, o_bc, o_cc, w_out_bf[l], mc[2], mc[3],
                                                      mc[4], g_ffn_l, rwt, tm=tm_ctx)
            xc = _moe(xc, h2c, comb_c, rank_c, cnt_c[:, :, 0], w1_bf[l], w3_bf[l], w2_bf[l], mc[5], None,
                      win=tm_ctx)

    return xl.reshape(bsz, s_len, d)
```

```python
import functools
import math

import numpy as np
import jax
import jax.numpy as jnp
from jax import lax
from jax.experimental import pallas as pl
from jax.experimental.pallas import tpu as pltpu

f32 = jnp.float32
bf16 = jnp.bfloat16

D_MODEL = 1024
GRID_W = 64
HEADS_A = 4
HD = 64
VD_A = 2 * HD
VT_ROWS = VD_A + 16
WIDTH_A = HEADS_A * VD_A
HEADS_B = 4
KV_HEADS_B = 2
WIDTH_B = HEADS_B * HD
WINDOW = 128
CONV_CH = 256
CONV_K = 31
CONV_HALO = 16
IN_WIDTH = 2560
OFF_QA, OFF_KA, OFF_VA, OFF_QB, OFF_KB, OFF_VB, OFF_UC = 0, 512, 1024, 1536, 1792, 1920, 2048
N_EXPERTS = 16
N_GROUPS = 4
EXPERTS_PER_GROUP = 4
EXPERT_FF = 512
ROPE_BASE = 10000.0
EPS = 1e-6
NEG_INF = -1e30
LANES = 128
SUBLANES = 8
QK_SCALE = HD ** -0.5
LOG2_E = math.log2(math.e)
QK_SCALE_LOG2 = QK_SCALE * LOG2_E

MOE_WINDOW = 1024
MOE_BLOCK_ROWS = 160
PIPE_UNROLL = 4
VMEM_LIMIT =56 * 1024 * 1024


def _cparams(sem):
    return pltpu.CompilerParams(dimension_semantics=sem, vmem_limit_bytes=VMEM_LIMIT)


def _mod_kernel(ct_ref, w_ref, b_ref, o_ref):
    tn = w_ref.shape[1]

    def body(i, carry):
        a0, a1 = carry
        r = pl.multiple_of(i * 8, 8)
        cv = ct_ref[pl.ds(r, 8), :]
        sv = cv * jax.nn.sigmoid(cv)
        w8 = w_ref[pl.ds(r, 8), :]
        return a0 + w8 * sv[:, 0:1], a1 + w8 * sv[:, 1:2]

    z = jnp.zeros((8, tn), f32)
    a0, a1 = lax.fori_loop(0, w_ref.shape[0] // 8, body, (z, z), unroll=8)
    r0 = jnp.sum(a0, axis=0, keepdims=True) + b_ref[...]
    r1 = jnp.sum(a1, axis=0, keepdims=True) + b_ref[...]
    o_ref[...] = jnp.concatenate([r0, r1, jnp.zeros((6, tn), f32)], axis=0)


def _modulation(ct, w_ada, b_ada):
    depth, d, n = w_ada.shape
    tn = 1536
    return pl.pallas_call(
        _mod_kernel,
        grid=(depth, n // tn),
        in_specs=[
            pl.BlockSpec((d, LANES), lambda l, j: (0, 0)),
            pl.BlockSpec((None, d, tn), lambda l, j: (l, 0, j)),
            pl.BlockSpec((None, 1, tn), lambda l, j: (l, 0, j)),
        ],
        out_specs=pl.BlockSpec((None, 8, tn), lambda l, j: (l, 0, j)),
        out_shape=jax.ShapeDtypeStruct((depth, 8, n), f32),
        compiler_params=_cparams(("arbitrary", "arbitrary")),
        name="modulation",
    )(ct, w_ada, b_ada.reshape(depth, 1, n))


def _rmsnorm_mod(xf, g, shift, scale):
    y = xf * lax.rsqrt(jnp.mean(xf * xf, axis=-1, keepdims=True) + EPS) * g
    return y * (1.0 + scale) + shift


def _inproj_kernel(*refs, rope):
    if rope:
        x_ref, sh_ref, sc_ref, g_ref, w_ref, cos_ref, sin_ref = refs[:7]
        outs = refs[7:]
    else:
        x_ref, sh_ref, sc_ref, g_ref, w_ref = refs[:5]
        outs = refs[5:]
    qa_ref, ka_ref, va_ref, qb_ref, kb_ref, vb_ref, z_ref = outs

    h = _rmsnorm_mod(x_ref[...], g_ref[...], sh_ref[...], sc_ref[...])
    p = jnp.dot(h.astype(bf16), w_ref[...], preferred_element_type=f32)
    tm = p.shape[0]
    lane = lax.broadcasted_iota(jnp.int32, (tm, LANES), 1)
    lo_half = lane < HD

    def rot(xc, scale):
        if rope:
            first = (lane % 32) < 16
            partner = jnp.where(first, pltpu.roll(xc, LANES - 16, 1), pltpu.roll(xc, 16, 1))
            xc = xc * cos_ref[...] + partner * sin_ref[...]
        return xc * scale if scale != 1.0 else xc

    def chunk(off, j):
        return p[:, off + j * LANES: off + (j + 1) * LANES]

    ones = jnp.ones((VT_ROWS - VD_A, tm), f32)
    for hh in range(HEADS_A):
        qa_ref[hh] = rot(chunk(OFF_QA, hh), QK_SCALE_LOG2).T.astype(bf16)
        ka_ref[hh] = rot(chunk(OFF_KA, hh), 1.0).astype(bf16)
        va_ref[hh] = jnp.concatenate([chunk(OFF_VA, hh).T, ones], axis=0).astype(bf16)
    for g in range(KV_HEADS_B):
        qb_ref[g] = rot(chunk(OFF_QB, g), QK_SCALE_LOG2).astype(bf16)
    kb = rot(chunk(OFF_KB, 0), 1.0)
    kb_sw = pltpu.roll(kb, HD, 1)
    kb_ref[0] = jnp.where(lo_half, kb, kb_sw).astype(bf16)
    kb_ref[1] = jnp.where(lo_half, kb_sw, kb).astype(bf16)
    vb = chunk(OFF_VB, 0)
    vb_sw = pltpu.roll(vb, HD, 1)
    zero = jnp.zeros_like(vb)
    vb_ref[0, 0] = jnp.where(lo_half, vb, zero).astype(bf16)
    vb_ref[0, 1] = jnp.where(lo_half, zero, vb_sw).astype(bf16)
    vb_ref[1, 0] = jnp.where(lo_half, vb_sw, zero).astype(bf16)
    vb_ref[1, 1] = jnp.where(lo_half, zero, vb).astype(bf16)
    a = p[:, OFF_UC: OFF_UC + CONV_CH]
    gt = p[:, OFF_UC + CONV_CH: OFF_UC + 2 * CONV_CH]
    z_ref[...] = a * jax.nn.sigmoid(gt)


def _inproj(x2, shift, scale, g, w_bf, layer, cos_t, sin_t, *, tm, tk):
    n, d = x2.shape
    rope = cos_t is not None
    per = tk // tm
    row = lambda i: (i, 0)
    const = lambda i: (0, 0)
    in_specs = [
        pl.BlockSpec((tm, d), row),
        pl.BlockSpec((1, d), const),
        pl.BlockSpec((1, d), const),
        pl.BlockSpec((1, d), const),
        pl.BlockSpec((None, d, IN_WIDTH), lambda i: (layer, 0, 0)),
    ]
    args = [x2, shift, scale, g, w_bf]
    if rope:
        in_specs += [pl.BlockSpec((tm, LANES), row), pl.BlockSpec((tm, LANES), row)]
        args += [cos_t, sin_t]
    out_shape = (
        jax.ShapeDtypeStruct((HEADS_A, LANES, n), bf16),
        jax.ShapeDtypeStruct((HEADS_A, n, LANES), bf16),
        jax.ShapeDtypeStruct((HEADS_A, n // tk, VT_ROWS, tk), bf16),
        jax.ShapeDtypeStruct((KV_HEADS_B, n, LANES), bf16),
        jax.ShapeDtypeStruct((KV_HEADS_B, n, LANES), bf16),
        jax.ShapeDtypeStruct((KV_HEADS_B, 2, n, LANES), bf16),
        jax.ShapeDtypeStruct((n, CONV_CH), f32),
    )
    out_specs = (
        pl.BlockSpec((HEADS_A, LANES, tm), lambda i: (0, 0, i)),
        pl.BlockSpec((HEADS_A, tm, LANES), lambda i: (0, i, 0)),
        pl.BlockSpec((HEADS_A, None, VT_ROWS, tm), lambda i: (0, i // per, 0, i % per)),
        pl.BlockSpec((KV_HEADS_B, tm, LANES), lambda i: (0, i, 0)),
        pl.BlockSpec((KV_HEADS_B, tm, LANES), lambda i: (0, i, 0)),
        pl.BlockSpec((KV_HEADS_B, 2, tm, LANES), lambda i: (0, 0, i, 0)),
        pl.BlockSpec((tm, CONV_CH), row),
    )
    return pl.pallas_call(
        functools.partial(_inproj_kernel, rope=rope),
        grid=(n // tm,),
        in_specs=in_specs,
        out_specs=out_specs,
        out_shape=out_shape,
        compiler_params=_cparams(("parallel",)),
        name="inproj_rope" if rope else "inproj_ctx",
    )(*args)


def _diff_attn_kernel(*refs, tq, tk, n_main, lambda_init):
    if n_main:
        (dl_ref, g_ref, q_ref, kl_ref, vl_ref, kc_ref, vc_ref, o_ref,
         m_scr, acc_scr, s_buf0, s_buf1, mc_buf0, mc_buf1) = refs
    else:
        dl_ref, g_ref, q_ref, kc_ref, vc_ref, o_ref, m_scr, acc_scr = refs

    qt = q_ref[...]
    sub = lax.broadcasted_iota(jnp.int32, qt.shape, 0)
    zero = jnp.zeros_like(qt)
    qq = jnp.concatenate([jnp.where(sub < HD, qt, zero), jnp.where(sub < HD, zero, qt)], axis=1)
    m_scr[...] = jnp.full(m_scr.shape, NEG_INF, f32)
    acc_scr[...] = jnp.zeros(acc_scr.shape, f32)

    def scores(k):
        return jnp.dot(k, qq, preferred_element_type=f32)

    def softmax_pv(s, m_chunk, vt):
        m_old = m_scr[...]
        m_new = jnp.maximum(m_old, m_chunk)
        alpha = jnp.exp2(m_old - m_new)
        p = jnp.exp2(s - m_new).astype(bf16)
        acc_scr[...] = alpha * acc_scr[...] + jnp.dot(vt, p, preferred_element_type=f32)
        m_scr[...] = m_new

    s_ctx = scores(kc_ref[...])
    if not n_main:
        softmax_pv(s_ctx, jnp.max(s_ctx, axis=0, keepdims=True), vc_ref[...])
    else:
        def stage_scores(c, s_buf, mc_buf):
            r = pl.multiple_of(c * tk, tk)
            s = scores(kl_ref[pl.ds(r, tk), :])
            s_buf[...] = s
            mc_buf[...] = jnp.max(s, axis=0, keepdims=True)

        def stage_softmax(c, s_buf, mc_buf):
            softmax_pv(s_buf[...], mc_buf[...], vl_ref[c])

        bufs = ((s_buf0, mc_buf0), (s_buf1, mc_buf1))
        unroll = PIPE_UNROLL
        stage_scores(0, *bufs[0])
        softmax_pv(s_ctx, jnp.max(s_ctx, axis=0, keepdims=True), vc_ref[...])

        def body(j, carry):
            c = unroll * j
            for u in range(unroll):
                stage_scores(c + u + 1, *bufs[(u + 1) % 2])
                stage_softmax(c + u, *bufs[u % 2])
            return carry

        lax.fori_loop(0, n_main // unroll - 1, body, 0)
        for c in range(n_main - unroll, n_main):
            if c + 1 < n_main:
                stage_scores(c + 1, *bufs[(c + 1) % 2])
            stage_softmax(c, *bufs[c % 2])

    acc = acc_scr[...]
    o0 = acc[:VD_A, :tq] / acc[VD_A:VD_A + 1, :tq]
    o1 = acc[:VD_A, tq:] / acc[VD_A:VD_A + 1, tq:]
    dl = dl_ref[...]
    lam = (jnp.exp(jnp.sum(dl[0:1] * dl[1:2], axis=1, keepdims=True))
           - jnp.exp(jnp.sum(dl[2:3] * dl[3:4], axis=1, keepdims=True)) + lambda_init)
    o = o0 - lam * o1
    y = o * lax.rsqrt(jnp.mean(o * o, axis=0, keepdims=True) + EPS) * g_ref[...]
    o_ref[...] = (y * (1.0 - lambda_init)).T.astype(o_ref.dtype)


def _diff_attention(dl, g_col, qa_t, ka_lat, va_lat, ka_ctx, va_ctx, *, lambda_init, tq, tk):
    nq = qa_t.shape[2]
    c = ka_ctx.shape[1]
    assert va_ctx.shape[1] == 1
    n_main = 0 if ka_lat is None else ka_lat.shape[1] // tk
    in_specs = [
        pl.BlockSpec((4, HD), lambda h, i: (0, 0)),
        pl.BlockSpec((VD_A, 1), lambda h, i: (0, 0)),
        pl.BlockSpec((None, LANES, tq), lambda h, i: (h, 0, i)),
    ]
    args = [dl, g_col, qa_t]
    if n_main:
        s_keys = ka_lat.shape[1]
        assert va_lat.shape[1:] == (n_main, VT_ROWS, tk)
        in_specs += [pl.BlockSpec((None, s_keys, LANES), lambda h, i: (h, 0, 0)),
                     pl.BlockSpec((None, n_main, VT_ROWS, tk), lambda h, i: (h, 0, 0, 0))]
        args += [ka_lat, va_lat]
    in_specs += [pl.BlockSpec((None, c, LANES), lambda h, i: (h, 0, 0)),
                 pl.BlockSpec((None, None, VT_ROWS, c), lambda h, i: (h, 0, 0, 0))]
    args += [ka_ctx, va_ctx]
    scratch = [pltpu.VMEM((1, 2 * tq), f32), pltpu.VMEM((VT_ROWS, 2 * tq), f32)]
    if n_main:
        assert n_main % PIPE_UNROLL == 0
        scratch += [pltpu.VMEM((tk, 2 * tq), f32), pltpu.VMEM((tk, 2 * tq), f32),
                    pltpu.VMEM((1, 2 * tq), f32), pltpu.VMEM((1, 2 * tq), f32)]
    return pl.pallas_call(
        functools.partial(_diff_attn_kernel, tq=tq, tk=tk, n_main=n_main, lambda_init=lambda_init),
        grid=(HEADS_A, nq // tq),
        in_specs=in_specs,
        out_specs=pl.BlockSpec((tq, VD_A), lambda h, i: (i, h)),
        out_shape=jax.ShapeDtypeStruct((nq, WIDTH_A), bf16),
        scratch_shapes=scratch,
        compiler_params=_cparams(("arbitrary", "arbitrary")),
        name="diff_attn" if n_main else "diff_attn_ctx",
    )(*args)


def _win_attn_kernel(*refs, tq, band, s_len):
    if band:
        (sink_ref, q_ref, kp_ref, ks_ref, kn_ref, kc_ref,
         vp_ref, vs_ref, vn_ref, vc_ref, o_ref) = refs
    else:
        sink_ref, q_ref, kc_ref, vc_ref, o_ref = refs
    i = pl.program_id(0)
    lane = lax.broadcasted_iota(jnp.int32, (tq, LANES), 1)
    if band:
        nloc = tq + 2 * WINDOW
        qpos = i * tq + lax.broadcasted_iota(jnp.int32, (tq, nloc), 0)
        kpos = i * tq - WINDOW + lax.broadcasted_iota(jnp.int32, (tq, nloc), 1)
        valid = (jnp.abs(qpos - kpos) <= WINDOW) & (kpos >= 0) & (kpos < s_len)
    nt = (((1,), (1,)), ((), ()))
    for g in range(KV_HEADS_B):
        q = q_ref[g]
        kc = kc_ref[g]
        if band:
            kloc = jnp.concatenate([kp_ref[g], ks_ref[g], kn_ref[g]], axis=0)
        o = jnp.zeros((tq, LANES), f32)
        for r in range(2):
            sink = sink_ref[2 * g + r] * LOG2_E
            qm = jnp.where((lane < HD) if r == 0 else (lane >= HD), q, jnp.zeros_like(q))
            s_ctx = lax.dot_general(qm, kc, nt, preferred_element_type=f32)
            m = jnp.maximum(jnp.max(s_ctx, axis=1, keepdims=True), sink)
            if band:
                s_loc = lax.dot_general(qm, kloc, nt, preferred_element_type=f32)
                s_loc = jnp.where(valid, s_loc, NEG_INF)
                m = jnp.maximum(m, jnp.max(s_loc, axis=1, keepdims=True))
            e_ctx = jnp.exp2(s_ctx - m)
            l = jnp.sum(e_ctx, axis=1, keepdims=True) + jnp.exp2(sink - m)
            o_r = jnp.dot(e_ctx.astype(bf16), vc_ref[g, r], preferred_element_type=f32)
            if band:
                e_loc = jnp.exp2(s_loc - m)
                l = l + jnp.sum(e_loc, axis=1, keepdims=True)
                vloc = jnp.concatenate([vp_ref[g, r], vs_ref[g, r], vn_ref[g, r]], axis=0)
                o_r = o_r + jnp.dot(e_loc.astype(bf16), vloc, preferred_element_type=f32)
            o = o + o_r / l
        o_ref[:, g * LANES:(g + 1) * LANES] = o.astype(o_ref.dtype)


def _win_attention(sink, qb, kb_lat, vb_lat, kb_ctx, vb_ctx, *, tq):
    nq = qb.shape[1]
    c = kb_ctx.shape[1]
    band = kb_lat is not None
    smem = pl.BlockSpec(memory_space=pltpu.SMEM)
    q_spec = pl.BlockSpec((KV_HEADS_B, tq, LANES), lambda i: (0, i, 0))
    kc_spec = pl.BlockSpec((KV_HEADS_B, c, LANES), lambda i: (0, 0, 0))
    vc_spec = pl.BlockSpec((KV_HEADS_B, 2, c, LANES), lambda i: (0, 0, 0, 0))
    if band:
        per = tq // WINDOW
        last = nq // WINDOW - 1
        prev = lambda i: jnp.maximum(i * per - 1, 0)
        nxt = lambda i: jnp.minimum((i + 1) * per, last)
        in_specs = [
            smem, q_spec,
            pl.BlockSpec((KV_HEADS_B, WINDOW, LANES), lambda i: (0, prev(i), 0)),
            pl.BlockSpec((KV_HEADS_B, tq, LANES), lambda i: (0, i, 0)),
            pl.BlockSpec((KV_HEADS_B, WINDOW, LANES), lambda i: (0, nxt(i), 0)),
            kc_spec,
            pl.BlockSpec((KV_HEADS_B, 2, WINDOW, LANES), lambda i: (0, 0, prev(i), 0)),
            pl.BlockSpec((KV_HEADS_B, 2, tq, LANES), lambda i: (0, 0, i, 0)),
            pl.BlockSpec((KV_HEADS_B, 2, WINDOW, LANES), lambda i: (0, 0, nxt(i), 0)),
            vc_spec,
        ]
        args = [sink, qb, kb_lat, kb_lat, kb_lat, kb_ctx, vb_lat, vb_lat, vb_lat, vb_ctx]
    else:
        in_specs = [smem, q_spec, kc_spec, vc_spec]
        args = [sink, qb, kb_ctx, vb_ctx]
    return pl.pallas_call(
        functools.partial(_win_attn_kernel, tq=tq, band=band, s_len=nq),
        grid=(nq // tq,),
        in_specs=in_specs,
        out_specs=pl.BlockSpec((tq, WIDTH_B), lambda i: (i, 0)),
        out_shape=jax.ShapeDtypeStruct((nq, WIDTH_B), bf16),
        compiler_params=_cparams(("parallel",)),
        name="win_attn" if band else "win_attn_ctx",
    )(*args)


def _conv_kernel(zp_ref, zs_ref, zn_ref, w_ref, b_ref, lg_ref, lb_ref, o_ref, zbuf, zsh, *, tm, rc):
    i = pl.program_id(0)
    n = pl.num_programs(0)
    halo = CONV_HALO
    zero = jnp.zeros((halo, CONV_CH), f32)
    zbuf[0:halo, :] = jnp.where(i > 0, zp_ref[...], zero)
    zbuf[halo:halo + tm, :] = zs_ref[...]
    zbuf[halo + tm:2 * halo + tm, :] = jnp.where(i < n - 1, zn_ref[...], zero)
    w = w_ref[...]
    base = halo - CONV_K // 2
    span = zsh.shape[1]
    for b in range(SUBLANES):
        zsh[b] = zbuf[b:b + span, :]

    for r0 in range(0, tm, rc):
        acc = jnp.zeros((rc, CONV_CH), f32)
        for k in range(CONV_K):
            off = base + k
            start = r0 + SUBLANES * (off // SUBLANES)
            acc = acc + zsh[off % SUBLANES, start:start + rc, :] * w[k:k + 1, :]
        zc = acc + b_ref[...]
        mu = jnp.mean(zc, axis=-1, keepdims=True)
        dz = zc - mu
        var = jnp.mean(dz * dz, axis=-1, keepdims=True)
        zn = dz * lax.rsqrt(var + EPS) * lg_ref[...] + lb_ref[...]
        o_ref[r0:r0 + rc, :] = (zn * jax.nn.sigmoid(zn)).astype(o_ref.dtype)


def _conv_module(z, w, b, ln_g, ln_b, *, tm):
    n = z.shape[0]
    per = tm // CONV_HALO
    last = n // CONV_HALO - 1
    vec = lambda i: (0, 0)
    return pl.pallas_call(
        functools.partial(_conv_kernel, tm=tm, rc=32),
        grid=(n // tm,),
        in_specs=[
            pl.BlockSpec((CONV_HALO, CONV_CH), lambda i: (jnp.maximum(i * per - 1, 0), 0)),
            pl.BlockSpec((tm, CONV_CH), lambda i: (i, 0)),
            pl.BlockSpec((CONV_HALO, CONV_CH), lambda i: (jnp.minimum((i + 1) * per, last), 0)),
            pl.BlockSpec((CONV_K, CONV_CH), vec),
            pl.BlockSpec((1, CONV_CH), vec),
            pl.BlockSpec((1, CONV_CH), vec),
            pl.BlockSpec((1, CONV_CH), vec),
        ],
        out_specs=pl.BlockSpec((tm, CONV_CH), lambda i: (i, 0)),
        out_shape=jax.ShapeDtypeStruct((n, CONV_CH), bf16),
        scratch_shapes=[pltpu.VMEM((tm + 2 * CONV_HALO, CONV_CH), f32),
                        pltpu.VMEM((SUBLANES, tm + 2 * CONV_HALO - SUBLANES, CONV_CH), f32)],
        compiler_params=_cparams(("parallel",)),
        name="conv_module",
    )(z, z, z, w, b, ln_g, ln_b)


def _split_bf16(a):
    hi = a.astype(bf16)
    lo = (a - hi.astype(f32)).astype(bf16)
    return hi, lo


def _route(lt, bias_ref):
    s = [jax.nn.sigmoid(lt[e:e + 1, :]) for e in range(N_EXPERTS)]
    b = [s[e] + bias_ref[e] for e in range(N_EXPERTS)]
    grp = []
    for g in range(N_GROUPS):
        v = b[4 * g:4 * g + 4]
        best = None
        for a_i in range(4):
            for b_i in range(a_i + 1, 4):
                t = v[a_i] + v[b_i]
                best = t if best is None else jnp.maximum(best, t)
        grp.append(best)
    sel = jnp.zeros_like(lt[0:1, :], dtype=jnp.int32)
    gbest = grp[0]
    for g in range(1, N_GROUPS):
        better = grp[g] > gbest
        sel = jnp.where(better, g, sel)
        gbest = jnp.where(better, grp[g], gbest)

    def pick(rows, j):
        out = rows[j]
        for g in range(1, N_GROUPS):
            out = jnp.where(sel == g, rows[4 * g + j], out)
        return out

    vb = [pick(b, j) for j in range(4)]
    vs = [pick(s, j) for j in range(4)]
    i1 = jnp.zeros_like(sel)
    b1 = vb[0]
    for j in range(1, 4):
        better = vb[j] > b1
        i1 = jnp.where(better, j, i1)
        b1 = jnp.where(better, vb[j], b1)
    i2 = jnp.full_like(sel, -1)
    b2 = jnp.full_like(b1, -jnp.inf)
    for j in range(4):
        better = (i1 != j) & ((i2 < 0) | (vb[j] > b2))
        i2 = jnp.where(better, j, i2)
        b2 = jnp.where(better, vb[j], b2)
    s1 = vs[0]
    s2 = vs[0]
    for j in range(1, 4):
        s1 = jnp.where(i1 == j, vs[j], s1)
        s2 = jnp.where(i2 == j, vs[j], s2)
    tot = s1 + s2
    w1 = s1 / tot
    w2 = s2 / tot
    e1 = sel * EXPERTS_PER_GROUP + i1
    e2 = sel * EXPERTS_PER_GROUP + i2
    rows, asg = [], []
    zero = jnp.zeros_like(w1)
    one = jnp.ones_like(w1)
    for e in range(N_EXPERTS):
        rows.append(jnp.where(e1 == e, w1, zero) + jnp.where(e2 == e, w2, zero))
        asg.append(jnp.where((e1 == e) | (e2 == e), one, zero))
    return jnp.concatenate(rows, axis=0), jnp.concatenate(asg, axis=0)


def _outproj_kernel(bias_ref, x_ref, oa_ref, ob_ref, oc_ref, w_ref, gate_ref, sh_ref, sc_ref, g_ref,
                    rwt_ref, xo_ref, h2_ref, comb_ref, rank_ref, cnt_ref):
    y = jnp.dot(oa_ref[...], w_ref[0:WIDTH_A, :], preferred_element_type=f32)
    y = y + jnp.dot(ob_ref[...], w_ref[WIDTH_A:WIDTH_A + WIDTH_B, :], preferred_element_type=f32)
    y = y + jnp.dot(oc_ref[...], w_ref[WIDTH_A + WIDTH_B:, :], preferred_element_type=f32)
    xn = x_ref[...] + gate_ref[...] * y
    xo_ref[...] = xn
    h2 = _rmsnorm_mod(xn, g_ref[...], sh_ref[...], sc_ref[...])
    h2_ref[...] = h2.astype(bf16)
    nt = (((1,), (1,)), ((), ()))
    h_hi, h_lo = _split_bf16(h2)
    r_hi, r_lo = _split_bf16(rwt_ref[...])
    lt = (lax.dot_general(r_hi, h_hi, nt, preferred_element_type=f32)
          + lax.dot_general(r_hi, h_lo, nt, preferred_element_type=f32)
          + lax.dot_general(r_lo, h_hi, nt, preferred_element_type=f32))
    comb, asg = _route(lt, bias_ref)
    tm = lt.shape[1]
    upper = (lax.broadcasted_iota(jnp.int32, (tm, tm), 0) < lax.broadcasted_iota(jnp.int32, (tm, tm), 1))
    rank = jnp.dot(asg.astype(bf16), upper.astype(bf16), preferred_element_type=f32)
    rank = jnp.where(asg > 0.0, rank, -1.0)
    for e in range(N_EXPERTS):
        comb_ref[e] = comb[e:e + 1, :]
        rank_ref[e] = rank[e:e + 1, :]
    cnt = jnp.sum(asg, axis=1, keepdims=True).astype(jnp.int32)
    cnt_ref[...] = jnp.broadcast_to(cnt, cnt_ref.shape)


def _outproj(router_bias, x2, oa, ob, oc, w_bf, layer, gate, shift, scale, g, rwt, *, tm):
    n, d = x2.shape
    row = lambda i: (i, 0)
    const = lambda i: (0, 0)
    return pl.pallas_call(
        _outproj_kernel,
        grid=(n // tm,),
        in_specs=[
            pl.BlockSpec(memory_space=pltpu.SMEM),
            pl.BlockSpec((tm, d), row),
            pl.BlockSpec((tm, WIDTH_A), row),
            pl.BlockSpec((tm, WIDTH_B), row),
            pl.BlockSpec((tm, CONV_CH), row),
            pl.BlockSpec((None, d, d), lambda i: (layer, 0, 0)),
            pl.BlockSpec((1, d), const),
            pl.BlockSpec((1, d), const),
            pl.BlockSpec((1, d), const),
            pl.BlockSpec((1, d), const),
            pl.BlockSpec((N_EXPERTS, d), const),
        ],
        out_specs=(
            pl.BlockSpec((tm, d), row),
            pl.BlockSpec((tm, d), row),
            pl.BlockSpec((N_EXPERTS, 1, tm), lambda i: (0, 0, i)),
            pl.BlockSpec((N_EXPERTS, 1, tm), lambda i: (0, 0, i)),
            pl.BlockSpec((None, N_EXPERTS, LANES), lambda i: (i, 0, 0)),
        ),
        out_shape=(
            jax.ShapeDtypeStruct((n, d), f32),
            jax.ShapeDtypeStruct((n, d), bf16),
            jax.ShapeDtypeStruct((N_EXPERTS, 1, n), f32),
            jax.ShapeDtypeStruct((N_EXPERTS, 1, n), f32),
            jax.ShapeDtypeStruct((n // tm, N_EXPERTS, LANES), jnp.int32),
        ),
        compiler_params=_cparams(("parallel",)),
        name="outproj_router",
    )(router_bias, x2, oa, ob, oc, w_bf, gate, shift, scale, g, rwt)


def _moe_kernel(*refs, final, rows):
    if final:
        nblk_ref, x_ref, h_ref, comb_ref, rank_ref, w1_ref, w3_ref, w2_ref, gate_ref, gf_ref, o_ref, acc = refs
    else:
        nblk_ref, x_ref, h_ref, comb_ref, rank_ref, w1_ref, w3_ref, w2_ref, gate_ref, o_ref, acc = refs
    w = pl.program_id(0)
    e = pl.program_id(1)

    @pl.when(e == 0)
    def _():
        acc[...] = jnp.zeros(acc.shape, f32)

    comb = comb_ref[...]
    rank = rank_ref[...]
    win = comb.shape[1]
    slot0 = lax.broadcasted_iota(jnp.int32, (rows, win), 0).astype(f32)

    def block(b, carry):
        match = rank == slot0 + (b * rows).astype(f32)
        onehot = jnp.where(match, 1.0, 0.0).astype(bf16)
        cw = jnp.sum(jnp.where(match, comb, 0.0), axis=1, keepdims=True)
        xe = jnp.dot(onehot, h_ref[...], preferred_element_type=f32).astype(bf16)
        a = jnp.dot(xe, w1_ref[...], preferred_element_type=f32)
        g = jnp.dot(xe, w3_ref[...], preferred_element_type=f32)
        u = (a * jax.nn.sigmoid(a) * g).astype(bf16)
        y = jnp.dot(u, w2_ref[...], preferred_element_type=f32)
        yw = (cw * y).astype(bf16)
        acc[...] += lax.dot_general(onehot, yw, (((0,), (0,)), ((), ())), preferred_element_type=f32)
        return carry

    lax.fori_loop(0, nblk_ref[w * N_EXPERTS + e], block, 0)

    @pl.when(e == pl.num_programs(1) - 1)
    def _():
        xn = x_ref[...] + gate_ref[...] * acc[...]
        if final:
            xn = xn * lax.rsqrt(jnp.mean(xn * xn, axis=-1, keepdims=True) + EPS) * gf_ref[...]
        o_ref[...] = xn


def _moe(x2, h2, comb, rank, counts, w1, w3, w2, layer, gate, g_final, *, win):
    n, d = x2.shape
    final = g_final is not None
    rows = MOE_BLOCK_ROWS
    nblk = ((counts + (rows - 1)) // rows).reshape(-1).astype(jnp.int32)
    row = lambda i, e, nb: (i, 0)
    const = lambda i, e, nb: (0, 0)
    per_expert = lambda i, e, nb: (e, 0, i)
    weight = lambda i, e, nb: (layer, e, 0, 0)
    in_specs = [
        pl.BlockSpec((win, d), row),
        pl.BlockSpec((win, d), row),
        pl.BlockSpec((None, 1, win), per_expert),
        pl.BlockSpec((None, 1, win), per_expert),
        pl.BlockSpec((None, None, d, EXPERT_FF), weight),
        pl.BlockSpec((None, None, d, EXPERT_FF), weight),
        pl.BlockSpec((None, None, EXPERT_FF, d), weight),
        pl.BlockSpec((1, d), const),
    ]
    args = [x2, h2, comb, rank, w1, w3, w2, gate]
    if final:
        in_specs.append(pl.BlockSpec((1, d), const))
        args.append(g_final)
    return pl.pallas_call(
        functools.partial(_moe_kernel, final=final, rows=rows),
        grid_spec=pltpu.PrefetchScalarGridSpec(
            num_scalar_prefetch=1,
            grid=(n // win, N_EXPERTS),
            in_specs=in_specs,
            out_specs=pl.BlockSpec((win, d), row),
            scratch_shapes=[pltpu.VMEM((win, d), f32)],
        ),
        out_shape=jax.ShapeDtypeStruct((n, d), f32),
        compiler_params=_cparams(("parallel", "arbitrary")),
        name="moe_final" if final else "moe",
    )(nblk, *args)


def _rope_tables(n_tok):
    rows = n_tok // GRID_W
    row = np.repeat(np.arange(rows, dtype=np.float64), GRID_W)
    col = np.tile(np.arange(GRID_W, dtype=np.float64), rows)
    n_freq = HD // 4
    inv = ROPE_BASE ** (-np.arange(n_freq, dtype=np.float64) / n_freq)
    lane = np.arange(LANES)
    j = lane % HD
    use_col = (j // 32) == 1
    freq = inv[j % n_freq]
    ang = np.where(use_col[None, :], col[:, None], row[:, None]) * freq[None, :]
    sign = np.where((lane % 32) < 16, -1.0, 1.0)
    return (jnp.asarray(np.cos(ang).astype(np.float32)),
            jnp.asarray((np.sin(ang) * sign[None, :]).astype(np.float32)))


def kernel(x, c, ctx, c_ctx, w_ada, b_ada, g_mix, w_in, diff_lambda, diff_norm_g, attn_sink, conv_w, conv_b,
           conv_ln_g, conv_ln_b, w_out, g_ffn, router_w, router_bias, w1, w3, w2, g_final):
    bsz, s_len, d = x.shape
    c_len = ctx.shape[1]
    depth = w_ada.shape[0]
    assert bsz == 1 and d == D_MODEL and s_len % 1024 == 0 and c_len % 256 == 0
    tm_lat = 256
    tm_ctx = min(c_len, 256)
    win_lat = MOE_WINDOW
    tq_a, tk_a = 256, 1024

    xl = x.reshape(s_len, d)
    xc = ctx.reshape(c_len, d)
    ct = jnp.zeros((d, LANES), f32).at[:, 0].set(c[0]).at[:, 1].set(c_ctx)
    mod_all = _modulation(ct, w_ada, b_ada)
    cos_t, sin_t = _rope_tables(s_len)
    rwt = router_w.T
    w_in_bf = w_in.astype(bf16)
    w_out_bf = w_out.astype(bf16)
    w1_bf, w3_bf, w2_bf = w1.astype(bf16), w3.astype(bf16), w2.astype(bf16)
    vec = lambda a: a.reshape(1, -1)

    for l in range(depth):
        last = l == depth - 1
        lambda_init = 0.8 - 0.6 * math.exp(-0.3 * l)
        ml = [mod_all[l, 0:1, k * d:(k + 1) * d] for k in range(6)]
        mc = [mod_all[l, 1:2, k * d:(k + 1) * d] for k in range(6)]
        g_mix_l, g_ffn_l = vec(g_mix[l]), vec(g_ffn[l])
        dl, dng = diff_lambda[l], diff_norm_g[l].reshape(-1, 1)

        qa, ka, va, qb, kb, vb, z = _inproj(xl, ml[0], ml[1], g_mix_l, w_in_bf, l, cos_t, sin_t, tm=tm_lat,
                                            tk=tk_a)
        qa_c, ka_c, va_c, qb_c, kb_c, vb_c, z_c = _inproj(xc, mc[0], mc[1], g_mix_l, w_in_bf, l, None, None,
                                                          tm=tm_ctx, tk=c_len)

        o_a = _diff_attention(dl, dng, qa, ka, va, ka_c, va_c, lambda_init=lambda_init, tq=tq_a, tk=tk_a)
        o_b = _win_attention(attn_sink[l], qb, kb, vb, kb_c, vb_c, tq=256)
        o_c = _conv_module(z, conv_w[l], vec(conv_b[l]), vec(conv_ln_g[l]), vec(conv_ln_b[l]), tm=tm_lat)
        xl, h2, comb, rank, cnt = _outproj(router_bias, xl, o_a, o_b, o_c, w_out_bf, l, ml[2], ml[3], ml[4],
                                           g_ffn_l, rwt, tm=win_lat)
        xl = _moe(xl, h2, comb, rank, cnt[:, :, 0], w1_bf, w3_bf, w2_bf, l, ml[5],
                  vec(g_final) if last else None, win=win_lat)

        if not last:
            o_ac = _diff_attention(dl, dng, qa_c, None, None, ka_c, va_c, lambda_init=lambda_init,
                                   tq=tm_ctx, tk=tk_a)
            o_bc = _win_attention(attn_sink[l], qb_c, None, None, kb_c, vb_c, tq=tm_ctx)
            o_cc = _conv_module(z_c, conv_w[l], vec(conv_b[l]), vec(conv_ln_g[l]), vec(conv_ln_b[l]), tm=tm_ctx)
            xc, h2c, comb_c, rank_c, cnt_c = _outproj(router_bias, xc, o_ac, o_bc, o_cc, w_out_bf, l, mc[2], mc[3],
                                                      mc[4], g_ffn_l, rwt, tm=tm_ctx)
            xc = _moe(xc, h2c, comb_c, rank_c, cnt_c[:, :, 0], w1_bf, w3_bf, w2_bf, l, mc[5], None,
                      win=tm_ctx)

    return xl.reshape(bsz, s_len, d)
```

```python
import functools
import math

import numpy as np
import jax
import jax.numpy as jnp
from jax import lax
from jax.experimental import pallas as pl
from jax.experimental.pallas import tpu as pltpu

f32 = jnp.float32
bf16 = jnp.bfloat16

D_MODEL = 1024
GRID_W = 64
HEADS_A = 4
HD = 64
VD_A = 2 * HD
VT_ROWS = VD_A + 16
WIDTH_A = HEADS_A * VD_A
HEADS_B = 4
KV_HEADS_B = 2
WIDTH_B = HEADS_B * HD
WINDOW = 128
CONV_CH = 256
CONV_K = 31
CONV_HALO = 16
IN_WIDTH = 2560
OFF_QA, OFF_KA, OFF_VA, OFF_QB, OFF_KB, OFF_VB, OFF_UC = 0, 512, 1024, 1536, 1792, 1920, 2048
N_EXPERTS = 16
N_GROUPS = 4
EXPERTS_PER_GROUP = 4
EXPERT_FF = 512
ROPE_BASE = 10000.0
EPS = 1e-6
NEG_INF = -1e30
LANES = 128
SUBLANES = 8
QK_SCALE = HD ** -0.5
LOG2_E = math.log2(math.e)
QK_SCALE_LOG2 = QK_SCALE * LOG2_E

MOE_WINDOW = 1024
MOE_EXPERTS_PER_STEP = 4
MOE_BLOCK_ROWS = 160
PIPE_UNROLL = 4
VMEM_LIMIT =56 * 1024 * 1024


def _cparams(sem):
    return pltpu.CompilerParams(dimension_semantics=sem, vmem_limit_bytes=VMEM_LIMIT)


def _mod_kernel(ct_ref, w_ref, b_ref, o_ref):
    tn = w_ref.shape[1]

    def body(i, carry):
        a0, a1 = carry
        r = pl.multiple_of(i * 8, 8)
        cv = ct_ref[pl.ds(r, 8), :]
        sv = cv * jax.nn.sigmoid(cv)
        w8 = w_ref[pl.ds(r, 8), :]
        return a0 + w8 * sv[:, 0:1], a1 + w8 * sv[:, 1:2]

    z = jnp.zeros((8, tn), f32)
    a0, a1 = lax.fori_loop(0, w_ref.shape[0] // 8, body, (z, z), unroll=8)
    r0 = jnp.sum(a0, axis=0, keepdims=True) + b_ref[...]
    r1 = jnp.sum(a1, axis=0, keepdims=True) + b_ref[...]
    o_ref[...] = jnp.concatenate([r0, r1, jnp.zeros((6, tn), f32)], axis=0)


def _modulation(ct, w_ada, b_ada):
    depth, d, n = w_ada.shape
    tn = 1536
    return pl.pallas_call(
        _mod_kernel,
        grid=(depth, n // tn),
        in_specs=[
            pl.BlockSpec((d, LANES), lambda l, j: (0, 0)),
            pl.BlockSpec((None, d, tn), lambda l, j: (l, 0, j)),
            pl.BlockSpec((None, 1, tn), lambda l, j: (l, 0, j)),
        ],
        out_specs=pl.BlockSpec((None, 8, tn), lambda l, j: (l, 0, j)),
        out_shape=jax.ShapeDtypeStruct((depth, 8, n), f32),
        compiler_params=_cparams(("arbitrary", "arbitrary")),
        name="modulation",
    )(ct, w_ada, b_ada.reshape(depth, 1, n))


def _rmsnorm_mod(xf, g, shift, scale):
    y = xf * lax.rsqrt(jnp.mean(xf * xf, axis=-1, keepdims=True) + EPS) * g
    return y * (1.0 + scale) + shift


def _inproj_kernel(*refs, rope):
    if rope:
        x_ref, sh_ref, sc_ref, g_ref, w_ref, cos_ref, sin_ref = refs[:7]
        outs = refs[7:]
    else:
        x_ref, sh_ref, sc_ref, g_ref, w_ref = refs[:5]
        outs = refs[5:]
    qa_ref, ka_ref, va_ref, qb_ref, kb_ref, vb_ref, z_ref = outs

    h = _rmsnorm_mod(x_ref[...], g_ref[...], sh_ref[...], sc_ref[...])
    p = jnp.dot(h.astype(bf16), w_ref[...], preferred_element_type=f32)
    tm = p.shape[0]
    lane = lax.broadcasted_iota(jnp.int32, (tm, LANES), 1)
    lo_half = lane < HD

    def rot(xc, scale):
        if rope:
            first = (lane % 32) < 16
            partner = jnp.where(first, pltpu.roll(xc, LANES - 16, 1), pltpu.roll(xc, 16, 1))
            xc = xc * cos_ref[...] + partner * sin_ref[...]
        return xc * scale if scale != 1.0 else xc

    def chunk(off, j):
        return p[:, off + j * LANES: off + (j + 1) * LANES]

    ones = jnp.ones((VT_ROWS - VD_A, tm), f32)
    for hh in range(HEADS_A):
        qa_ref[hh] = rot(chunk(OFF_QA, hh), QK_SCALE_LOG2).T.astype(bf16)
        ka_ref[hh] = rot(chunk(OFF_KA, hh), 1.0).astype(bf16)
        va_ref[hh] = jnp.concatenate([chunk(OFF_VA, hh).T, ones], axis=0).astype(bf16)
    for g in range(KV_HEADS_B):
        qb_ref[g] = rot(chunk(OFF_QB, g), QK_SCALE_LOG2).astype(bf16)
    kb = rot(chunk(OFF_KB, 0), 1.0)
    kb_sw = pltpu.roll(kb, HD, 1)
    kb_ref[0] = jnp.where(lo_half, kb, kb_sw).astype(bf16)
    kb_ref[1] = jnp.where(lo_half, kb_sw, kb).astype(bf16)
    vb = chunk(OFF_VB, 0)
    vb_sw = pltpu.roll(vb, HD, 1)
    zero = jnp.zeros_like(vb)
    vb_ref[0, 0] = jnp.where(lo_half, vb, zero).astype(bf16)
    vb_ref[0, 1] = jnp.where(lo_half, zero, vb_sw).astype(bf16)
    vb_ref[1, 0] = jnp.where(lo_half, vb_sw, zero).astype(bf16)
    vb_ref[1, 1] = jnp.where(lo_half, zero, vb).astype(bf16)
    a = p[:, OFF_UC: OFF_UC + CONV_CH]
    gt = p[:, OFF_UC + CONV_CH: OFF_UC + 2 * CONV_CH]
    z_ref[...] = a * jax.nn.sigmoid(gt)


def _inproj(x2, shift, scale, g, w_bf, layer, cos_t, sin_t, *, tm, tk):
    n, d = x2.shape
    rope = cos_t is not None
    per = tk // tm
    row = lambda i: (i, 0)
    const = lambda i: (0, 0)
    in_specs = [
        pl.BlockSpec((tm, d), row),
        pl.BlockSpec((1, d), const),
        pl.BlockSpec((1, d), const),
        pl.BlockSpec((1, d), const),
        pl.BlockSpec((None, d, IN_WIDTH), lambda i: (layer, 0, 0)),
    ]
    args = [x2, shift, scale, g, w_bf]
    if rope:
        in_specs += [pl.BlockSpec((tm, LANES), row), pl.BlockSpec((tm, LANES), row)]
        args += [cos_t, sin_t]
    out_shape = (
        jax.ShapeDtypeStruct((HEADS_A, LANES, n), bf16),
        jax.ShapeDtypeStruct((HEADS_A, n, LANES), bf16),
        jax.ShapeDtypeStruct((HEADS_A, n // tk, VT_ROWS, tk), bf16),
        jax.ShapeDtypeStruct((KV_HEADS_B, n, LANES), bf16),
        jax.ShapeDtypeStruct((KV_HEADS_B, n, LANES), bf16),
        jax.ShapeDtypeStruct((KV_HEADS_B, 2, n, LANES), bf16),
        jax.ShapeDtypeStruct((n, CONV_CH), f32),
    )
    out_specs = (
        pl.BlockSpec((HEADS_A, LANES, tm), lambda i: (0, 0, i)),
        pl.BlockSpec((HEADS_A, tm, LANES), lambda i: (0, i, 0)),
        pl.BlockSpec((HEADS_A, None, VT_ROWS, tm), lambda i: (0, i // per, 0, i % per)),
        pl.BlockSpec((KV_HEADS_B, tm, LANES), lambda i: (0, i, 0)),
        pl.BlockSpec((KV_HEADS_B, tm, LANES), lambda i: (0, i, 0)),
        pl.BlockSpec((KV_HEADS_B, 2, tm, LANES), lambda i: (0, 0, i, 0)),
        pl.BlockSpec((tm, CONV_CH), row),
    )
    return pl.pallas_call(
        functools.partial(_inproj_kernel, rope=rope),
        grid=(n // tm,),
        in_specs=in_specs,
        out_specs=out_specs,
        out_shape=out_shape,
        compiler_params=_cparams(("parallel",)),
        name="inproj_rope" if rope else "inproj_ctx",
    )(*args)


def _diff_attn_kernel(*refs, tq, tk, n_main, lambda_init):
    if n_main:
        (dl_ref, g_ref, q_ref, qn_ref, kl_ref, vl_ref, kc_ref, vc_ref, o_ref,
         m_scr, acc_scr, s_buf0, s_buf1, s_bufc, mc_buf0, mc_buf1, mc_bufc) = refs
    else:
        dl_ref, g_ref, q_ref, kc_ref, vc_ref, o_ref, m_scr, acc_scr = refs

    def stacked_maps(qt):
        sub = lax.broadcasted_iota(jnp.int32, qt.shape, 0)
        zero = jnp.zeros_like(qt)
        return jnp.concatenate([jnp.where(sub < HD, qt, zero), jnp.where(sub < HD, zero, qt)], axis=1)

    qq = stacked_maps(q_ref[...])
    m_scr[...] = jnp.full(m_scr.shape, NEG_INF, f32)
    acc_scr[...] = jnp.zeros(acc_scr.shape, f32)

    def softmax_pv(s, m_chunk, vt):
        m_old = m_scr[...]
        m_new = jnp.maximum(m_old, m_chunk)
        alpha = jnp.exp2(m_old - m_new)
        p = jnp.exp2(s - m_new).astype(bf16)
        acc_scr[...] = alpha * acc_scr[...] + jnp.dot(vt, p, preferred_element_type=f32)
        m_scr[...] = m_new

    if not n_main:
        s_ctx = jnp.dot(kc_ref[...], qq, preferred_element_type=f32)
        softmax_pv(s_ctx, jnp.max(s_ctx, axis=0, keepdims=True), vc_ref[...])
    else:
        def stage_scores(keys, q_stacked, s_buf, mc_buf):
            s = jnp.dot(keys, q_stacked, preferred_element_type=f32)
            s_buf[...] = s
            mc_buf[...] = jnp.max(s, axis=0, keepdims=True)

        def latent_keys(c):
            return kl_ref[pl.ds(pl.multiple_of(c * tk, tk), tk), :]

        bufs = ((s_buf0, mc_buf0), (s_buf1, mc_buf1))
        unroll = PIPE_UNROLL

        @pl.when(pl.program_id(1) == 0)
        def _():
            stage_scores(latent_keys(0), qq, *bufs[0])

        def body(j, carry):
            c = unroll * j
            for u in range(unroll):
                stage_scores(latent_keys(c + u + 1), qq, *bufs[(u + 1) % 2])
                softmax_pv(bufs[u % 2][0][...], bufs[u % 2][1][...], vl_ref[c + u])
            return carry

        lax.fori_loop(0, n_main // unroll - 1, body, 0)
        for c in range(n_main - unroll, n_main):
            if c + 1 < n_main:
                stage_scores(latent_keys(c + 1), qq, *bufs[(c + 1) % 2])
            else:
                stage_scores(kc_ref[...], qq, s_bufc, mc_bufc)
            softmax_pv(bufs[c % 2][0][...], bufs[c % 2][1][...], vl_ref[c])
        stage_scores(latent_keys(0), stacked_maps(qn_ref[...]), *bufs[0])
        softmax_pv(s_bufc[...], mc_bufc[...], vc_ref[...])

    acc = acc_scr[...]
    o0 = acc[:VD_A, :tq] / acc[VD_A:VD_A + 1, :tq]
    o1 = acc[:VD_A, tq:] / acc[VD_A:VD_A + 1, tq:]
    dl = dl_ref[...]
    lam = (jnp.exp(jnp.sum(dl[0:1] * dl[1:2], axis=1, keepdims=True))
           - jnp.exp(jnp.sum(dl[2:3] * dl[3:4], axis=1, keepdims=True)) + lambda_init)
    o = o0 - lam * o1
    y = o * lax.rsqrt(jnp.mean(o * o, axis=0, keepdims=True) + EPS) * g_ref[...]
    o_ref[...] = (y * (1.0 - lambda_init)).T.astype(o_ref.dtype)


def _diff_attention(dl, g_col, qa_t, ka_lat, va_lat, ka_ctx, va_ctx, *, lambda_init, tq, tk):
    nq = qa_t.shape[2]
    c = ka_ctx.shape[1]
    assert va_ctx.shape[1] == 1
    n_main = 0 if ka_lat is None else ka_lat.shape[1] // tk
    in_specs = [
        pl.BlockSpec((4, HD), lambda h, i: (0, 0)),
        pl.BlockSpec((VD_A, 1), lambda h, i: (0, 0)),
        pl.BlockSpec((None, LANES, tq), lambda h, i: (h, 0, i)),
    ]
    args = [dl, g_col, qa_t]
    if n_main:
        s_keys = ka_lat.shape[1]
        assert va_lat.shape[1:] == (n_main, VT_ROWS, tk)
        last_tile = nq // tq - 1
        in_specs += [pl.BlockSpec((None, LANES, tq), lambda h, i: (h, 0, jnp.minimum(i + 1, last_tile))),
                     pl.BlockSpec((None, s_keys, LANES), lambda h, i: (h, 0, 0)),
                     pl.BlockSpec((None, n_main, VT_ROWS, tk), lambda h, i: (h, 0, 0, 0))]
        args += [qa_t, ka_lat, va_lat]
    in_specs += [pl.BlockSpec((None, c, LANES), lambda h, i: (h, 0, 0)),
                 pl.BlockSpec((None, None, VT_ROWS, c), lambda h, i: (h, 0, 0, 0))]
    args += [ka_ctx, va_ctx]
    scratch = [pltpu.VMEM((1, 2 * tq), f32), pltpu.VMEM((VT_ROWS, 2 * tq), f32)]
    if n_main:
        assert n_main % PIPE_UNROLL == 0
        scratch += [pltpu.VMEM((tk, 2 * tq), f32), pltpu.VMEM((tk, 2 * tq), f32), pltpu.VMEM((c, 2 * tq), f32),
                    pltpu.VMEM((1, 2 * tq), f32), pltpu.VMEM((1, 2 * tq), f32), pltpu.VMEM((1, 2 * tq), f32)]
    return pl.pallas_call(
        functools.partial(_diff_attn_kernel, tq=tq, tk=tk, n_main=n_main, lambda_init=lambda_init),
        grid=(HEADS_A, nq // tq),
        in_specs=in_specs,
        out_specs=pl.BlockSpec((tq, VD_A), lambda h, i: (i, h)),
        out_shape=jax.ShapeDtypeStruct((nq, WIDTH_A), bf16),
        scratch_shapes=scratch,
        compiler_params=_cparams(("arbitrary", "arbitrary")),
        name="diff_attn" if n_main else "diff_attn_ctx",
    )(*args)


def _win_attn_kernel(*refs, tq, band, s_len):
    if band:
        (sink_ref, q_ref, kp_ref, ks_ref, kn_ref, kc_ref,
         vp_ref, vs_ref, vn_ref, vc_ref, o_ref) = refs
    else:
        sink_ref, q_ref, kc_ref, vc_ref, o_ref = refs
    i = pl.program_id(0)
    lane = lax.broadcasted_iota(jnp.int32, (tq, LANES), 1)
    if band:
        nloc = tq + 2 * WINDOW
        qpos = i * tq + lax.broadcasted_iota(jnp.int32, (tq, nloc), 0)
        kpos = i * tq - WINDOW + lax.broadcasted_iota(jnp.int32, (tq, nloc), 1)
        valid = (jnp.abs(qpos - kpos) <= WINDOW) & (kpos >= 0) & (kpos < s_len)
    nt = (((1,), (1,)), ((), ()))
    for g in range(KV_HEADS_B):
        q = q_ref[g]
        kc = kc_ref[g]
        if band:
            kloc = jnp.concatenate([kp_ref[g], ks_ref[g], kn_ref[g]], axis=0)
        o = jnp.zeros((tq, LANES), f32)
        for r in range(2):
            sink = sink_ref[2 * g + r] * LOG2_E
            qm = jnp.where((lane < HD) if r == 0 else (lane >= HD), q, jnp.zeros_like(q))
            s_ctx = lax.dot_general(qm, kc, nt, preferred_element_type=f32)
            m = jnp.maximum(jnp.max(s_ctx, axis=1, keepdims=True), sink)
            if band:
                s_loc = lax.dot_general(qm, kloc, nt, preferred_element_type=f32)
                s_loc = jnp.where(valid, s_loc, NEG_INF)
                m = jnp.maximum(m, jnp.max(s_loc, axis=1, keepdims=True))
            e_ctx = jnp.exp2(s_ctx - m)
            l = jnp.sum(e_ctx, axis=1, keepdims=True) + jnp.exp2(sink - m)
            o_r = jnp.dot(e_ctx.astype(bf16), vc_ref[g, r], preferred_element_type=f32)
            if band:
                e_loc = jnp.exp2(s_loc - m)
                l = l + jnp.sum(e_loc, axis=1, keepdims=True)
                vloc = jnp.concatenate([vp_ref[g, r], vs_ref[g, r], vn_ref[g, r]], axis=0)
                o_r = o_r + jnp.dot(e_loc.astype(bf16), vloc, preferred_element_type=f32)
            o = o + o_r / l
        o_ref[:, g * LANES:(g + 1) * LANES] = o.astype(o_ref.dtype)


def _win_attention(sink, qb, kb_lat, vb_lat, kb_ctx, vb_ctx, *, tq):
    nq = qb.shape[1]
    c = kb_ctx.shape[1]
    band = kb_lat is not None
    smem = pl.BlockSpec(memory_space=pltpu.SMEM)
    q_spec = pl.BlockSpec((KV_HEADS_B, tq, LANES), lambda i: (0, i, 0))
    kc_spec = pl.BlockSpec((KV_HEADS_B, c, LANES), lambda i: (0, 0, 0))
    vc_spec = pl.BlockSpec((KV_HEADS_B, 2, c, LANES), lambda i: (0, 0, 0, 0))
    if band:
        per = tq // WINDOW
        last = nq // WINDOW - 1
        prev = lambda i: jnp.maximum(i * per - 1, 0)
        nxt = lambda i: jnp.minimum((i + 1) * per, last)
        in_specs = [
            smem, q_spec,
            pl.BlockSpec((KV_HEADS_B, WINDOW, LANES), lambda i: (0, prev(i), 0)),
            pl.BlockSpec((KV_HEADS_B, tq, LANES), lambda i: (0, i, 0)),
            pl.BlockSpec((KV_HEADS_B, WINDOW, LANES), lambda i: (0, nxt(i), 0)),
            kc_spec,
            pl.BlockSpec((KV_HEADS_B, 2, WINDOW, LANES), lambda i: (0, 0, prev(i), 0)),
            pl.BlockSpec((KV_HEADS_B, 2, tq, LANES), lambda i: (0, 0, i, 0)),
            pl.BlockSpec((KV_HEADS_B, 2, WINDOW, LANES), lambda i: (0, 0, nxt(i), 0)),
            vc_spec,
        ]
        args = [sink, qb, kb_lat, kb_lat, kb_lat, kb_ctx, vb_lat, vb_lat, vb_lat, vb_ctx]
    else:
        in_specs = [smem, q_spec, kc_spec, vc_spec]
        args = [sink, qb, kb_ctx, vb_ctx]
    return pl.pallas_call(
        functools.partial(_win_attn_kernel, tq=tq, band=band, s_len=nq),
        grid=(nq // tq,),
        in_specs=in_specs,
        out_specs=pl.BlockSpec((tq, WIDTH_B), lambda i: (i, 0)),
        out_shape=jax.ShapeDtypeStruct((nq, WIDTH_B), bf16),
        compiler_params=_cparams(("parallel",)),
        name="win_attn" if band else "win_attn_ctx",
    )(*args)


def _conv_kernel(zp_ref, zs_ref, zn_ref, w_ref, b_ref, lg_ref, lb_ref, o_ref, zbuf, zsh, *, tm, rc):
    i = pl.program_id(0)
    n = pl.num_programs(0)
    halo = CONV_HALO
    zero = jnp.zeros((halo, CONV_CH), f32)
    zbuf[0:halo, :] = jnp.where(i > 0, zp_ref[...], zero)
    zbuf[halo:halo + tm, :] = zs_ref[...]
    zbuf[halo + tm:2 * halo + tm, :] = jnp.where(i < n - 1, zn_ref[...], zero)
    w = w_ref[...]
    base = halo - CONV_K // 2
    span = zsh.shape[1]
    for b in range(SUBLANES):
        zsh[b] = zbuf[b:b + span, :]

    for r0 in range(0, tm, rc):
        acc = jnp.zeros((rc, CONV_CH), f32)
        for k in range(CONV_K):
            off = base + k
            start = r0 + SUBLANES * (off // SUBLANES)
            acc = acc + zsh[off % SUBLANES, start:start + rc, :] * w[k:k + 1, :]
        zc = acc + b_ref[...]
        mu = jnp.mean(zc, axis=-1, keepdims=True)
        dz = zc - mu
        var = jnp.mean(dz * dz, axis=-1, keepdims=True)
        zn = dz * lax.rsqrt(var + EPS) * lg_ref[...] + lb_ref[...]
        o_ref[r0:r0 + rc, :] = (zn * jax.nn.sigmoid(zn)).astype(o_ref.dtype)


def _conv_module(z, w, b, ln_g, ln_b, *, tm):
    n = z.shape[0]
    per = tm // CONV_HALO
    last = n // CONV_HALO - 1
    vec = lambda i: (0, 0)
    return pl.pallas_call(
        functools.partial(_conv_kernel, tm=tm, rc=32),
        grid=(n // tm,),
        in_specs=[
            pl.BlockSpec((CONV_HALO, CONV_CH), lambda i: (jnp.maximum(i * per - 1, 0), 0)),
            pl.BlockSpec((tm, CONV_CH), lambda i: (i, 0)),
            pl.BlockSpec((CONV_HALO, CONV_CH), lambda i: (jnp.minimum((i + 1) * per, last), 0)),
            pl.BlockSpec((CONV_K, CONV_CH), vec),
            pl.BlockSpec((1, CONV_CH), vec),
            pl.BlockSpec((1, CONV_CH), vec),
            pl.BlockSpec((1, CONV_CH), vec),
        ],
        out_specs=pl.BlockSpec((tm, CONV_CH), lambda i: (i, 0)),
        out_shape=jax.ShapeDtypeStruct((n, CONV_CH), bf16),
        scratch_shapes=[pltpu.VMEM((tm + 2 * CONV_HALO, CONV_CH), f32),
                        pltpu.VMEM((SUBLANES, tm + 2 * CONV_HALO - SUBLANES, CONV_CH), f32)],
        compiler_params=_cparams(("parallel",)),
        name="conv_module",
    )(z, z, z, w, b, ln_g, ln_b)


def _split_bf16(a):
    hi = a.astype(bf16)
    lo = (a - hi.astype(f32)).astype(bf16)
    return hi, lo


def _route(lt, bias_ref):
    s = [jax.nn.sigmoid(lt[e:e + 1, :]) for e in range(N_EXPERTS)]
    b = [s[e] + bias_ref[e] for e in range(N_EXPERTS)]
    grp = []
    for g in range(N_GROUPS):
        v = b[4 * g:4 * g + 4]
        best = None
        for a_i in range(4):
            for b_i in range(a_i + 1, 4):
                t = v[a_i] + v[b_i]
                best = t if best is None else jnp.maximum(best, t)
        grp.append(best)
    sel = jnp.zeros_like(lt[0:1, :], dtype=jnp.int32)
    gbest = grp[0]
    for g in range(1, N_GROUPS):
        better = grp[g] > gbest
        sel = jnp.where(better, g, sel)
        gbest = jnp.where(better, grp[g], gbest)

    def pick(rows, j):
        out = rows[j]
        for g in range(1, N_GROUPS):
            out = jnp.where(sel == g, rows[4 * g + j], out)
        return out

    vb = [pick(b, j) for j in range(4)]
    vs = [pick(s, j) for j in range(4)]
    i1 = jnp.zeros_like(sel)
    b1 = vb[0]
    for j in range(1, 4):
        better = vb[j] > b1
        i1 = jnp.where(better, j, i1)
        b1 = jnp.where(better, vb[j], b1)
    i2 = jnp.full_like(sel, -1)
    b2 = jnp.full_like(b1, -jnp.inf)
    for j in range(4):
        better = (i1 != j) & ((i2 < 0) | (vb[j] > b2))
        i2 = jnp.where(better, j, i2)
        b2 = jnp.where(better, vb[j], b2)
    s1 = vs[0]
    s2 = vs[0]
    for j in range(1, 4):
        s1 = jnp.where(i1 == j, vs[j], s1)
        s2 = jnp.where(i2 == j, vs[j], s2)
    tot = s1 + s2
    w1 = s1 / tot
    w2 = s2 / tot
    e1 = sel * EXPERTS_PER_GROUP + i1
    e2 = sel * EXPERTS_PER_GROUP + i2
    rows, asg = [], []
    zero = jnp.zeros_like(w1)
    one = jnp.ones_like(w1)
    for e in range(N_EXPERTS):
        rows.append(jnp.where(e1 == e, w1, zero) + jnp.where(e2 == e, w2, zero))
        asg.append(jnp.where((e1 == e) | (e2 == e), one, zero))
    return jnp.concatenate(rows, axis=0), jnp.concatenate(asg, axis=0)


def _outproj_kernel(bias_ref, x_ref, oa_ref, ob_ref, oc_ref, w_ref, gate_ref, sh_ref, sc_ref, g_ref,
                    rwt_ref, xo_ref, h2_ref, comb_ref, rank_ref, cnt_ref):
    y = jnp.dot(oa_ref[...], w_ref[0:WIDTH_A, :], preferred_element_type=f32)
    y = y + jnp.dot(ob_ref[...], w_ref[WIDTH_A:WIDTH_A + WIDTH_B, :], preferred_element_type=f32)
    y = y + jnp.dot(oc_ref[...], w_ref[WIDTH_A + WIDTH_B:, :], preferred_element_type=f32)
    xn = x_ref[...] + gate_ref[...] * y
    xo_ref[...] = xn
    h2 = _rmsnorm_mod(xn, g_ref[...], sh_ref[...], sc_ref[...])
    h2_ref[...] = h2.astype(bf16)
    nt = (((1,), (1,)), ((), ()))
    h_hi, h_lo = _split_bf16(h2)
    r_hi, r_lo = _split_bf16(rwt_ref[...])
    lt = (lax.dot_general(r_hi, h_hi, nt, preferred_element_type=f32)
          + lax.dot_general(r_hi, h_lo, nt, preferred_element_type=f32)
          + lax.dot_general(r_lo, h_hi, nt, preferred_element_type=f32))
    comb, asg = _route(lt, bias_ref)
    tm = lt.shape[1]
    upper = (lax.broadcasted_iota(jnp.int32, (tm, tm), 0) < lax.broadcasted_iota(jnp.int32, (tm, tm), 1))
    rank = jnp.dot(asg.astype(bf16), upper.astype(bf16), preferred_element_type=f32)
    rank = jnp.where(asg > 0.0, rank, -1.0)
    for e in range(N_EXPERTS):
        comb_ref[e] = comb[e:e + 1, :]
        rank_ref[e] = rank[e:e + 1, :]
    cnt = jnp.sum(asg, axis=1, keepdims=True).astype(jnp.int32)
    cnt_ref[...] = jnp.broadcast_to(cnt, cnt_ref.shape)


def _outproj(router_bias, x2, oa, ob, oc, w_bf, layer, gate, shift, scale, g, rwt, *, tm):
    n, d = x2.shape
    row = lambda i: (i, 0)
    const = lambda i: (0, 0)
    return pl.pallas_call(
        _outproj_kernel,
        grid=(n // tm,),
        in_specs=[
            pl.BlockSpec(memory_space=pltpu.SMEM),
            pl.BlockSpec((tm, d), row),
            pl.BlockSpec((tm, WIDTH_A), row),
            pl.BlockSpec((tm, WIDTH_B), row),
            pl.BlockSpec((tm, CONV_CH), row),
            pl.BlockSpec((None, d, d), lambda i: (layer, 0, 0)),
            pl.BlockSpec((1, d), const),
            pl.BlockSpec((1, d), const),
            pl.BlockSpec((1, d), const),
            pl.BlockSpec((1, d), const),
            pl.BlockSpec((N_EXPERTS, d), const),
        ],
        out_specs=(
            pl.BlockSpec((tm, d), row),
            pl.BlockSpec((tm, d), row),
            pl.BlockSpec((N_EXPERTS, 1, tm), lambda i: (0, 0, i)),
            pl.BlockSpec((N_EXPERTS, 1, tm), lambda i: (0, 0, i)),
            pl.BlockSpec((None, N_EXPERTS, LANES), lambda i: (i, 0, 0)),
        ),
        out_shape=(
            jax.ShapeDtypeStruct((n, d), f32),
            jax.ShapeDtypeStruct((n, d), bf16),
            jax.ShapeDtypeStruct((N_EXPERTS, 1, n), f32),
            jax.ShapeDtypeStruct((N_EXPERTS, 1, n), f32),
            jax.ShapeDtypeStruct((n // tm, N_EXPERTS, LANES), jnp.int32),
        ),
        compiler_params=_cparams(("parallel",)),
        name="outproj_router",
    )(router_bias, x2, oa, ob, oc, w_bf, gate, shift, scale, g, rwt)


def _moe_kernel(*refs, final, rows):
    if final:
        nblk_ref, x_ref, h_ref, comb_ref, rank_ref, w1_ref, w3_ref, w2_ref, gate_ref, gf_ref, o_ref, acc = refs
    else:
        nblk_ref, x_ref, h_ref, comb_ref, rank_ref, w1_ref, w3_ref, w2_ref, gate_ref, o_ref, acc = refs
    w = pl.program_id(0)
    step = pl.program_id(1)
    per_step = w1_ref.shape[0]

    @pl.when(step == 0)
    def _():
        acc[...] = jnp.zeros(acc.shape, f32)

    win = comb_ref.shape[2]
    slot0 = lax.broadcasted_iota(jnp.int32, (rows, win), 0).astype(f32)

    for k in range(per_step):
        comb = comb_ref[k]
        rank = rank_ref[k]

        def block(b, carry, comb=comb, rank=rank, k=k):
            match = rank == slot0 + (b * rows).astype(f32)
            onehot = jnp.where(match, 1.0, 0.0).astype(bf16)
            cw = jnp.sum(jnp.where(match, comb, 0.0), axis=1, keepdims=True)
            xe = jnp.dot(onehot, h_ref[...], preferred_element_type=f32).astype(bf16)
            a = jnp.dot(xe, w1_ref[k], preferred_element_type=f32)
            g = jnp.dot(xe, w3_ref[k], preferred_element_type=f32)
            u = (a * jax.nn.sigmoid(a) * g).astype(bf16)
            y = jnp.dot(u, w2_ref[k], preferred_element_type=f32)
            yw = (cw * y).astype(bf16)
            acc[...] += lax.dot_general(onehot, yw, (((0,), (0,)), ((), ())), preferred_element_type=f32)
            return carry

        lax.fori_loop(0, nblk_ref[w * N_EXPERTS + step * per_step + k], block, 0)

    @pl.when(step == pl.num_programs(1) - 1)
    def _():
        xn = x_ref[...] + gate_ref[...] * acc[...]
        if final:
            xn = xn * lax.rsqrt(jnp.mean(xn * xn, axis=-1, keepdims=True) + EPS) * gf_ref[...]
        o_ref[...] = xn


def _moe(x2, h2, comb, rank, counts, w1, w3, w2, layer, gate, g_final, *, win):
    n, d = x2.shape
    final = g_final is not None
    rows = MOE_BLOCK_ROWS
    nblk = ((counts + (rows - 1)) // rows).reshape(-1).astype(jnp.int32)
    row = lambda i, e, nb: (i, 0)
    const = lambda i, e, nb: (0, 0)
    per_expert = lambda i, e, nb: (e, 0, i)
    weight = lambda i, e, nb: (layer, e, 0, 0)
    eps = MOE_EXPERTS_PER_STEP
    in_specs = [
        pl.BlockSpec((win, d), row),
        pl.BlockSpec((win, d), row),
        pl.BlockSpec((eps, 1, win), per_expert),
        pl.BlockSpec((eps, 1, win), per_expert),
        pl.BlockSpec((None, eps, d, EXPERT_FF), weight),
        pl.BlockSpec((None, eps, d, EXPERT_FF), weight),
        pl.BlockSpec((None, eps, EXPERT_FF, d), weight),
        pl.BlockSpec((1, d), const),
    ]
    args = [x2, h2, comb, rank, w1, w3, w2, gate]
    if final:
        in_specs.append(pl.BlockSpec((1, d), const))
        args.append(g_final)
    return pl.pallas_call(
        functools.partial(_moe_kernel, final=final, rows=rows),
        grid_spec=pltpu.PrefetchScalarGridSpec(
            num_scalar_prefetch=1,
            grid=(n // win, N_EXPERTS // eps),
            in_specs=in_specs,
            out_specs=pl.BlockSpec((win, d), row),
            scratch_shapes=[pltpu.VMEM((win, d), f32)],
        ),
        out_shape=jax.ShapeDtypeStruct((n, d), f32),
        compiler_params=_cparams(("parallel", "arbitrary")),
        name="moe_final" if final else "moe",
    )(nblk, *args)


def _rope_tables(n_tok):
    rows = n_tok // GRID_W
    row = np.repeat(np.arange(rows, dtype=np.float64), GRID_W)
    col = np.tile(np.arange(GRID_W, dtype=np.float64), rows)
    n_freq = HD // 4
    inv = ROPE_BASE ** (-np.arange(n_freq, dtype=np.float64) / n_freq)
    lane = np.arange(LANES)
    j = lane % HD
    use_col = (j // 32) == 1
    freq = inv[j % n_freq]
    ang = np.where(use_col[None, :], col[:, None], row[:, None]) * freq[None, :]
    sign = np.where((lane % 32) < 16, -1.0, 1.0)
    return (jnp.asarray(np.cos(ang).astype(np.float32)),
            jnp.asarray((np.sin(ang) * sign[None, :]).astype(np.float32)))


def kernel(x, c, ctx, c_ctx, w_ada, b_ada, g_mix, w_in, diff_lambda, diff_norm_g, attn_sink, conv_w, conv_b,
           conv_ln_g, conv_ln_b, w_out, g_ffn, router_w, router_bias, w1, w3, w2, g_final):
    bsz, s_len, d = x.shape
    c_len = ctx.shape[1]
    depth = w_ada.shape[0]
    assert bsz == 1 and d == D_MODEL and s_len % 1024 == 0 and c_len % 256 == 0
    tm_lat = 256
    tm_ctx = min(c_len, 256)
    win_lat = MOE_WINDOW
    tq_a, tk_a = 256, 1024

    xl = x.reshape(s_len, d)
    xc = ctx.reshape(c_len, d)
    ct = jnp.zeros((d, LANES), f32).at[:, 0].set(c[0]).at[:, 1].set(c_ctx)
    mod_all = _modulation(ct, w_ada, b_ada)
    cos_t, sin_t = _rope_tables(s_len)
    rwt = router_w.T
    w_in_bf = w_in.astype(bf16)
    w_out_bf = w_out.astype(bf16)
    w1_bf, w3_bf, w2_bf = w1.astype(bf16), w3.astype(bf16), w2.astype(bf16)
    vec = lambda a: a.reshape(1, -1)

    for l in range(depth):
        last = l == depth - 1
        lambda_init = 0.8 - 0.6 * math.exp(-0.3 * l)
        ml = [mod_all[l, 0:1, k * d:(k + 1) * d] for k in range(6)]
        mc = [mod_all[l, 1:2, k * d:(k + 1) * d] for k in range(6)]
        g_mix_l, g_ffn_l = vec(g_mix[l]), vec(g_ffn[l])
        dl, dng = diff_lambda[l], diff_norm_g[l].reshape(-1, 1)

        qa, ka, va, qb, kb, vb, z = _inproj(xl, ml[0], ml[1], g_mix_l, w_in_bf, l, cos_t, sin_t, tm=tm_lat,
                                            tk=tk_a)
        qa_c, ka_c, va_c, qb_c, kb_c, vb_c, z_c = _inproj(xc, mc[0], mc[1], g_mix_l, w_in_bf, l, None, None,
                                                          tm=tm_ctx, tk=c_len)

        o_a = _diff_attention(dl, dng, qa, ka, va, ka_c, va_c, lambda_init=lambda_init, tq=tq_a, tk=tk_a)
        o_b = _win_attention(attn_sink[l], qb, kb, vb, kb_c, vb_c, tq=256)
        o_c = _conv_module(z, conv_w[l], vec(conv_b[l]), vec(conv_ln_g[l]), vec(conv_ln_b[l]), tm=tm_lat)
        xl, h2, comb, rank, cnt = _outproj(router_bias, xl, o_a, o_b, o_c, w_out_bf, l, ml[2], ml[3], ml[4],
                                           g_ffn_l, rwt, tm=win_lat)
        xl = _moe(xl, h2, comb, rank, cnt[:, :, 0], w1_bf, w3_bf, w2_bf, l, ml[5],
                  vec(g_final) if last else None, win=win_lat)

        if not last:
            o_ac = _diff_attention(dl, dng, qa_c, None, None, ka_c, va_c, lambda_init=lambda_init,
                                   tq=tm_ctx, tk=tk_a)
            o_bc = _win_attention(attn_sink[l], qb_c, None, None, kb_c, vb_c, tq=tm_ctx)
            o_cc = _conv_module(z_c, conv_w[l], vec(conv_b[l]), vec(conv_ln_g[l]), vec(conv_ln_b[l]), tm=tm_ctx)
            xc, h2c, comb_c, rank_c, cnt_c = _outproj(router_bias, xc, o_ac, o_bc, o_cc, w_out_bf, l, mc[2], mc[3],
                                                      mc[4], g_ffn_l, rwt, tm=tm_ctx)
            xc = _moe(xc, h2c, comb_c, rank_c, cnt_c[:, :, 0], w1_bf, w3_bf, w2_bf, l, mc[5], None,
                      win=tm_ctx)

    return xl.reshape(bsz, s_len, d)
```

```python
import functools
import math

import numpy as np
import jax
import jax.numpy as jnp
from jax import lax
from jax.experimental import pallas as pl
from jax.experimental.pallas import tpu as pltpu

f32 = jnp.float32
bf16 = jnp.bfloat16

D_MODEL = 1024
GRID_W = 64
HEADS_A = 4
HD = 64
VD_A = 2 * HD
VT_ROWS = VD_A + 16
WIDTH_A = HEADS_A * VD_A
HEADS_B = 4
KV_HEADS_B = 2
WIDTH_B = HEADS_B * HD
WINDOW = 128
CONV_CH = 256
CONV_K = 31
CONV_HALO = 16
IN_WIDTH = 2560
OFF_QA, OFF_KA, OFF_VA, OFF_QB, OFF_KB, OFF_VB, OFF_UC = 0, 512, 1024, 1536, 1792, 1920, 2048
N_EXPERTS = 16
N_GROUPS = 4
EXPERTS_PER_GROUP = 4
EXPERT_FF = 512
ROPE_BASE = 10000.0
EPS = 1e-6
NEG_INF = -1e30
LANES = 128
SUBLANES = 8
QK_SCALE = HD ** -0.5
LOG2_E = math.log2(math.e)
QK_SCALE_LOG2 = QK_SCALE * LOG2_E

MOE_WINDOW = 1024
MOE_EXPERTS_PER_STEP = 4
MOE_BLOCK_ROWS = 160
PIPE_UNROLL = 4
VMEM_LIMIT =56 * 1024 * 1024


def _cparams(sem):
    return pltpu.CompilerParams(dimension_semantics=sem, vmem_limit_bytes=VMEM_LIMIT)


def _mod_kernel(ct_ref, w_ref, b_ref, o_ref):
    tn = w_ref.shape[1]

    def body(i, carry):
        a0, a1 = carry
        r = pl.multiple_of(i * 8, 8)
        cv = ct_ref[pl.ds(r, 8), :]
        sv = cv * jax.nn.sigmoid(cv)
        w8 = w_ref[pl.ds(r, 8), :]
        return a0 + w8 * sv[:, 0:1], a1 + w8 * sv[:, 1:2]

    z = jnp.zeros((8, tn), f32)
    a0, a1 = lax.fori_loop(0, w_ref.shape[0] // 8, body, (z, z), unroll=8)
    r0 = jnp.sum(a0, axis=0, keepdims=True) + b_ref[...]
    r1 = jnp.sum(a1, axis=0, keepdims=True) + b_ref[...]
    o_ref[...] = jnp.concatenate([r0, r1, jnp.zeros((6, tn), f32)], axis=0)


def _modulation(ct, w_ada, b_ada):
    depth, d, n = w_ada.shape
    tn = 1536
    return pl.pallas_call(
        _mod_kernel,
        grid=(depth, n // tn),
        in_specs=[
            pl.BlockSpec((d, LANES), lambda l, j: (0, 0)),
            pl.BlockSpec((None, d, tn), lambda l, j: (l, 0, j)),
            pl.BlockSpec((None, 1, tn), lambda l, j: (l, 0, j)),
        ],
        out_specs=pl.BlockSpec((None, 8, tn), lambda l, j: (l, 0, j)),
        out_shape=jax.ShapeDtypeStruct((depth, 8, n), f32),
        compiler_params=_cparams(("arbitrary", "arbitrary")),
        name="modulation",
    )(ct, w_ada, b_ada.reshape(depth, 1, n))


def _rmsnorm_mod(xf, g, shift, scale):
    y = xf * lax.rsqrt(jnp.mean(xf * xf, axis=-1, keepdims=True) + EPS) * g
    return y * (1.0 + scale) + shift


def _inproj_kernel(*refs, rope):
    if rope:
        x_ref, sh_ref, sc_ref, g_ref, w_ref, cos_ref, sin_ref = refs[:7]
        outs = refs[7:]
    else:
        x_ref, sh_ref, sc_ref, g_ref, w_ref = refs[:5]
        outs = refs[5:]
    qa_ref, ka_ref, va_ref, qb_ref, kb_ref, vb_ref, z_ref = outs

    h = _rmsnorm_mod(x_ref[...], g_ref[...], sh_ref[...], sc_ref[...])
    p = jnp.dot(h.astype(bf16), w_ref[...], preferred_element_type=f32)
    tm = p.shape[0]
    lane = lax.broadcasted_iota(jnp.int32, (tm, LANES), 1)
    lo_half = lane < HD

    def rot(xc, scale):
        if rope:
            first = (lane % 32) < 16
            partner = jnp.where(first, pltpu.roll(xc, LANES - 16, 1), pltpu.roll(xc, 16, 1))
            xc = xc * cos_ref[...] + partner * sin_ref[...]
        return xc * scale if scale != 1.0 else xc

    def chunk(off, j):
        return p[:, off + j * LANES: off + (j + 1) * LANES]

    ones = jnp.ones((VT_ROWS - VD_A, tm), f32)
    for hh in range(HEADS_A):
        qa_ref[hh] = rot(chunk(OFF_QA, hh), QK_SCALE_LOG2).T.astype(bf16)
        ka_ref[hh] = rot(chunk(OFF_KA, hh), 1.0).astype(bf16)
        va_ref[hh] = jnp.concatenate([chunk(OFF_VA, hh).T, ones], axis=0).astype(bf16)
    for g in range(KV_HEADS_B):
        qb_ref[g] = rot(chunk(OFF_QB, g), QK_SCALE_LOG2).astype(bf16)
    kb = rot(chunk(OFF_KB, 0), 1.0)
    kb_sw = pltpu.roll(kb, HD, 1)
    kb_ref[0] = jnp.where(lo_half, kb, kb_sw).astype(bf16)
    kb_ref[1] = jnp.where(lo_half, kb_sw, kb).astype(bf16)
    vb = chunk(OFF_VB, 0)
    vb_sw = pltpu.roll(vb, HD, 1)
    zero = jnp.zeros_like(vb)
    vb_ref[0, 0] = jnp.where(lo_half, vb, zero).astype(bf16)
    vb_ref[0, 1] = jnp.where(lo_half, zero, vb_sw).astype(bf16)
    vb_ref[1, 0] = jnp.where(lo_half, vb_sw, zero).astype(bf16)
    vb_ref[1, 1] = jnp.where(lo_half, zero, vb).astype(bf16)
    a = p[:, OFF_UC: OFF_UC + CONV_CH]
    gt = p[:, OFF_UC + CONV_CH: OFF_UC + 2 * CONV_CH]
    z_ref[...] = a * jax.nn.sigmoid(gt)


def _inproj(x2, shift, scale, g, w_bf, layer, cos_t, sin_t, *, tm, tk):
    n, d = x2.shape
    rope = cos_t is not None
    per = tk // tm
    row = lambda i: (i, 0)
    const = lambda i: (0, 0)
    in_specs = [
        pl.BlockSpec((tm, d), row),
        pl.BlockSpec((1, d), const),
        pl.BlockSpec((1, d), const),
        pl.BlockSpec((1, d), const),
        pl.BlockSpec((None, d, IN_WIDTH), lambda i: (layer, 0, 0)),
    ]
    args = [x2, shift, scale, g, w_bf]
    if rope:
        in_specs += [pl.BlockSpec((tm, LANES), row), pl.BlockSpec((tm, LANES), row)]
        args += [cos_t, sin_t]
    out_shape = (
        jax.ShapeDtypeStruct((HEADS_A, LANES, n), bf16),
        jax.ShapeDtypeStruct((HEADS_A, n, LANES), bf16),
        jax.ShapeDtypeStruct((HEADS_A, n // tk, VT_ROWS, tk), bf16),
        jax.ShapeDtypeStruct((KV_HEADS_B, n, LANES), bf16),
        jax.ShapeDtypeStruct((KV_HEADS_B, n, LANES), bf16),
        jax.ShapeDtypeStruct((KV_HEADS_B, 2, n, LANES), bf16),
        jax.ShapeDtypeStruct((n, CONV_CH), f32),
    )
    out_specs = (
        pl.BlockSpec((HEADS_A, LANES, tm), lambda i: (0, 0, i)),
        pl.BlockSpec((HEADS_A, tm, LANES), lambda i: (0, i, 0)),
        pl.BlockSpec((HEADS_A, None, VT_ROWS, tm), lambda i: (0, i // per, 0, i % per)),
        pl.BlockSpec((KV_HEADS_B, tm, LANES), lambda i: (0, i, 0)),
        pl.BlockSpec((KV_HEADS_B, tm, LANES), lambda i: (0, i, 0)),
        pl.BlockSpec((KV_HEADS_B, 2, tm, LANES), lambda i: (0, 0, i, 0)),
        pl.BlockSpec((tm, CONV_CH), row),
    )
    return pl.pallas_call(
        functools.partial(_inproj_kernel, rope=rope),
        grid=(n // tm,),
        in_specs=in_specs,
        out_specs=out_specs,
        out_shape=out_shape,
        compiler_params=_cparams(("parallel",)),
        name="inproj_rope" if rope else "inproj_ctx",
    )(*args)


def _diff_attn_kernel(*refs, tq, tk, n_main, lambda_init):
    if n_main:
        (dl_ref, g_ref, q_ref, qn_ref, kl_ref, vl_ref, kc_ref, vc_ref, o_ref,
         m_scr, acc_scr, s_buf0, s_buf1, s_bufc, mc_buf0, mc_buf1, mc_bufc) = refs
    else:
        dl_ref, g_ref, q_ref, kc_ref, vc_ref, o_ref, m_scr, acc_scr = refs

    def stacked_maps(qt):
        sub = lax.broadcasted_iota(jnp.int32, qt.shape, 0)
        zero = jnp.zeros_like(qt)
        return jnp.concatenate([jnp.where(sub < HD, qt, zero), jnp.where(sub < HD, zero, qt)], axis=1)

    qq = stacked_maps(q_ref[...])
    m_scr[...] = jnp.full(m_scr.shape, NEG_INF, f32)
    acc_scr[...] = jnp.zeros(acc_scr.shape, f32)

    def softmax_pv(s, m_chunk, vt):
        m_old = m_scr[...]
        m_new = jnp.maximum(m_old, m_chunk)
        alpha = jnp.exp2(m_old - m_new)
        p = jnp.exp2(s - m_new).astype(bf16)
        acc_scr[...] = alpha * acc_scr[...] + jnp.dot(vt, p, preferred_element_type=f32)
        m_scr[...] = m_new

    if not n_main:
        s_ctx = jnp.dot(kc_ref[...], qq, preferred_element_type=f32)
        softmax_pv(s_ctx, jnp.max(s_ctx, axis=0, keepdims=True), vc_ref[...])
    else:
        def stage_scores(keys, q_stacked, s_buf, mc_buf):
            s = jnp.dot(keys, q_stacked, preferred_element_type=f32)
            s_buf[...] = s
            mc_buf[...] = jnp.max(s, axis=0, keepdims=True)

        def latent_keys(c):
            return kl_ref[pl.ds(pl.multiple_of(c * tk, tk), tk), :]

        bufs = ((s_buf0, mc_buf0), (s_buf1, mc_buf1))
        unroll = PIPE_UNROLL

        @pl.when(pl.program_id(1) == 0)
        def _():
            stage_scores(latent_keys(0), qq, *bufs[0])

        def body(j, carry):
            c = unroll * j
            for u in range(unroll):
                stage_scores(latent_keys(c + u + 1), qq, *bufs[(u + 1) % 2])
                softmax_pv(bufs[u % 2][0][...], bufs[u % 2][1][...], vl_ref[c + u])
            return carry

        lax.fori_loop(0, n_main // unroll - 1, body, 0)
        for c in range(n_main - unroll, n_main):
            if c + 1 < n_main:
                stage_scores(latent_keys(c + 1), qq, *bufs[(c + 1) % 2])
            else:
                stage_scores(kc_ref[...], qq, s_bufc, mc_bufc)
            softmax_pv(bufs[c % 2][0][...], bufs[c % 2][1][...], vl_ref[c])
        stage_scores(latent_keys(0), stacked_maps(qn_ref[...]), *bufs[0])
        softmax_pv(s_bufc[...], mc_bufc[...], vc_ref[...])

    acc = acc_scr[...]
    o0 = acc[:VD_A, :tq] / acc[VD_A:VD_A + 1, :tq]
    o1 = acc[:VD_A, tq:] / acc[VD_A:VD_A + 1, tq:]
    dl = dl_ref[...]
    lam = (jnp.exp(jnp.sum(dl[0:1] * dl[1:2], axis=1, keepdims=True))
           - jnp.exp(jnp.sum(dl[2:3] * dl[3:4], axis=1, keepdims=True)) + lambda_init)
    o = o0 - lam * o1
    y = o * lax.rsqrt(jnp.mean(o * o, axis=0, keepdims=True) + EPS) * g_ref[...]
    o_ref[...] = (y * (1.0 - lambda_init)).T.astype(o_ref.dtype)


def _diff_attention(dl, g_col, qa_t, ka_lat, va_lat, ka_ctx, va_ctx, *, lambda_init, tq, tk):
    nq = qa_t.shape[2]
    c = ka_ctx.shape[1]
    assert va_ctx.shape[1] == 1
    n_main = 0 if ka_lat is None else ka_lat.shape[1] // tk
    in_specs = [
        pl.BlockSpec((4, HD), lambda h, i: (0, 0)),
        pl.BlockSpec((VD_A, 1), lambda h, i: (0, 0)),
        pl.BlockSpec((None, LANES, tq), lambda h, i: (h, 0, i)),
    ]
    args = [dl, g_col, qa_t]
    if n_main:
        s_keys = ka_lat.shape[1]
        assert va_lat.shape[1:] == (n_main, VT_ROWS, tk)
        last_tile = nq // tq - 1
        in_specs += [pl.BlockSpec((None, LANES, tq), lambda h, i: (h, 0, jnp.minimum(i + 1, last_tile))),
                     pl.BlockSpec((None, s_keys, LANES), lambda h, i: (h, 0, 0)),
                     pl.BlockSpec((None, n_main, VT_ROWS, tk), lambda h, i: (h, 0, 0, 0))]
        args += [qa_t, ka_lat, va_lat]
    in_specs += [pl.BlockSpec((None, c, LANES), lambda h, i: (h, 0, 0)),
                 pl.BlockSpec((None, None, VT_ROWS, c), lambda h, i: (h, 0, 0, 0))]
    args += [ka_ctx, va_ctx]
    scratch = [pltpu.VMEM((1, 2 * tq), f32), pltpu.VMEM((VT_ROWS, 2 * tq), f32)]
    if n_main:
        assert n_main % PIPE_UNROLL == 0
        scratch += [pltpu.VMEM((tk, 2 * tq), f32), pltpu.VMEM((tk, 2 * tq), f32), pltpu.VMEM((c, 2 * tq), f32),
                    pltpu.VMEM((1, 2 * tq), f32), pltpu.VMEM((1, 2 * tq), f32), pltpu.VMEM((1, 2 * tq), f32)]
    return pl.pallas_call(
        functools.partial(_diff_attn_kernel, tq=tq, tk=tk, n_main=n_main, lambda_init=lambda_init),
        grid=(HEADS_A, nq // tq),
        in_specs=in_specs,
        out_specs=pl.BlockSpec((tq, VD_A), lambda h, i: (i, h)),
        out_shape=jax.ShapeDtypeStruct((nq, WIDTH_A), bf16),
        scratch_shapes=scratch,
        compiler_params=_cparams(("arbitrary", "arbitrary")),
        name="diff_attn" if n_main else "diff_attn_ctx",
    )(*args)


def _win_attn_kernel(*refs, tq, band, s_len):
    if band:
        (sink_ref, q_ref, kp_ref, ks_ref, kn_ref, kc_ref,
         vp_ref, vs_ref, vn_ref, vc_ref, o_ref) = refs
    else:
        sink_ref, q_ref, kc_ref, vc_ref, o_ref = refs
    i = pl.program_id(0)
    lane = lax.broadcasted_iota(jnp.int32, (tq, LANES), 1)
    if band:
        nloc = tq + 2 * WINDOW
        qpos = i * tq + lax.broadcasted_iota(jnp.int32, (tq, nloc), 0)
        kpos = i * tq - WINDOW + lax.broadcasted_iota(jnp.int32, (tq, nloc), 1)
        valid = (jnp.abs(qpos - kpos) <= WINDOW) & (kpos >= 0) & (kpos < s_len)
    nt = (((1,), (1,)), ((), ()))
    for g in range(KV_HEADS_B):
        q = q_ref[g]
        kc = kc_ref[g]
        if band:
            kloc = jnp.concatenate([kp_ref[g], ks_ref[g], kn_ref[g]], axis=0)
        o = jnp.zeros((tq, LANES), f32)
        for r in range(2):
            sink = sink_ref[2 * g + r] * LOG2_E
            qm = jnp.where((lane < HD) if r == 0 else (lane >= HD), q, jnp.zeros_like(q))
            s_ctx = lax.dot_general(qm, kc, nt, preferred_element_type=f32)
            m = jnp.maximum(jnp.max(s_ctx, axis=1, keepdims=True), sink)
            if band:
                s_loc = lax.dot_general(qm, kloc, nt, preferred_element_type=f32)
                s_loc = jnp.where(valid, s_loc, NEG_INF)
                m = jnp.maximum(m, jnp.max(s_loc, axis=1, keepdims=True))
            e_ctx = jnp.exp2(s_ctx - m)
            l = jnp.sum(e_ctx, axis=1, keepdims=True) + jnp.exp2(sink - m)
            o_r = jnp.dot(e_ctx.astype(bf16), vc_ref[g, r], preferred_element_type=f32)
            if band:
                e_loc = jnp.exp2(s_loc - m)
                l = l + jnp.sum(e_loc, axis=1, keepdims=True)
                vloc = jnp.concatenate([vp_ref[g, r], vs_ref[g, r], vn_ref[g, r]], axis=0)
                o_r = o_r + jnp.dot(e_loc.astype(bf16), vloc, preferred_element_type=f32)
            o = o + o_r / l
        o_ref[:, g * LANES:(g + 1) * LANES] = o.astype(o_ref.dtype)


def _win_attention(sink, qb, kb_lat, vb_lat, kb_ctx, vb_ctx, *, tq):
    nq = qb.shape[1]
    c = kb_ctx.shape[1]
    band = kb_lat is not None
    smem = pl.BlockSpec(memory_space=pltpu.SMEM)
    q_spec = pl.BlockSpec((KV_HEADS_B, tq, LANES), lambda i: (0, i, 0))
    kc_spec = pl.BlockSpec((KV_HEADS_B, c, LANES), lambda i: (0, 0, 0))
    vc_spec = pl.BlockSpec((KV_HEADS_B, 2, c, LANES), lambda i: (0, 0, 0, 0))
    if band:
        per = tq // WINDOW
        last = nq // WINDOW - 1
        prev = lambda i: jnp.maximum(i * per - 1, 0)
        nxt = lambda i: jnp.minimum((i + 1) * per, last)
        in_specs = [
            smem, q_spec,
            pl.BlockSpec((KV_HEADS_B, WINDOW, LANES), lambda i: (0, prev(i), 0)),
            pl.BlockSpec((KV_HEADS_B, tq, LANES), lambda i: (0, i, 0)),
            pl.BlockSpec((KV_HEADS_B, WINDOW, LANES), lambda i: (0, nxt(i), 0)),
            kc_spec,
            pl.BlockSpec((KV_HEADS_B, 2, WINDOW, LANES), lambda i: (0, 0, prev(i), 0)),
            pl.BlockSpec((KV_HEADS_B, 2, tq, LANES), lambda i: (0, 0, i, 0)),
            pl.BlockSpec((KV_HEADS_B, 2, WINDOW, LANES), lambda i: (0, 0, nxt(i), 0)),
            vc_spec,
        ]
        args = [sink, qb, kb_lat, kb_lat, kb_lat, kb_ctx, vb_lat, vb_lat, vb_lat, vb_ctx]
    else:
        in_specs = [smem, q_spec, kc_spec, vc_spec]
        args = [sink, qb, kb_ctx, vb_ctx]
    return pl.pallas_call(
        functools.partial(_win_attn_kernel, tq=tq, band=band, s_len=nq),
        grid=(nq // tq,),
        in_specs=in_specs,
        out_specs=pl.BlockSpec((tq, WIDTH_B), lambda i: (i, 0)),
        out_shape=jax.ShapeDtypeStruct((nq, WIDTH_B), bf16),
        compiler_params=_cparams(("parallel",)),
        name="win_attn" if band else "win_attn_ctx",
    )(*args)


def _conv_kernel(zp_ref, zs_ref, zn_ref, w_ref, b_ref, lg_ref, lb_ref, o_ref, zbuf, zsh, *, tm, rc):
    i = pl.program_id(0)
    n = pl.num_programs(0)
    halo = CONV_HALO
    zero = jnp.zeros((halo, CONV_CH), f32)
    zbuf[0:halo, :] = jnp.where(i > 0, zp_ref[...], zero)
    zbuf[halo:halo + tm, :] = zs_ref[...]
    zbuf[halo + tm:2 * halo + tm, :] = jnp.where(i < n - 1, zn_ref[...], zero)
    w = w_ref[...]
    base = halo - CONV_K // 2
    span = zsh.shape[1]
    for b in range(SUBLANES):
        zsh[b] = zbuf[b:b + span, :]

    for r0 in range(0, tm, rc):
        acc = jnp.zeros((rc, CONV_CH), f32)
        for k in range(CONV_K):
            off = base + k
            start = r0 + SUBLANES * (off // SUBLANES)
            acc = acc + zsh[off % SUBLANES, start:start + rc, :] * w[k:k + 1, :]
        zc = acc + b_ref[...]
        mu = jnp.mean(zc, axis=-1, keepdims=True)
        dz = zc - mu
        var = jnp.mean(dz * dz, axis=-1, keepdims=True)
        zn = dz * lax.rsqrt(var + EPS) * lg_ref[...] + lb_ref[...]
        o_ref[r0:r0 + rc, :] = (zn * jax.nn.sigmoid(zn)).astype(o_ref.dtype)


def _conv_module(z, w, b, ln_g, ln_b, *, tm):
    n = z.shape[0]
    per = tm // CONV_HALO
    last = n // CONV_HALO - 1
    vec = lambda i: (0, 0)
    return pl.pallas_call(
        functools.partial(_conv_kernel, tm=tm, rc=32),
        grid=(n // tm,),
        in_specs=[
            pl.BlockSpec((CONV_HALO, CONV_CH), lambda i: (jnp.maximum(i * per - 1, 0), 0)),
            pl.BlockSpec((tm, CONV_CH), lambda i: (i, 0)),
            pl.BlockSpec((CONV_HALO, CONV_CH), lambda i: (jnp.minimum((i + 1) * per, last), 0)),
            pl.BlockSpec((CONV_K, CONV_CH), vec),
            pl.BlockSpec((1, CONV_CH), vec),
            pl.BlockSpec((1, CONV_CH), vec),
            pl.BlockSpec((1, CONV_CH), vec),
        ],
        out_specs=pl.BlockSpec((tm, CONV_CH), lambda i: (i, 0)),
        out_shape=jax.ShapeDtypeStruct((n, CONV_CH), bf16),
        scratch_shapes=[pltpu.VMEM((tm + 2 * CONV_HALO, CONV_CH), f32),
                        pltpu.VMEM((SUBLANES, tm + 2 * CONV_HALO - SUBLANES, CONV_CH), f32)],
        compiler_params=_cparams(("parallel",)),
        name="conv_module",
    )(z, z, z, w, b, ln_g, ln_b)


def _split_bf16(a):
    hi = a.astype(bf16)
    lo = (a - hi.astype(f32)).astype(bf16)
    return hi, lo


def _route(lt, bias_ref):
    s = [jax.nn.sigmoid(lt[e:e + 1, :]) for e in range(N_EXPERTS)]
    b = [s[e] + bias_ref[e] for e in range(N_EXPERTS)]
    grp = []
    for g in range(N_GROUPS):
        v = b[4 * g:4 * g + 4]
        best = None
        for a_i in range(4):
            for b_i in range(a_i + 1, 4):
                t = v[a_i] + v[b_i]
                best = t if best is None else jnp.maximum(best, t)
        grp.append(best)
    sel = jnp.zeros_like(lt[0:1, :], dtype=jnp.int32)
    gbest = grp[0]
    for g in range(1, N_GROUPS):
        better = grp[g] > gbest
        sel = jnp.where(better, g, sel)
        gbest = jnp.where(better, grp[g], gbest)

    def pick(rows, j):
        out = rows[j]
        for g in range(1, N_GROUPS):
            out = jnp.where(sel == g, rows[4 * g + j], out)
        return out

    vb = [pick(b, j) for j in range(4)]
    vs = [pick(s, j) for j in range(4)]
    i1 = jnp.zeros_like(sel)
    b1 = vb[0]
    for j in range(1, 4):
        better = vb[j] > b1
        i1 = jnp.where(better, j, i1)
        b1 = jnp.where(better, vb[j], b1)
    i2 = jnp.full_like(sel, -1)
    b2 = jnp.full_like(b1, -jnp.inf)
    for j in range(4):
        better = (i1 != j) & ((i2 < 0) | (vb[j] > b2))
        i2 = jnp.where(better, j, i2)
        b2 = jnp.where(better, vb[j], b2)
    s1 = vs[0]
    s2 = vs[0]
    for j in range(1, 4):
        s1 = jnp.where(i1 == j, vs[j], s1)
        s2 = jnp.where(i2 == j, vs[j], s2)
    tot = s1 + s2
    w1 = s1 / tot
    w2 = s2 / tot
    e1 = sel * EXPERTS_PER_GROUP + i1
    e2 = sel * EXPERTS_PER_GROUP + i2
    rows, asg = [], []
    zero = jnp.zeros_like(w1)
    one = jnp.ones_like(w1)
    for e in range(N_EXPERTS):
        rows.append(jnp.where(e1 == e, w1, zero) + jnp.where(e2 == e, w2, zero))
        asg.append(jnp.where((e1 == e) | (e2 == e), one, zero))
    return jnp.concatenate(rows, axis=0), jnp.concatenate(asg, axis=0)


def _outproj_kernel(bias_ref, x_ref, oa_ref, ob_ref, oc_ref, w_ref, gate_ref, sh_ref, sc_ref, g_ref,
                    rwt_ref, xo_ref, h2_ref, comb_ref, rank_ref, cnt_ref):
    y = jnp.dot(oa_ref[...], w_ref[0:WIDTH_A, :], preferred_element_type=f32)
    y = y + jnp.dot(ob_ref[...], w_ref[WIDTH_A:WIDTH_A + WIDTH_B, :], preferred_element_type=f32)
    y = y + jnp.dot(oc_ref[...], w_ref[WIDTH_A + WIDTH_B:, :], preferred_element_type=f32)
    xn = x_ref[...] + gate_ref[...] * y
    xo_ref[...] = xn
    h2 = _rmsnorm_mod(xn, g_ref[...], sh_ref[...], sc_ref[...])
    h2_ref[...] = h2.astype(bf16)
    nt = (((1,), (1,)), ((), ()))
    h_hi, h_lo = _split_bf16(h2)
    r_hi, r_lo = _split_bf16(rwt_ref[...])
    lt = (lax.dot_general(r_hi, h_hi, nt, preferred_element_type=f32)
          + lax.dot_general(r_hi, h_lo, nt, preferred_element_type=f32)
          + lax.dot_general(r_lo, h_hi, nt, preferred_element_type=f32))
    comb, asg = _route(lt, bias_ref)
    tm = lt.shape[1]
    upper = (lax.broadcasted_iota(jnp.int32, (tm, tm), 0) < lax.broadcasted_iota(jnp.int32, (tm, tm), 1))
    rank = jnp.dot(asg.astype(bf16), upper.astype(bf16), preferred_element_type=f32)
    rank = jnp.where(asg > 0.0, rank, -1.0)
    for e in range(N_EXPERTS):
        comb_ref[e] = comb[e:e + 1, :]
        rank_ref[e] = rank[e:e + 1, :]
    cnt = jnp.sum(asg, axis=1, keepdims=True).astype(jnp.int32)
    cnt_ref[...] = jnp.broadcast_to(cnt, cnt_ref.shape)


def _outproj(router_bias, x2, oa, ob, oc, w_bf, layer, gate, shift, scale, g, rwt, *, tm):
    n, d = x2.shape
    row = lambda i: (i, 0)
    const = lambda i: (0, 0)
    return pl.pallas_call(
        _outproj_kernel,
        grid=(n // tm,),
        in_specs=[
            pl.BlockSpec(memory_space=pltpu.SMEM),
            pl.BlockSpec((tm, d), row),
            pl.BlockSpec((tm, WIDTH_A), row),
            pl.BlockSpec((tm, WIDTH_B), row),
            pl.BlockSpec((tm, CONV_CH), row),
            pl.BlockSpec((None, d, d), lambda i: (layer, 0, 0)),
            pl.BlockSpec((1, d), const),
            pl.BlockSpec((1, d), const),
            pl.BlockSpec((1, d), const),
            pl.BlockSpec((1, d), const),
            pl.BlockSpec((N_EXPERTS, d), const),
        ],
        out_specs=(
            pl.BlockSpec((tm, d), row),
            pl.BlockSpec((tm, d), row),
            pl.BlockSpec((N_EXPERTS, 1, tm), lambda i: (0, 0, i)),
            pl.BlockSpec((N_EXPERTS, 1, tm), lambda i: (0, 0, i)),
            pl.BlockSpec((None, N_EXPERTS, LANES), lambda i: (i, 0, 0)),
        ),
        out_shape=(
            jax.ShapeDtypeStruct((n, d), f32),
            jax.ShapeDtypeStruct((n, d), bf16),
            jax.ShapeDtypeStruct((N_EXPERTS, 1, n), f32),
            jax.ShapeDtypeStruct((N_EXPERTS, 1, n), f32),
            jax.ShapeDtypeStruct((n // tm, N_EXPERTS, LANES), jnp.int32),
        ),
        compiler_params=_cparams(("parallel",)),
        name="outproj_router",
    )(router_bias, x2, oa, ob, oc, w_bf, gate, shift, scale, g, rwt)


def _moe_kernel(*refs, final, rows):
    if final:
        nblk_ref, x_ref, h_ref, comb_ref, rank_ref, w1_ref, w3_ref, w2_ref, gate_ref, gf_ref, o_ref, acc = refs
    else:
        nblk_ref, x_ref, h_ref, comb_ref, rank_ref, w1_ref, w3_ref, w2_ref, gate_ref, o_ref, acc = refs
    w = pl.program_id(0)
    step = pl.program_id(1)
    per_step = w1_ref.shape[0]

    @pl.when(step == 0)
    def _():
        acc[...] = jnp.zeros(acc.shape, f32)

    win = comb_ref.shape[2]
    slot0 = lax.broadcasted_iota(jnp.int32, (rows, win), 0).astype(f32)
    trans_lhs = (((0,), (0,)), ((), ()))

    def select(k, first_slot):
        match = rank_ref[k] == slot0 + first_slot
        onehot = jnp.where(match, 1.0, 0.0).astype(bf16)
        cw = jnp.sum(jnp.where(match, comb_ref[k], 0.0), axis=1, keepdims=True)
        return onehot, cw

    def expert(k, xe, cw):
        a = jnp.dot(xe, w1_ref[k], preferred_element_type=f32)
        g = jnp.dot(xe, w3_ref[k], preferred_element_type=f32)
        u = (a * jax.nn.sigmoid(a) * g).astype(bf16)
        y = jnp.dot(u, w2_ref[k], preferred_element_type=f32)
        return (cw * y).astype(bf16)

    nblk = [nblk_ref[w * N_EXPERTS + step * per_step + k] for k in range(per_step)]
    single = nblk[0] <= 1
    for k in range(1, per_step):
        single = jnp.logical_and(single, nblk[k] <= 1)

    @pl.when(single)
    def _():
        picks = [select(k, 0.0) for k in range(per_step)]
        onehot = jnp.concatenate([p[0] for p in picks], axis=0)
        xe = jnp.dot(onehot, h_ref[...], preferred_element_type=f32).astype(bf16)
        yw = jnp.concatenate([expert(k, xe[k * rows:(k + 1) * rows], picks[k][1]) for k in range(per_step)],
                             axis=0)
        acc[...] += lax.dot_general(onehot, yw, trans_lhs, preferred_element_type=f32)

    @pl.when(jnp.logical_not(single))
    def _():
        for k in range(per_step):
            def block(b, carry, k=k):
                onehot, cw = select(k, (b * rows).astype(f32))
                xe = jnp.dot(onehot, h_ref[...], preferred_element_type=f32).astype(bf16)
                acc[...] += lax.dot_general(onehot, expert(k, xe, cw), trans_lhs, preferred_element_type=f32)
                return carry

            lax.fori_loop(0, nblk[k], block, 0)

    @pl.when(step == pl.num_programs(1) - 1)
    def _():
        xn = x_ref[...] + gate_ref[...] * acc[...]
        if final:
            xn = xn * lax.rsqrt(jnp.mean(xn * xn, axis=-1, keepdims=True) + EPS) * gf_ref[...]
        o_ref[...] = xn


def _moe(x2, h2, comb, rank, counts, w1, w3, w2, layer, gate, g_final, *, win):
    n, d = x2.shape
    final = g_final is not None
    rows = MOE_BLOCK_ROWS
    nblk = ((counts + (rows - 1)) // rows).reshape(-1).astype(jnp.int32)
    row = lambda i, e, nb: (i, 0)
    const = lambda i, e, nb: (0, 0)
    per_expert = lambda i, e, nb: (e, 0, i)
    weight = lambda i, e, nb: (layer, e, 0, 0)
    eps = MOE_EXPERTS_PER_STEP
    in_specs = [
        pl.BlockSpec((win, d), row),
        pl.BlockSpec((win, d), row),
        pl.BlockSpec((eps, 1, win), per_expert),
        pl.BlockSpec((eps, 1, win), per_expert),
        pl.BlockSpec((None, eps, d, EXPERT_FF), weight),
        pl.BlockSpec((None, eps, d, EXPERT_FF), weight),
        pl.BlockSpec((None, eps, EXPERT_FF, d), weight),
        pl.BlockSpec((1, d), const),
    ]
    args = [x2, h2, comb, rank, w1, w3, w2, gate]
    if final:
        in_specs.append(pl.BlockSpec((1, d), const))
        args.append(g_final)
    return pl.pallas_call(
        functools.partial(_moe_kernel, final=final, rows=rows),
        grid_spec=pltpu.PrefetchScalarGridSpec(
            num_scalar_prefetch=1,
            grid=(n // win, N_EXPERTS // eps),
            in_specs=in_specs,
            out_specs=pl.BlockSpec((win, d), row),
            scratch_shapes=[pltpu.VMEM((win, d), f32)],
        ),
        out_shape=jax.ShapeDtypeStruct((n, d), f32),
        compiler_params=_cparams(("parallel", "arbitrary")),
        name="moe_final" if final else "moe",
    )(nblk, *args)


def _rope_tables(n_tok):
    rows = n_tok // GRID_W
    row = np.repeat(np.arange(rows, dtype=np.float64), GRID_W)
    col = np.tile(np.arange(GRID_W, dtype=np.float64), rows)
    n_freq = HD // 4
    inv = ROPE_BASE ** (-np.arange(n_freq, dtype=np.float64) / n_freq)
    lane = np.arange(LANES)
    j = lane % HD
    use_col = (j // 32) == 1
    freq = inv[j % n_freq]
    ang = np.where(use_col[None, :], col[:, None], row[:, None]) * freq[None, :]
    sign = np.where((lane % 32) < 16, -1.0, 1.0)
    return (jnp.asarray(np.cos(ang).astype(np.float32)),
            jnp.asarray((np.sin(ang) * sign[None, :]).astype(np.float32)))


def kernel(x, c, ctx, c_ctx, w_ada, b_ada, g_mix, w_in, diff_lambda, diff_norm_g, attn_sink, conv_w, conv_b,
           conv_ln_g, conv_ln_b, w_out, g_ffn, router_w, router_bias, w1, w3, w2, g_final):
    bsz, s_len, d = x.shape
    c_len = ctx.shape[1]
    depth = w_ada.shape[0]
    assert bsz == 1 and d == D_MODEL and s_len % 1024 == 0 and c_len % 256 == 0
    tm_lat = 256
    tm_ctx = min(c_len, 256)
    win_lat = MOE_WINDOW
    tq_a, tk_a = 256, 1024

    xl = x.reshape(s_len, d)
    xc = ctx.reshape(c_len, d)
    ct = jnp.zeros((d, LANES), f32).at[:, 0].set(c[0]).at[:, 1].set(c_ctx)
    mod_all = _modulation(ct, w_ada, b_ada)
    cos_t, sin_t = _rope_tables(s_len)
    rwt = router_w.T
    w_in_bf = w_in.astype(bf16)
    w_out_bf = w_out.astype(bf16)
    w1_bf, w3_bf, w2_bf = w1.astype(bf16), w3.astype(bf16), w2.astype(bf16)
    vec = lambda a: a.reshape(1, -1)

    for l in range(depth):
        last = l == depth - 1
        lambda_init = 0.8 - 0.6 * math.exp(-0.3 * l)
        ml = [mod_all[l, 0:1, k * d:(k + 1) * d] for k in range(6)]
        mc = [mod_all[l, 1:2, k * d:(k + 1) * d] for k in range(6)]
        g_mix_l, g_ffn_l = vec(g_mix[l]), vec(g_ffn[l])
        dl, dng = diff_lambda[l], diff_norm_g[l].reshape(-1, 1)

        qa, ka, va, qb, kb, vb, z = _inproj(xl, ml[0], ml[1], g_mix_l, w_in_bf, l, cos_t, sin_t, tm=tm_lat,
                                            tk=tk_a)
        qa_c, ka_c, va_c, qb_c, kb_c, vb_c, z_c = _inproj(xc, mc[0], mc[1], g_mix_l, w_in_bf, l, None, None,
                                                          tm=tm_ctx, tk=c_len)

        o_a = _diff_attention(dl, dng, qa, ka, va, ka_c, va_c, lambda_init=lambda_init, tq=tq_a, tk=tk_a)
        o_b = _win_attention(attn_sink[l], qb, kb, vb, kb_c, vb_c, tq=256)
        o_c = _conv_module(z, conv_w[l], vec(conv_b[l]), vec(conv_ln_g[l]), vec(conv_ln_b[l]), tm=tm_lat)
        xl, h2, comb, rank, cnt = _outproj(router_bias, xl, o_a, o_b, o_c, w_out_bf, l, ml[2], ml[3], ml[4],
                                           g_ffn_l, rwt, tm=win_lat)
        xl = _moe(xl, h2, comb, rank, cnt[:, :, 0], w1_bf, w3_bf, w2_bf, l, ml[5],
                  vec(g_final) if last else None, win=win_lat)

        if not last:
            o_ac = _diff_attention(dl, dng, qa_c, None, None, ka_c, va_c, lambda_init=lambda_init,
                                   tq=tm_ctx, tk=tk_a)
            o_bc = _win_attention(attn_sink[l], qb_c, None, None, kb_c, vb_c, tq=tm_ctx)
            o_cc = _conv_module(z_c, conv_w[l], vec(conv_b[l]), vec(conv_ln_g[l]), vec(conv_ln_b[l]), tm=tm_ctx)
            xc, h2c, comb_c, rank_c, cnt_c = _outproj(router_bias, xc, o_ac, o_bc, o_cc, w_out_bf, l, mc[2], mc[3],
                                                      mc[4], g_ffn_l, rwt, tm=tm_ctx)
            xc = _moe(xc, h2c, comb_c, rank_c, cnt_c[:, :, 0], w1_bf, w3_bf, w2_bf, l, mc[5], None,
                      win=tm_ctx)

    return xl.reshape(bsz, s_len, d)
```

```python
import functools
import math

import numpy as np
import jax
import jax.numpy as jnp
from jax import lax
from jax.experimental import pallas as pl
from jax.experimental.pallas import tpu as pltpu

f32 = jnp.float32
bf16 = jnp.bfloat16

D_MODEL = 1024
GRID_W = 64
HEADS_A = 4
HD = 64
VD_A = 2 * HD
VT_ROWS = VD_A + 16
WIDTH_A = HEADS_A * VD_A
HEADS_B = 4
KV_HEADS_B = 2
WIDTH_B = HEADS_B * HD
WINDOW = 128
CONV_CH = 256
CONV_K = 31
CONV_HALO = 16
IN_WIDTH = 2560
OFF_QA, OFF_KA, OFF_VA, OFF_QB, OFF_KB, OFF_VB, OFF_UC = 0, 512, 1024, 1536, 1792, 1920, 2048
N_EXPERTS = 16
N_GROUPS = 4
EXPERTS_PER_GROUP = 4
EXPERT_FF = 512
ROPE_BASE = 10000.0
EPS = 1e-6
NEG_INF = -1e30
LANES = 128
SUBLANES = 8
QK_SCALE = HD ** -0.5
LOG2_E = math.log2(math.e)
QK_SCALE_LOG2 = QK_SCALE * LOG2_E

MOE_WINDOW = 1024
MOE_EXPERTS_PER_STEP = 4
MOE_BLOCK_ROWS = 128
MOE_CHUNK_BLOCKS = 2
PIPE_UNROLL = 4
VMEM_LIMIT =56 * 1024 * 1024


def _cparams(sem):
    return pltpu.CompilerParams(dimension_semantics=sem, vmem_limit_bytes=VMEM_LIMIT)


def _mod_kernel(ct_ref, w_ref, b_ref, o_ref):
    tn = w_ref.shape[1]

    def body(i, carry):
        a0, a1 = carry
        r = pl.multiple_of(i * 8, 8)
        cv = ct_ref[pl.ds(r, 8), :]
        sv = cv * jax.nn.sigmoid(cv)
        w8 = w_ref[pl.ds(r, 8), :]
        return a0 + w8 * sv[:, 0:1], a1 + w8 * sv[:, 1:2]

    z = jnp.zeros((8, tn), f32)
    a0, a1 = lax.fori_loop(0, w_ref.shape[0] // 8, body, (z, z), unroll=8)
    r0 = jnp.sum(a0, axis=0, keepdims=True) + b_ref[...]
    r1 = jnp.sum(a1, axis=0, keepdims=True) + b_ref[...]
    o_ref[...] = jnp.concatenate([r0, r1, jnp.zeros((6, tn), f32)], axis=0)


def _modulation(ct, w_ada, b_ada):
    depth, d, n = w_ada.shape
    tn = 1536
    return pl.pallas_call(
        _mod_kernel,
        grid=(depth, n // tn),
        in_specs=[
            pl.BlockSpec((d, LANES), lambda l, j: (0, 0)),
            pl.BlockSpec((None, d, tn), lambda l, j: (l, 0, j)),
            pl.BlockSpec((None, 1, tn), lambda l, j: (l, 0, j)),
        ],
        out_specs=pl.BlockSpec((None, 8, tn), lambda l, j: (l, 0, j)),
        out_shape=jax.ShapeDtypeStruct((depth, 8, n), f32),
        compiler_params=_cparams(("arbitrary", "arbitrary")),
        name="modulation",
    )(ct, w_ada, b_ada.reshape(depth, 1, n))


def _rmsnorm_mod(xf, g, shift, scale):
    y = xf * lax.rsqrt(jnp.mean(xf * xf, axis=-1, keepdims=True) + EPS) * g
    return y * (1.0 + scale) + shift


def _inproj_kernel(*refs, rope):
    if rope:
        x_ref, sh_ref, sc_ref, g_ref, w_ref, cos_ref, sin_ref = refs[:7]
        outs = refs[7:]
    else:
        x_ref, sh_ref, sc_ref, g_ref, w_ref = refs[:5]
        outs = refs[5:]
    qa_ref, ka_ref, va_ref, qb_ref, kb_ref, vb_ref, z_ref = outs

    h = _rmsnorm_mod(x_ref[...], g_ref[...], sh_ref[...], sc_ref[...])
    p = jnp.dot(h.astype(bf16), w_ref[...], preferred_element_type=f32)
    tm = p.shape[0]
    lane = lax.broadcasted_iota(jnp.int32, (tm, LANES), 1)
    lo_half = lane < HD

    def rot(xc, scale):
        if rope:
            first = (lane % 32) < 16
            partner = jnp.where(first, pltpu.roll(xc, LANES - 16, 1), pltpu.roll(xc, 16, 1))
            xc = xc * cos_ref[...] + partner * sin_ref[...]
        return xc * scale if scale != 1.0 else xc

    def chunk(off, j):
        return p[:, off + j * LANES: off + (j + 1) * LANES]

    ones = jnp.ones((VT_ROWS - VD_A, tm), f32)
    for hh in range(HEADS_A):
        qa_ref[hh] = rot(chunk(OFF_QA, hh), QK_SCALE_LOG2).T.astype(bf16)
        ka_ref[hh] = rot(chunk(OFF_KA, hh), 1.0).astype(bf16)
        va_ref[hh] = jnp.concatenate([chunk(OFF_VA, hh).T, ones], axis=0).astype(bf16)
    for g in range(KV_HEADS_B):
        qb_ref[g] = rot(chunk(OFF_QB, g), QK_SCALE_LOG2).astype(bf16)
    kb = rot(chunk(OFF_KB, 0), 1.0)
    kb_sw = pltpu.roll(kb, HD, 1)
    kb_ref[0] = jnp.where(lo_half, kb, kb_sw).astype(bf16)
    kb_ref[1] = jnp.where(lo_half, kb_sw, kb).astype(bf16)
    vb = chunk(OFF_VB, 0)
    vb_sw = pltpu.roll(vb, HD, 1)
    zero = jnp.zeros_like(vb)
    vb_ref[0, 0] = jnp.where(lo_half, vb, zero).astype(bf16)
    vb_ref[0, 1] = jnp.where(lo_half, zero, vb_sw).astype(bf16)
    vb_ref[1, 0] = jnp.where(lo_half, vb_sw, zero).astype(bf16)
    vb_ref[1, 1] = jnp.where(lo_half, zero, vb).astype(bf16)
    a = p[:, OFF_UC: OFF_UC + CONV_CH]
    gt = p[:, OFF_UC + CONV_CH: OFF_UC + 2 * CONV_CH]
    z_ref[...] = a * jax.nn.sigmoid(gt)


def _inproj(x2, shift, scale, g, w_bf, layer, cos_t, sin_t, *, tm, tk):
    n, d = x2.shape
    rope = cos_t is not None
    per = tk // tm
    row = lambda i: (i, 0)
    const = lambda i: (0, 0)
    in_specs = [
        pl.BlockSpec((tm, d), row),
        pl.BlockSpec((1, d), const),
        pl.BlockSpec((1, d), const),
        pl.BlockSpec((1, d), const),
        pl.BlockSpec((None, d, IN_WIDTH), lambda i: (layer, 0, 0)),
    ]
    args = [x2, shift, scale, g, w_bf]
    if rope:
        in_specs += [pl.BlockSpec((tm, LANES), row), pl.BlockSpec((tm, LANES), row)]
        args += [cos_t, sin_t]
    out_shape = (
        jax.ShapeDtypeStruct((HEADS_A, LANES, n), bf16),
        jax.ShapeDtypeStruct((HEADS_A, n, LANES), bf16),
        jax.ShapeDtypeStruct((HEADS_A, n // tk, VT_ROWS, tk), bf16),
        jax.ShapeDtypeStruct((KV_HEADS_B, n, LANES), bf16),
        jax.ShapeDtypeStruct((KV_HEADS_B, n, LANES), bf16),
        jax.ShapeDtypeStruct((KV_HEADS_B, 2, n, LANES), bf16),
        jax.ShapeDtypeStruct((n, CONV_CH), f32),
    )
    out_specs = (
        pl.BlockSpec((HEADS_A, LANES, tm), lambda i: (0, 0, i)),
        pl.BlockSpec((HEADS_A, tm, LANES), lambda i: (0, i, 0)),
        pl.BlockSpec((HEADS_A, None, VT_ROWS, tm), lambda i: (0, i // per, 0, i % per)),
        pl.BlockSpec((KV_HEADS_B, tm, LANES), lambda i: (0, i, 0)),
        pl.BlockSpec((KV_HEADS_B, tm, LANES), lambda i: (0, i, 0)),
        pl.BlockSpec((KV_HEADS_B, 2, tm, LANES), lambda i: (0, 0, i, 0)),
        pl.BlockSpec((tm, CONV_CH), row),
    )
    return pl.pallas_call(
        functools.partial(_inproj_kernel, rope=rope),
        grid=(n // tm,),
        in_specs=in_specs,
        out_specs=out_specs,
        out_shape=out_shape,
        compiler_params=_cparams(("parallel",)),
        name="inproj_rope" if rope else "inproj_ctx",
    )(*args)


def _diff_attn_kernel(*refs, tq, tk, n_main, lambda_init):
    if n_main:
        (dl_ref, g_ref, q_ref, qn_ref, kl_ref, vl_ref, kc_ref, vc_ref, o_ref,
         m_scr, acc_scr, s_buf0, s_buf1, s_bufc, mc_buf0, mc_buf1, mc_bufc) = refs
    else:
        dl_ref, g_ref, q_ref, kc_ref, vc_ref, o_ref, m_scr, acc_scr = refs

    def stacked_maps(qt):
        sub = lax.broadcasted_iota(jnp.int32, qt.shape, 0)
        zero = jnp.zeros_like(qt)
        return jnp.concatenate([jnp.where(sub < HD, qt, zero), jnp.where(sub < HD, zero, qt)], axis=1)

    qq = stacked_maps(q_ref[...])
    m_scr[...] = jnp.full(m_scr.shape, NEG_INF, f32)
    acc_scr[...] = jnp.zeros(acc_scr.shape, f32)

    def softmax_pv(s, m_chunk, vt):
        m_old = m_scr[...]
        m_new = jnp.maximum(m_old, m_chunk)
        alpha = jnp.exp2(m_old - m_new)
        p = jnp.exp2(s - m_new).astype(bf16)
        acc_scr[...] = alpha * acc_scr[...] + jnp.dot(vt, p, preferred_element_type=f32)
        m_scr[...] = m_new

    if not n_main:
        s_ctx = jnp.dot(kc_ref[...], qq, preferred_element_type=f32)
        softmax_pv(s_ctx, jnp.max(s_ctx, axis=0, keepdims=True), vc_ref[...])
    else:
        def stage_scores(keys, q_stacked, s_buf, mc_buf):
            s = jnp.dot(keys, q_stacked, preferred_element_type=f32)
            s_buf[...] = s
            mc_buf[...] = jnp.max(s, axis=0, keepdims=True)

        def latent_keys(c):
            return kl_ref[pl.ds(pl.multiple_of(c * tk, tk), tk), :]

        bufs = ((s_buf0, mc_buf0), (s_buf1, mc_buf1))
        unroll = PIPE_UNROLL

        @pl.when(pl.program_id(1) == 0)
        def _():
            stage_scores(latent_keys(0), qq, *bufs[0])

        def body(j, carry):
            c = unroll * j
            for u in range(unroll):
                stage_scores(latent_keys(c + u + 1), qq, *bufs[(u + 1) % 2])
                softmax_pv(bufs[u % 2][0][...], bufs[u % 2][1][...], vl_ref[c + u])
            return carry

        lax.fori_loop(0, n_main // unroll - 1, body, 0)
        for c in range(n_main - unroll, n_main):
            if c + 1 < n_main:
                stage_scores(latent_keys(c + 1), qq, *bufs[(c + 1) % 2])
            else:
                stage_scores(kc_ref[...], qq, s_bufc, mc_bufc)
            softmax_pv(bufs[c % 2][0][...], bufs[c % 2][1][...], vl_ref[c])
        stage_scores(latent_keys(0), stacked_maps(qn_ref[...]), *bufs[0])
        softmax_pv(s_bufc[...], mc_bufc[...], vc_ref[...])

    acc = acc_scr[...]
    o0 = acc[:VD_A, :tq] / acc[VD_A:VD_A + 1, :tq]
    o1 = acc[:VD_A, tq:] / acc[VD_A:VD_A + 1, tq:]
    dl = dl_ref[...]
    lam = (jnp.exp(jnp.sum(dl[0:1] * dl[1:2], axis=1, keepdims=True))
           - jnp.exp(jnp.sum(dl[2:3] * dl[3:4], axis=1, keepdims=True)) + lambda_init)
    o = o0 - lam * o1
    y = o * lax.rsqrt(jnp.mean(o * o, axis=0, keepdims=True) + EPS) * g_ref[...]
    o_ref[...] = (y * (1.0 - lambda_init)).T.astype(o_ref.dtype)


def _diff_attention(dl, g_col, qa_t, ka_lat, va_lat, ka_ctx, va_ctx, *, lambda_init, tq, tk):
    nq = qa_t.shape[2]
    c = ka_ctx.shape[1]
    assert va_ctx.shape[1] == 1
    n_main = 0 if ka_lat is None else ka_lat.shape[1] // tk
    in_specs = [
        pl.BlockSpec((4, HD), lambda h, i: (0, 0)),
        pl.BlockSpec((VD_A, 1), lambda h, i: (0, 0)),
        pl.BlockSpec((None, LANES, tq), lambda h, i: (h, 0, i)),
    ]
    args = [dl, g_col, qa_t]
    if n_main:
        s_keys = ka_lat.shape[1]
        assert va_lat.shape[1:] == (n_main, VT_ROWS, tk)
        last_tile = nq // tq - 1
        in_specs += [pl.BlockSpec((None, LANES, tq), lambda h, i: (h, 0, jnp.minimum(i + 1, last_tile))),
                     pl.BlockSpec((None, s_keys, LANES), lambda h, i: (h, 0, 0)),
                     pl.BlockSpec((None, n_main, VT_ROWS, tk), lambda h, i: (h, 0, 0, 0))]
        args += [qa_t, ka_lat, va_lat]
    in_specs += [pl.BlockSpec((None, c, LANES), lambda h, i: (h, 0, 0)),
                 pl.BlockSpec((None, None, VT_ROWS, c), lambda h, i: (h, 0, 0, 0))]
    args += [ka_ctx, va_ctx]
    scratch = [pltpu.VMEM((1, 2 * tq), f32), pltpu.VMEM((VT_ROWS, 2 * tq), f32)]
    if n_main:
        assert n_main % PIPE_UNROLL == 0
        scratch += [pltpu.VMEM((tk, 2 * tq), f32), pltpu.VMEM((tk, 2 * tq), f32), pltpu.VMEM((c, 2 * tq), f32),
                    pltpu.VMEM((1, 2 * tq), f32), pltpu.VMEM((1, 2 * tq), f32), pltpu.VMEM((1, 2 * tq), f32)]
    return pl.pallas_call(
        functools.partial(_diff_attn_kernel, tq=tq, tk=tk, n_main=n_main, lambda_init=lambda_init),
        grid=(HEADS_A, nq // tq),
        in_specs=in_specs,
        out_specs=pl.BlockSpec((tq, VD_A), lambda h, i: (i, h)),
        out_shape=jax.ShapeDtypeStruct((nq, WIDTH_A), bf16),
        scratch_shapes=scratch,
        compiler_params=_cparams(("arbitrary", "arbitrary")),
        name="diff_attn" if n_main else "diff_attn_ctx",
    )(*args)


def _win_attn_kernel(*refs, tq, band, s_len):
    if band:
        (sink_ref, q_ref, kp_ref, ks_ref, kn_ref, kc_ref,
         vp_ref, vs_ref, vn_ref, vc_ref, o_ref) = refs
    else:
        sink_ref, q_ref, kc_ref, vc_ref, o_ref = refs
    i = pl.program_id(0)
    lane = lax.broadcasted_iota(jnp.int32, (tq, LANES), 1)
    if band:
        nloc = tq + 2 * WINDOW
        qpos = i * tq + lax.broadcasted_iota(jnp.int32, (tq, nloc), 0)
        kpos = i * tq - WINDOW + lax.broadcasted_iota(jnp.int32, (tq, nloc), 1)
        valid = (jnp.abs(qpos - kpos) <= WINDOW) & (kpos >= 0) & (kpos < s_len)
    nt = (((1,), (1,)), ((), ()))
    for g in range(KV_HEADS_B):
        q = q_ref[g]
        kc = kc_ref[g]
        if band:
            kloc = jnp.concatenate([kp_ref[g], ks_ref[g], kn_ref[g]], axis=0)
        o = jnp.zeros((tq, LANES), f32)
        for r in range(2):
            sink = sink_ref[2 * g + r] * LOG2_E
            qm = jnp.where((lane < HD) if r == 0 else (lane >= HD), q, jnp.zeros_like(q))
            s_ctx = lax.dot_general(qm, kc, nt, preferred_element_type=f32)
            m = jnp.maximum(jnp.max(s_ctx, axis=1, keepdims=True), sink)
            if band:
                s_loc = lax.dot_general(qm, kloc, nt, preferred_element_type=f32)
                s_loc = jnp.where(valid, s_loc, NEG_INF)
                m = jnp.maximum(m, jnp.max(s_loc, axis=1, keepdims=True))
            e_ctx = jnp.exp2(s_ctx - m)
            l = jnp.sum(e_ctx, axis=1, keepdims=True) + jnp.exp2(sink - m)
            o_r = jnp.dot(e_ctx.astype(bf16), vc_ref[g, r], preferred_element_type=f32)
            if band:
                e_loc = jnp.exp2(s_loc - m)
                l = l + jnp.sum(e_loc, axis=1, keepdims=True)
                vloc = jnp.concatenate([vp_ref[g, r], vs_ref[g, r], vn_ref[g, r]], axis=0)
                o_r = o_r + jnp.dot(e_loc.astype(bf16), vloc, preferred_element_type=f32)
            o = o + o_r / l
        o_ref[:, g * LANES:(g + 1) * LANES] = o.astype(o_ref.dtype)


def _win_attention(sink, qb, kb_lat, vb_lat, kb_ctx, vb_ctx, *, tq):
    nq = qb.shape[1]
    c = kb_ctx.shape[1]
    band = kb_lat is not None
    smem = pl.BlockSpec(memory_space=pltpu.SMEM)
    q_spec = pl.BlockSpec((KV_HEADS_B, tq, LANES), lambda i: (0, i, 0))
    kc_spec = pl.BlockSpec((KV_HEADS_B, c, LANES), lambda i: (0, 0, 0))
    vc_spec = pl.BlockSpec((KV_HEADS_B, 2, c, LANES), lambda i: (0, 0, 0, 0))
    if band:
        per = tq // WINDOW
        last = nq // WINDOW - 1
        prev = lambda i: jnp.maximum(i * per - 1, 0)
        nxt = lambda i: jnp.minimum((i + 1) * per, last)
        in_specs = [
            smem, q_spec,
            pl.BlockSpec((KV_HEADS_B, WINDOW, LANES), lambda i: (0, prev(i), 0)),
            pl.BlockSpec((KV_HEADS_B, tq, LANES), lambda i: (0, i, 0)),
            pl.BlockSpec((KV_HEADS_B, WINDOW, LANES), lambda i: (0, nxt(i), 0)),
            kc_spec,
            pl.BlockSpec((KV_HEADS_B, 2, WINDOW, LANES), lambda i: (0, 0, prev(i), 0)),
            pl.BlockSpec((KV_HEADS_B, 2, tq, LANES), lambda i: (0, 0, i, 0)),
            pl.BlockSpec((KV_HEADS_B, 2, WINDOW, LANES), lambda i: (0, 0, nxt(i), 0)),
            vc_spec,
        ]
        args = [sink, qb, kb_lat, kb_lat, kb_lat, kb_ctx, vb_lat, vb_lat, vb_lat, vb_ctx]
    else:
        in_specs = [smem, q_spec, kc_spec, vc_spec]
        args = [sink, qb, kb_ctx, vb_ctx]
    return pl.pallas_call(
        functools.partial(_win_attn_kernel, tq=tq, band=band, s_len=nq),
        grid=(nq // tq,),
        in_specs=in_specs,
        out_specs=pl.BlockSpec((tq, WIDTH_B), lambda i: (i, 0)),
        out_shape=jax.ShapeDtypeStruct((nq, WIDTH_B), bf16),
        compiler_params=_cparams(("parallel",)),
        name="win_attn" if band else "win_attn_ctx",
    )(*args)


def _conv_kernel(zp_ref, zs_ref, zn_ref, w_ref, b_ref, lg_ref, lb_ref, o_ref, zbuf, zsh, *, tm, rc):
    i = pl.program_id(0)
    n = pl.num_programs(0)
    halo = CONV_HALO
    zero = jnp.zeros((halo, CONV_CH), f32)
    zbuf[0:halo, :] = jnp.where(i > 0, zp_ref[...], zero)
    zbuf[halo:halo + tm, :] = zs_ref[...]
    zbuf[halo + tm:2 * halo + tm, :] = jnp.where(i < n - 1, zn_ref[...], zero)
    w = w_ref[...]
    base = halo - CONV_K // 2
    span = zsh.shape[1]
    for b in range(SUBLANES):
        zsh[b] = zbuf[b:b + span, :]

    for r0 in range(0, tm, rc):
        acc = jnp.zeros((rc, CONV_CH), f32)
        for k in range(CONV_K):
            off = base + k
            start = r0 + SUBLANES * (off // SUBLANES)
            acc = acc + zsh[off % SUBLANES, start:start + rc, :] * w[k:k + 1, :]
        zc = acc + b_ref[...]
        mu = jnp.mean(zc, axis=-1, keepdims=True)
        dz = zc - mu
        var = jnp.mean(dz * dz, axis=-1, keepdims=True)
        zn = dz * lax.rsqrt(var + EPS) * lg_ref[...] + lb_ref[...]
        o_ref[r0:r0 + rc, :] = (zn * jax.nn.sigmoid(zn)).astype(o_ref.dtype)


def _conv_module(z, w, b, ln_g, ln_b, *, tm):
    n = z.shape[0]
    per = tm // CONV_HALO
    last = n // CONV_HALO - 1
    vec = lambda i: (0, 0)
    return pl.pallas_call(
        functools.partial(_conv_kernel, tm=tm, rc=32),
        grid=(n // tm,),
        in_specs=[
            pl.BlockSpec((CONV_HALO, CONV_CH), lambda i: (jnp.maximum(i * per - 1, 0), 0)),
            pl.BlockSpec((tm, CONV_CH), lambda i: (i, 0)),
            pl.BlockSpec((CONV_HALO, CONV_CH), lambda i: (jnp.minimum((i + 1) * per, last), 0)),
            pl.BlockSpec((CONV_K, CONV_CH), vec),
            pl.BlockSpec((1, CONV_CH), vec),
            pl.BlockSpec((1, CONV_CH), vec),
            pl.BlockSpec((1, CONV_CH), vec),
        ],
        out_specs=pl.BlockSpec((tm, CONV_CH), lambda i: (i, 0)),
        out_shape=jax.ShapeDtypeStruct((n, CONV_CH), bf16),
        scratch_shapes=[pltpu.VMEM((tm + 2 * CONV_HALO, CONV_CH), f32),
                        pltpu.VMEM((SUBLANES, tm + 2 * CONV_HALO - SUBLANES, CONV_CH), f32)],
        compiler_params=_cparams(("parallel",)),
        name="conv_module",
    )(z, z, z, w, b, ln_g, ln_b)


def _split_bf16(a):
    hi = a.astype(bf16)
    lo = (a - hi.astype(f32)).astype(bf16)
    return hi, lo


def _route(lt, bias_ref):
    s = [jax.nn.sigmoid(lt[e:e + 1, :]) for e in range(N_EXPERTS)]
    b = [s[e] + bias_ref[e] for e in range(N_EXPERTS)]
    grp = []
    for g in range(N_GROUPS):
        v = b[4 * g:4 * g + 4]
        best = None
        for a_i in range(4):
            for b_i in range(a_i + 1, 4):
                t = v[a_i] + v[b_i]
                best = t if best is None else jnp.maximum(best, t)
        grp.append(best)
    sel = jnp.zeros_like(lt[0:1, :], dtype=jnp.int32)
    gbest = grp[0]
    for g in range(1, N_GROUPS):
        better = grp[g] > gbest
        sel = jnp.where(better, g, sel)
        gbest = jnp.where(better, grp[g], gbest)

    def pick(rows, j):
        out = rows[j]
        for g in range(1, N_GROUPS):
            out = jnp.where(sel == g, rows[4 * g + j], out)
        return out

    vb = [pick(b, j) for j in range(4)]
    vs = [pick(s, j) for j in range(4)]
    i1 = jnp.zeros_like(sel)
    b1 = vb[0]
    for j in range(1, 4):
        better = vb[j] > b1
        i1 = jnp.where(better, j, i1)
        b1 = jnp.where(better, vb[j], b1)
    i2 = jnp.full_like(sel, -1)
    b2 = jnp.full_like(b1, -jnp.inf)
    for j in range(4):
        better = (i1 != j) & ((i2 < 0) | (vb[j] > b2))
        i2 = jnp.where(better, j, i2)
        b2 = jnp.where(better, vb[j], b2)
    s1 = vs[0]
    s2 = vs[0]
    for j in range(1, 4):
        s1 = jnp.where(i1 == j, vs[j], s1)
        s2 = jnp.where(i2 == j, vs[j], s2)
    tot = s1 + s2
    w1 = s1 / tot
    w2 = s2 / tot
    e1 = sel * EXPERTS_PER_GROUP + i1
    e2 = sel * EXPERTS_PER_GROUP + i2
    rows, asg = [], []
    zero = jnp.zeros_like(w1)
    one = jnp.ones_like(w1)
    for e in range(N_EXPERTS):
        rows.append(jnp.where(e1 == e, w1, zero) + jnp.where(e2 == e, w2, zero))
        asg.append(jnp.where((e1 == e) | (e2 == e), one, zero))
    return jnp.concatenate(rows, axis=0), jnp.concatenate(asg, axis=0)


def _outproj_kernel(bias_ref, x_ref, oa_ref, ob_ref, oc_ref, w_ref, gate_ref, sh_ref, sc_ref, g_ref,
                    rwt_ref, xo_ref, h2_ref, comb_ref, rank_ref, cnt_ref):
    y = jnp.dot(oa_ref[...], w_ref[0:WIDTH_A, :], preferred_element_type=f32)
    y = y + jnp.dot(ob_ref[...], w_ref[WIDTH_A:WIDTH_A + WIDTH_B, :], preferred_element_type=f32)
    y = y + jnp.dot(oc_ref[...], w_ref[WIDTH_A + WIDTH_B:, :], preferred_element_type=f32)
    xn = x_ref[...] + gate_ref[...] * y
    xo_ref[...] = xn
    h2 = _rmsnorm_mod(xn, g_ref[...], sh_ref[...], sc_ref[...])
    h2_ref[...] = h2.astype(bf16)
    nt = (((1,), (1,)), ((), ()))
    h_hi, h_lo = _split_bf16(h2)
    r_hi, r_lo = _split_bf16(rwt_ref[...])
    lt = (lax.dot_general(r_hi, h_hi, nt, preferred_element_type=f32)
          + lax.dot_general(r_hi, h_lo, nt, preferred_element_type=f32)
          + lax.dot_general(r_lo, h_hi, nt, preferred_element_type=f32))
    comb, asg = _route(lt, bias_ref)
    tm = lt.shape[1]
    upper = (lax.broadcasted_iota(jnp.int32, (tm, tm), 0) < lax.broadcasted_iota(jnp.int32, (tm, tm), 1))
    rank = jnp.dot(asg.astype(bf16), upper.astype(bf16), preferred_element_type=f32)
    rank = jnp.where(asg > 0.0, rank, -1.0)
    for e in range(N_EXPERTS):
        comb_ref[e] = comb[e:e + 1, :]
        rank_ref[e] = rank[e:e + 1, :]
    cnt = jnp.sum(asg, axis=1, keepdims=True).astype(jnp.int32)
    cnt_ref[...] = jnp.broadcast_to(cnt, cnt_ref.shape)


def _outproj(router_bias, x2, oa, ob, oc, w_bf, layer, gate, shift, scale, g, rwt, *, tm):
    n, d = x2.shape
    row = lambda i: (i, 0)
    const = lambda i: (0, 0)
    return pl.pallas_call(
        _outproj_kernel,
        grid=(n // tm,),
        in_specs=[
            pl.BlockSpec(memory_space=pltpu.SMEM),
            pl.BlockSpec((tm, d), row),
            pl.BlockSpec((tm, WIDTH_A), row),
            pl.BlockSpec((tm, WIDTH_B), row),
            pl.BlockSpec((tm, CONV_CH), row),
            pl.BlockSpec((None, d, d), lambda i: (layer, 0, 0)),
            pl.BlockSpec((1, d), const),
            pl.BlockSpec((1, d), const),
            pl.BlockSpec((1, d), const),
            pl.BlockSpec((1, d), const),
            pl.BlockSpec((N_EXPERTS, d), const),
        ],
        out_specs=(
            pl.BlockSpec((tm, d), row),
            pl.BlockSpec((tm, d), row),
            pl.BlockSpec((N_EXPERTS, 1, tm), lambda i: (0, 0, i)),
            pl.BlockSpec((N_EXPERTS, 1, tm), lambda i: (0, 0, i)),
            pl.BlockSpec((None, N_EXPERTS, LANES), lambda i: (i, 0, 0)),
        ),
        out_shape=(
            jax.ShapeDtypeStruct((n, d), f32),
            jax.ShapeDtypeStruct((n, d), bf16),
            jax.ShapeDtypeStruct((N_EXPERTS, 1, n), f32),
            jax.ShapeDtypeStruct((N_EXPERTS, 1, n), f32),
            jax.ShapeDtypeStruct((n // tm, N_EXPERTS, LANES), jnp.int32),
        ),
        compiler_params=_cparams(("parallel",)),
        name="outproj_router",
    )(router_bias, x2, oa, ob, oc, w_bf, gate, shift, scale, g, rwt)


def _moe_kernel(*refs, final, rows, max_blocks):
    if final:
        (nchunk_ref, blk_expert_ref, blk_slot_ref, x_ref, h_ref, comb_ref, rank_ref, w1_ref, w3_ref, w2_ref,
         gate_ref, gf_ref, o_ref, acc) = refs
    else:
        (nchunk_ref, blk_expert_ref, blk_slot_ref, x_ref, h_ref, comb_ref, rank_ref, w1_ref, w3_ref, w2_ref,
         gate_ref, o_ref, acc) = refs
    step = pl.program_id(1)
    flat_step = pl.program_id(0) * pl.num_programs(1) + step

    @pl.when(step == 0)
    def _():
        acc[...] = jnp.zeros(acc.shape, f32)

    win = comb_ref.shape[2]
    slot0 = lax.broadcasted_iota(jnp.int32, (rows, win), 0).astype(f32)

    def chunk(c, carry):
        onehots, weights, experts = [], [], []
        for i in range(MOE_CHUNK_BLOCKS):
            j = flat_step * max_blocks + c * MOE_CHUNK_BLOCKS + i
            k = blk_expert_ref[j]
            first = blk_slot_ref[j].astype(f32)
            match = rank_ref[k] == slot0 + first
            onehots.append(jnp.where(match, 1.0, 0.0).astype(bf16))
            weights.append(jnp.sum(jnp.where(match, comb_ref[k], 0.0), axis=1, keepdims=True))
            experts.append(k)
        onehot = jnp.concatenate(onehots, axis=0)
        xe = jnp.dot(onehot, h_ref[...], preferred_element_type=f32).astype(bf16)
        outs = []
        for i, k in enumerate(experts):
            xk = xe[i * rows:(i + 1) * rows]
            a = jnp.dot(xk, w1_ref[k], preferred_element_type=f32)
            g = jnp.dot(xk, w3_ref[k], preferred_element_type=f32)
            u = (a * jax.nn.sigmoid(a) * g).astype(bf16)
            y = jnp.dot(u, w2_ref[k], preferred_element_type=f32)
            outs.append((weights[i] * y).astype(bf16))
        yw = jnp.concatenate(outs, axis=0)
        acc[...] += lax.dot_general(onehot, yw, (((0,), (0,)), ((), ())), preferred_element_type=f32)
        return carry

    lax.fori_loop(0, nchunk_ref[flat_step], chunk, 0)

    @pl.when(step == pl.num_programs(1) - 1)
    def _():
        xn = x_ref[...] + gate_ref[...] * acc[...]
        if final:
            xn = xn * lax.rsqrt(jnp.mean(xn * xn, axis=-1, keepdims=True) + EPS) * gf_ref[...]
        o_ref[...] = xn


def _moe_block_tables(counts, rows, eps, max_blocks):
    n_win = counts.shape[0]
    nb = ((counts + (rows - 1)) // rows).reshape(n_win, N_EXPERTS // eps, eps)
    ends = jnp.cumsum(nb, axis=-1)
    starts = ends - nb
    total = ends[..., -1]
    j = jnp.arange(max_blocks, dtype=jnp.int32)
    owner = jnp.sum(j[None, None, :, None] >= ends[:, :, None, :], axis=-1)
    owner = jnp.minimum(owner, eps - 1).astype(jnp.int32)
    start_of_owner = jnp.take_along_axis(starts, owner, axis=-1)
    slot = (j[None, None, :] - start_of_owner) * rows
    unused = j[None, None, :] >= total[..., None]
    slot = jnp.where(unused, jnp.int32(1 << 24), slot).astype(jnp.int32)
    nchunk = ((total + (MOE_CHUNK_BLOCKS - 1)) // MOE_CHUNK_BLOCKS).astype(jnp.int32)
    return nchunk.reshape(-1), owner.reshape(-1), slot.reshape(-1)


def _moe(x2, h2, comb, rank, counts, w1, w3, w2, layer, gate, g_final, *, win):
    n, d = x2.shape
    final = g_final is not None
    rows = MOE_BLOCK_ROWS
    eps = MOE_EXPERTS_PER_STEP
    max_blocks = -(-(2 * win // rows + eps) // MOE_CHUNK_BLOCKS) * MOE_CHUNK_BLOCKS
    nchunk, blk_expert, blk_slot = _moe_block_tables(counts, rows, eps, max_blocks)
    row = lambda i, e, *_: (i, 0)
    const = lambda i, e, *_: (0, 0)
    per_expert = lambda i, e, *_: (e, 0, i)
    weight = lambda i, e, *_: (layer, e, 0, 0)
    in_specs = [
        pl.BlockSpec((win, d), row),
        pl.BlockSpec((win, d), row),
        pl.BlockSpec((eps, 1, win), per_expert),
        pl.BlockSpec((eps, 1, win), per_expert),
        pl.BlockSpec((None, eps, d, EXPERT_FF), weight),
        pl.BlockSpec((None, eps, d, EXPERT_FF), weight),
        pl.BlockSpec((None, eps, EXPERT_FF, d), weight),
        pl.BlockSpec((1, d), const),
    ]
    args = [x2, h2, comb, rank, w1, w3, w2, gate]
    if final:
        in_specs.append(pl.BlockSpec((1, d), const))
        args.append(g_final)
    return pl.pallas_call(
        functools.partial(_moe_kernel, final=final, rows=rows, max_blocks=max_blocks),
        grid_spec=pltpu.PrefetchScalarGridSpec(
            num_scalar_prefetch=3,
            grid=(n // win, N_EXPERTS // eps),
            in_specs=in_specs,
            out_specs=pl.BlockSpec((win, d), row),
            scratch_shapes=[pltpu.VMEM((win, d), f32)],
        ),
        out_shape=jax.ShapeDtypeStruct((n, d), f32),
        compiler_params=_cparams(("parallel", "arbitrary")),
        name="moe_final" if final else "moe",
    )(nchunk, blk_expert, blk_slot, *args)


def _rope_tables(n_tok):
    rows = n_tok // GRID_W
    row = np.repeat(np.arange(rows, dtype=np.float64), GRID_W)
    col = np.tile(np.arange(GRID_W, dtype=np.float64), rows)
    n_freq = HD // 4
    inv = ROPE_BASE ** (-np.arange(n_freq, dtype=np.float64) / n_freq)
    lane = np.arange(LANES)
    j = lane % HD
    use_col = (j // 32) == 1
    freq = inv[j % n_freq]
    ang = np.where(use_col[None, :], col[:, None], row[:, None]) * freq[None, :]
    sign = np.where((lane % 32) < 16, -1.0, 1.0)
    return (jnp.asarray(np.cos(ang).astype(np.float32)),
            jnp.asarray((np.sin(ang) * sign[None, :]).astype(np.float32)))


def kernel(x, c, ctx, c_ctx, w_ada, b_ada, g_mix, w_in, diff_lambda, diff_norm_g, attn_sink, conv_w, conv_b,
           conv_ln_g, conv_ln_b, w_out, g_ffn, router_w, router_bias, w1, w3, w2, g_final):
    bsz, s_len, d = x.shape
    c_len = ctx.shape[1]
    depth = w_ada.shape[0]
    assert bsz == 1 and d == D_MODEL and s_len % 1024 == 0 and c_len % 256 == 0
    tm_lat = 256
    tm_ctx = min(c_len, 256)
    win_lat = MOE_WINDOW
    tq_a, tk_a = 256, 1024

    xl = x.reshape(s_len, d)
    xc = ctx.reshape(c_len, d)
    ct = jnp.zeros((d, LANES), f32).at[:, 0].set(c[0]).at[:, 1].set(c_ctx)
    mod_all = _modulation(ct, w_ada, b_ada)
    cos_t, sin_t = _rope_tables(s_len)
    rwt = router_w.T
    w_in_bf = w_in.astype(bf16)
    w_out_bf = w_out.astype(bf16)
    w1_bf, w3_bf, w2_bf = w1.astype(bf16), w3.astype(bf16), w2.astype(bf16)
    vec = lambda a: a.reshape(1, -1)

    for l in range(depth):
        last = l == depth - 1
        lambda_init = 0.8 - 0.6 * math.exp(-0.3 * l)
        ml = [mod_all[l, 0:1, k * d:(k + 1) * d] for k in range(6)]
        mc = [mod_all[l, 1:2, k * d:(k + 1) * d] for k in range(6)]
        g_mix_l, g_ffn_l = vec(g_mix[l]), vec(g_ffn[l])
        dl, dng = diff_lambda[l], diff_norm_g[l].reshape(-1, 1)

        qa, ka, va, qb, kb, vb, z = _inproj(xl, ml[0], ml[1], g_mix_l, w_in_bf, l, cos_t, sin_t, tm=tm_lat,
                                            tk=tk_a)
        qa_c, ka_c, va_c, qb_c, kb_c, vb_c, z_c = _inproj(xc, mc[0], mc[1], g_mix_l, w_in_bf, l, None, None,
                                                          tm=tm_ctx, tk=c_len)

        o_a = _diff_attention(dl, dng, qa, ka, va, ka_c, va_c, lambda_init=lambda_init, tq=tq_a, tk=tk_a)
        o_b = _win_attention(attn_sink[l], qb, kb, vb, kb_c, vb_c, tq=256)
        o_c = _conv_module(z, conv_w[l], vec(conv_b[l]), vec(conv_ln_g[l]), vec(conv_ln_b[l]), tm=tm_lat)
        xl, h2, comb, rank, cnt = _outproj(router_bias, xl, o_a, o_b, o_c, w_out_bf, l, ml[2], ml[3], ml[4],
                                           g_ffn_l, rwt, tm=win_lat)
        xl = _moe(xl, h2, comb, rank, cnt[:, :, 0], w1_bf, w3_bf, w2_bf, l, ml[5],
                  vec(g_final) if last else None, win=win_lat)

        if not last:
            o_ac = _diff_attention(dl, dng, qa_c, None, None, ka_c, va_c, lambda_init=lambda_init,
                                   tq=tm_ctx, tk=tk_a)
            o_bc = _win_attention(attn_sink[l], qb_c, None, None, kb_c, vb_c, tq=tm_ctx)
            o_cc = _conv_module(z_c, conv_w[l], vec(conv_b[l]), vec(conv_ln_g[l]), vec(conv_ln_b[l]), tm=tm_ctx)
            xc, h2c, comb_c, rank_c, cnt_c = _outproj(router_bias, xc, o_ac, o_bc, o_cc, w_out_bf, l, mc[2], mc[3],
                                                      mc[4], g_ffn_l, rwt, tm=tm_ctx)
            xc = _moe(xc, h2c, comb_c, rank_c, cnt_c[:, :, 0], w1_bf, w3_bf, w2_bf, l, mc[5], None,
                      win=tm_ctx)

    return xl.reshape(bsz, s_len, d)
```

```python
import functools
import math

import numpy as np
import jax
import jax.numpy as jnp
from jax import lax
from jax.experimental import pallas as pl
from jax.experimental.pallas import tpu as pltpu

f32 = jnp.float32
bf16 = jnp.bfloat16

D_MODEL = 1024
GRID_W = 64
HEADS_A = 4
HD = 64
VD_A = 2 * HD
VT_ROWS = VD_A + 16
WIDTH_A = HEADS_A * VD_A
HEADS_B = 4
KV_HEADS_B = 2
WIDTH_B = HEADS_B * HD
WINDOW = 128
CONV_CH = 256
CONV_K = 31
CONV_HALO = 16
IN_WIDTH = 2560
OFF_QA, OFF_KA, OFF_VA, OFF_QB, OFF_KB, OFF_VB, OFF_UC = 0, 512, 1024, 1536, 1792, 1920, 2048
N_EXPERTS = 16
N_GROUPS = 4
EXPERTS_PER_GROUP = 4
EXPERT_FF = 512
ROPE_BASE = 10000.0
EPS = 1e-6
NEG_INF = -1e30
LANES = 128
SUBLANES = 8
QK_SCALE = HD ** -0.5
LOG2_E = math.log2(math.e)
QK_SCALE_LOG2 = QK_SCALE * LOG2_E

MOE_WINDOW = 1024
MOE_EXPERTS_PER_STEP = 4
MOE_BLOCK_ROWS = 128
MOE_CHUNK_BLOCKS = 2
PIPE_UNROLL = 4
VMEM_LIMIT =56 * 1024 * 1024


def _cparams(sem):
    return pltpu.CompilerParams(dimension_semantics=sem, vmem_limit_bytes=VMEM_LIMIT)


def _mod_kernel(ct_ref, w_ref, b_ref, o_ref):
    tn = w_ref.shape[1]

    def body(i, carry):
        a0, a1 = carry
        r = pl.multiple_of(i * 8, 8)
        cv = ct_ref[pl.ds(r, 8), :]
        sv = cv * jax.nn.sigmoid(cv)
        w8 = w_ref[pl.ds(r, 8), :]
        return a0 + w8 * sv[:, 0:1], a1 + w8 * sv[:, 1:2]

    z = jnp.zeros((8, tn), f32)
    a0, a1 = lax.fori_loop(0, w_ref.shape[0] // 8, body, (z, z), unroll=8)
    r0 = jnp.sum(a0, axis=0, keepdims=True) + b_ref[...]
    r1 = jnp.sum(a1, axis=0, keepdims=True) + b_ref[...]
    o_ref[...] = jnp.concatenate([r0, r1, jnp.zeros((6, tn), f32)], axis=0)


def _modulation(ct, w_ada, b_ada):
    depth, d, n = w_ada.shape
    tn = 1536
    return pl.pallas_call(
        _mod_kernel,
        grid=(depth, n // tn),
        in_specs=[
            pl.BlockSpec((d, LANES), lambda l, j: (0, 0)),
            pl.BlockSpec((None, d, tn), lambda l, j: (l, 0, j)),
            pl.BlockSpec((None, 1, tn), lambda l, j: (l, 0, j)),
        ],
        out_specs=pl.BlockSpec((None, 8, tn), lambda l, j: (l, 0, j)),
        out_shape=jax.ShapeDtypeStruct((depth, 8, n), f32),
        compiler_params=_cparams(("arbitrary", "arbitrary")),
        name="modulation",
    )(ct, w_ada, b_ada.reshape(depth, 1, n))


def _rmsnorm_mod(xf, g, shift, scale):
    y = xf * lax.rsqrt(jnp.mean(xf * xf, axis=-1, keepdims=True) + EPS) * g
    return y * (1.0 + scale) + shift


def _inproj_kernel(*refs, rope):
    if rope:
        x_ref, sh_ref, sc_ref, g_ref, w_ref, cos_ref, sin_ref = refs[:7]
        outs = refs[7:]
    else:
        x_ref, sh_ref, sc_ref, g_ref, w_ref = refs[:5]
        outs = refs[5:]
    qa_ref, ka_ref, va_ref, qb_ref, kb_ref, vb_ref, z_ref = outs

    h = _rmsnorm_mod(x_ref[...], g_ref[...], sh_ref[...], sc_ref[...])
    p = jnp.dot(h.astype(bf16), w_ref[...], preferred_element_type=f32)
    tm = p.shape[0]
    lane = lax.broadcasted_iota(jnp.int32, (tm, LANES), 1)
    lo_half = lane < HD

    def rot(xc, scale):
        if rope:
            first = (lane % 32) < 16
            partner = jnp.where(first, pltpu.roll(xc, LANES - 16, 1), pltpu.roll(xc, 16, 1))
            xc = xc * cos_ref[...] + partner * sin_ref[...]
        return xc * scale if scale != 1.0 else xc

    def chunk(off, j):
        return p[:, off + j * LANES: off + (j + 1) * LANES]

    ones = jnp.ones((VT_ROWS - VD_A, tm), f32)
    for hh in range(HEADS_A):
        qa_ref[hh] = rot(chunk(OFF_QA, hh), QK_SCALE_LOG2).T.astype(bf16)
        ka_ref[hh] = rot(chunk(OFF_KA, hh), 1.0).astype(bf16)
        va_ref[hh] = jnp.concatenate([chunk(OFF_VA, hh).T, ones], axis=0).astype(bf16)
    for g in range(KV_HEADS_B):
        qb_ref[g] = rot(chunk(OFF_QB, g), QK_SCALE_LOG2).astype(bf16)
    kb = rot(chunk(OFF_KB, 0), 1.0)
    kb_sw = pltpu.roll(kb, HD, 1)
    kb_ref[0] = jnp.where(lo_half, kb, kb_sw).astype(bf16)
    kb_ref[1] = jnp.where(lo_half, kb_sw, kb).astype(bf16)
    vb = chunk(OFF_VB, 0)
    vb_sw = pltpu.roll(vb, HD, 1)
    zero = jnp.zeros_like(vb)
    vb_ref[0, 0] = jnp.where(lo_half, vb, zero).astype(bf16)
    vb_ref[0, 1] = jnp.where(lo_half, zero, vb_sw).astype(bf16)
    vb_ref[1, 0] = jnp.where(lo_half, vb_sw, zero).astype(bf16)
    vb_ref[1, 1] = jnp.where(lo_half, zero, vb).astype(bf16)
    a = p[:, OFF_UC: OFF_UC + CONV_CH]
    gt = p[:, OFF_UC + CONV_CH: OFF_UC + 2 * CONV_CH]
    z_ref[...] = a * jax.nn.sigmoid(gt)


def _inproj(x2, shift, scale, g, w_bf, layer, cos_t, sin_t, *, tm, tk):
    n, d = x2.shape
    rope = cos_t is not None
    per = tk // tm
    row = lambda i: (i, 0)
    const = lambda i: (0, 0)
    in_specs = [
        pl.BlockSpec((tm, d), row),
        pl.BlockSpec((1, d), const),
        pl.BlockSpec((1, d), const),
        pl.BlockSpec((1, d), const),
        pl.BlockSpec((None, d, IN_WIDTH), lambda i: (layer, 0, 0)),
    ]
    args = [x2, shift, scale, g, w_bf]
    if rope:
        in_specs += [pl.BlockSpec((tm, LANES), row), pl.BlockSpec((tm, LANES), row)]
        args += [cos_t, sin_t]
    out_shape = (
        jax.ShapeDtypeStruct((HEADS_A, LANES, n), bf16),
        jax.ShapeDtypeStruct((HEADS_A, n, LANES), bf16),
        jax.ShapeDtypeStruct((HEADS_A, n // tk, VT_ROWS, tk), bf16),
        jax.ShapeDtypeStruct((KV_HEADS_B, n, LANES), bf16),
        jax.ShapeDtypeStruct((KV_HEADS_B, n, LANES), bf16),
        jax.ShapeDtypeStruct((KV_HEADS_B, 2, n, LANES), bf16),
        jax.ShapeDtypeStruct((n, CONV_CH), f32),
    )
    out_specs = (
        pl.BlockSpec((HEADS_A, LANES, tm), lambda i: (0, 0, i)),
        pl.BlockSpec((HEADS_A, tm, LANES), lambda i: (0, i, 0)),
        pl.BlockSpec((HEADS_A, None, VT_ROWS, tm), lambda i: (0, i // per, 0, i % per)),
        pl.BlockSpec((KV_HEADS_B, tm, LANES), lambda i: (0, i, 0)),
        pl.BlockSpec((KV_HEADS_B, tm, LANES), lambda i: (0, i, 0)),
        pl.BlockSpec((KV_HEADS_B, 2, tm, LANES), lambda i: (0, 0, i, 0)),
        pl.BlockSpec((tm, CONV_CH), row),
    )
    return pl.pallas_call(
        functools.partial(_inproj_kernel, rope=rope),
        grid=(n // tm,),
        in_specs=in_specs,
        out_specs=out_specs,
        out_shape=out_shape,
        compiler_params=_cparams(("parallel",)),
        name="inproj_rope" if rope else "inproj_ctx",
    )(*args)


def _diff_attn_kernel(*refs, tq, tk, n_main, lambda_init):
    if n_main:
        (dl_ref, g_ref, q_ref, qn_ref, kl_ref, vl_ref, kc_ref, vc_ref, o_ref,
         m_scr, acc_scr, s_buf0, s_buf1, s_bufc, mc_buf0, mc_buf1, mc_bufc) = refs
    else:
        dl_ref, g_ref, q_ref, kc_ref, vc_ref, o_ref, m_scr, acc_scr = refs

    def stacked_maps(qt):
        sub = lax.broadcasted_iota(jnp.int32, qt.shape, 0)
        zero = jnp.zeros_like(qt)
        return jnp.concatenate([jnp.where(sub < HD, qt, zero), jnp.where(sub < HD, zero, qt)], axis=1)

    qq = stacked_maps(q_ref[...])
    m_scr[...] = jnp.full(m_scr.shape, NEG_INF, f32)
    acc_scr[...] = jnp.zeros(acc_scr.shape, f32)

    def softmax_pv(s, m_chunk, vt):
        m_old = m_scr[...]
        m_new = jnp.maximum(m_old, m_chunk)
        alpha = jnp.exp2(m_old - m_new)
        p = jnp.exp2(s - m_new).astype(bf16)
        acc_scr[...] = alpha * acc_scr[...] + jnp.dot(vt, p, preferred_element_type=f32)
        m_scr[...] = m_new

    if not n_main:
        s_ctx = jnp.dot(kc_ref[...], qq, preferred_element_type=f32)
        softmax_pv(s_ctx, jnp.max(s_ctx, axis=0, keepdims=True), vc_ref[...])
    else:
        def stage_scores(keys, q_stacked, s_buf, mc_buf):
            s = jnp.dot(keys, q_stacked, preferred_element_type=f32)
            s_buf[...] = s
            mc_buf[...] = jnp.max(s, axis=0, keepdims=True)

        def latent_keys(c):
            return kl_ref[pl.ds(pl.multiple_of(c * tk, tk), tk), :]

        bufs = ((s_buf0, mc_buf0), (s_buf1, mc_buf1))
        unroll = PIPE_UNROLL

        @pl.when(pl.program_id(1) == 0)
        def _():
            stage_scores(latent_keys(0), qq, *bufs[0])

        def body(j, carry):
            c = unroll * j
            for u in range(unroll):
                stage_scores(latent_keys(c + u + 1), qq, *bufs[(u + 1) % 2])
                softmax_pv(bufs[u % 2][0][...], bufs[u % 2][1][...], vl_ref[c + u])
            return carry

        lax.fori_loop(0, n_main // unroll - 1, body, 0)
        for c in range(n_main - unroll, n_main):
            if c + 1 < n_main:
                stage_scores(latent_keys(c + 1), qq, *bufs[(c + 1) % 2])
            else:
                stage_scores(kc_ref[...], qq, s_bufc, mc_bufc)
            softmax_pv(bufs[c % 2][0][...], bufs[c % 2][1][...], vl_ref[c])
        stage_scores(latent_keys(0), stacked_maps(qn_ref[...]), *bufs[0])
        softmax_pv(s_bufc[...], mc_bufc[...], vc_ref[...])

    acc = acc_scr[...]
    o0 = acc[:VD_A, :tq] / acc[VD_A:VD_A + 1, :tq]
    o1 = acc[:VD_A, tq:] / acc[VD_A:VD_A + 1, tq:]
    dl = dl_ref[...]
    lam = (jnp.exp(jnp.sum(dl[0:1] * dl[1:2], axis=1, keepdims=True))
           - jnp.exp(jnp.sum(dl[2:3] * dl[3:4], axis=1, keepdims=True)) + lambda_init)
    o = o0 - lam * o1
    y = o * lax.rsqrt(jnp.mean(o * o, axis=0, keepdims=True) + EPS) * g_ref[...]
    o_ref[...] = (y * (1.0 - lambda_init)).T.astype(o_ref.dtype)


def _diff_attention(dl, g_col, qa_t, ka_lat, va_lat, ka_ctx, va_ctx, *, lambda_init, tq, tk):
    nq = qa_t.shape[2]
    c = ka_ctx.shape[1]
    assert va_ctx.shape[1] == 1
    n_main = 0 if ka_lat is None else ka_lat.shape[1] // tk
    in_specs = [
        pl.BlockSpec((4, HD), lambda h, i: (0, 0)),
        pl.BlockSpec((VD_A, 1), lambda h, i: (0, 0)),
        pl.BlockSpec((None, LANES, tq), lambda h, i: (h, 0, i)),
    ]
    args = [dl, g_col, qa_t]
    if n_main:
        s_keys = ka_lat.shape[1]
        assert va_lat.shape[1:] == (n_main, VT_ROWS, tk)
        last_tile = nq // tq - 1
        in_specs += [pl.BlockSpec((None, LANES, tq), lambda h, i: (h, 0, jnp.minimum(i + 1, last_tile))),
                     pl.BlockSpec((None, s_keys, LANES), lambda h, i: (h, 0, 0)),
                     pl.BlockSpec((None, n_main, VT_ROWS, tk), lambda h, i: (h, 0, 0, 0))]
        args += [qa_t, ka_lat, va_lat]
    in_specs += [pl.BlockSpec((None, c, LANES), lambda h, i: (h, 0, 0)),
                 pl.BlockSpec((None, None, VT_ROWS, c), lambda h, i: (h, 0, 0, 0))]
    args += [ka_ctx, va_ctx]
    scratch = [pltpu.VMEM((1, 2 * tq), f32), pltpu.VMEM((VT_ROWS, 2 * tq), f32)]
    if n_main:
        assert n_main % PIPE_UNROLL == 0
        scratch += [pltpu.VMEM((tk, 2 * tq), f32), pltpu.VMEM((tk, 2 * tq), f32), pltpu.VMEM((c, 2 * tq), f32),
                    pltpu.VMEM((1, 2 * tq), f32), pltpu.VMEM((1, 2 * tq), f32), pltpu.VMEM((1, 2 * tq), f32)]
    return pl.pallas_call(
        functools.partial(_diff_attn_kernel, tq=tq, tk=tk, n_main=n_main, lambda_init=lambda_init),
        grid=(HEADS_A, nq // tq),
        in_specs=in_specs,
        out_specs=pl.BlockSpec((tq, VD_A), lambda h, i: (i, h)),
        out_shape=jax.ShapeDtypeStruct((nq, WIDTH_A), bf16),
        scratch_shapes=scratch,
        compiler_params=_cparams(("arbitrary", "arbitrary")),
        name="diff_attn" if n_main else "diff_attn_ctx",
    )(*args)


def _win_attn_kernel(*refs, tq, band, s_len):
    if band:
        (sink_ref, q_ref, kp_ref, ks_ref, kn_ref, kc_ref,
         vp_ref, vs_ref, vn_ref, vc_ref, o_ref) = refs
    else:
        sink_ref, q_ref, kc_ref, vc_ref, o_ref = refs
    i = pl.program_id(0)
    lane = lax.broadcasted_iota(jnp.int32, (tq, LANES), 1)
    if band:
        nloc = tq + 2 * WINDOW
        qpos = i * tq + lax.broadcasted_iota(jnp.int32, (tq, nloc), 0)
        kpos = i * tq - WINDOW + lax.broadcasted_iota(jnp.int32, (tq, nloc), 1)
        valid = (jnp.abs(qpos - kpos) <= WINDOW) & (kpos >= 0) & (kpos < s_len)
    nt = (((1,), (1,)), ((), ()))
    for g in range(KV_HEADS_B):
        q = q_ref[g]
        kc = kc_ref[g]
        if band:
            kloc = jnp.concatenate([kp_ref[g], ks_ref[g], kn_ref[g]], axis=0)
        o = jnp.zeros((tq, LANES), f32)
        for r in range(2):
            sink = sink_ref[2 * g + r] * LOG2_E
            qm = jnp.where((lane < HD) if r == 0 else (lane >= HD), q, jnp.zeros_like(q))
            s_ctx = lax.dot_general(qm, kc, nt, preferred_element_type=f32)
            m = jnp.maximum(jnp.max(s_ctx, axis=1, keepdims=True), sink)
            if band:
                s_loc = lax.dot_general(qm, kloc, nt, preferred_element_type=f32)
                s_loc = jnp.where(valid, s_loc, NEG_INF)
                m = jnp.maximum(m, jnp.max(s_loc, axis=1, keepdims=True))
            e_ctx = jnp.exp2(s_ctx - m)
            l = jnp.sum(e_ctx, axis=1, keepdims=True) + jnp.exp2(sink - m)
            o_r = jnp.dot(e_ctx.astype(bf16), vc_ref[g, r], preferred_element_type=f32)
            if band:
                e_loc = jnp.exp2(s_loc - m)
                l = l + jnp.sum(e_loc, axis=1, keepdims=True)
                vloc = jnp.concatenate([vp_ref[g, r], vs_ref[g, r], vn_ref[g, r]], axis=0)
                o_r = o_r + jnp.dot(e_loc.astype(bf16), vloc, preferred_element_type=f32)
            o = o + o_r / l
        o_ref[:, g * LANES:(g + 1) * LANES] = o.astype(o_ref.dtype)


def _win_attention(sink, qb, kb_lat, vb_lat, kb_ctx, vb_ctx, *, tq):
    nq = qb.shape[1]
    c = kb_ctx.shape[1]
    band = kb_lat is not None
    smem = pl.BlockSpec(memory_space=pltpu.SMEM)
    q_spec = pl.BlockSpec((KV_HEADS_B, tq, LANES), lambda i: (0, i, 0))
    kc_spec = pl.BlockSpec((KV_HEADS_B, c, LANES), lambda i: (0, 0, 0))
    vc_spec = pl.BlockSpec((KV_HEADS_B, 2, c, LANES), lambda i: (0, 0, 0, 0))
    if band:
        per = tq // WINDOW
        last = nq // WINDOW - 1
        prev = lambda i: jnp.maximum(i * per - 1, 0)
        nxt = lambda i: jnp.minimum((i + 1) * per, last)
        in_specs = [
            smem, q_spec,
            pl.BlockSpec((KV_HEADS_B, WINDOW, LANES), lambda i: (0, prev(i), 0)),
            pl.BlockSpec((KV_HEADS_B, tq, LANES), lambda i: (0, i, 0)),
            pl.BlockSpec((KV_HEADS_B, WINDOW, LANES), lambda i: (0, nxt(i), 0)),
            kc_spec,
            pl.BlockSpec((KV_HEADS_B, 2, WINDOW, LANES), lambda i: (0, 0, prev(i), 0)),
            pl.BlockSpec((KV_HEADS_B, 2, tq, LANES), lambda i: (0, 0, i, 0)),
            pl.BlockSpec((KV_HEADS_B, 2, WINDOW, LANES), lambda i: (0, 0, nxt(i), 0)),
            vc_spec,
        ]
        args = [sink, qb, kb_lat, kb_lat, kb_lat, kb_ctx, vb_lat, vb_lat, vb_lat, vb_ctx]
    else:
        in_specs = [smem, q_spec, kc_spec, vc_spec]
        args = [sink, qb, kb_ctx, vb_ctx]
    return pl.pallas_call(
        functools.partial(_win_attn_kernel, tq=tq, band=band, s_len=nq),
        grid=(nq // tq,),
        in_specs=in_specs,
        out_specs=pl.BlockSpec((tq, WIDTH_B), lambda i: (i, 0)),
        out_shape=jax.ShapeDtypeStruct((nq, WIDTH_B), bf16),
        compiler_params=_cparams(("parallel",)),
        name="win_attn" if band else "win_attn_ctx",
    )(*args)


def _conv_kernel(zp_ref, zs_ref, zn_ref, w_ref, b_ref, lg_ref, lb_ref, o_ref, zbuf, zsh, *, tm, rc):
    i = pl.program_id(0)
    n = pl.num_programs(0)
    halo = CONV_HALO
    zero = jnp.zeros((halo, CONV_CH), f32)
    zbuf[0:halo, :] = jnp.where(i > 0, zp_ref[...], zero)
    zbuf[halo:halo + tm, :] = zs_ref[...]
    zbuf[halo + tm:2 * halo + tm, :] = jnp.where(i < n - 1, zn_ref[...], zero)
    w = w_ref[...]
    base = halo - CONV_K // 2
    span = zsh.shape[1]
    for b in range(SUBLANES):
        zsh[b] = zbuf[b:b + span, :]

    for r0 in range(0, tm, rc):
        acc = jnp.zeros((rc, CONV_CH), f32)
        for k in range(CONV_K):
            off = base + k
            start = r0 + SUBLANES * (off // SUBLANES)
            acc = acc + zsh[off % SUBLANES, start:start + rc, :] * w[k:k + 1, :]
        zc = acc + b_ref[...]
        mu = jnp.mean(zc, axis=-1, keepdims=True)
        dz = zc - mu
        var = jnp.mean(dz * dz, axis=-1, keepdims=True)
        zn = dz * lax.rsqrt(var + EPS) * lg_ref[...] + lb_ref[...]
        o_ref[r0:r0 + rc, :] = (zn * jax.nn.sigmoid(zn)).astype(o_ref.dtype)


def _conv_module(z, w, b, ln_g, ln_b, *, tm):
    n = z.shape[0]
    per = tm // CONV_HALO
    last = n // CONV_HALO - 1
    vec = lambda i: (0, 0)
    return pl.pallas_call(
        functools.partial(_conv_kernel, tm=tm, rc=32),
        grid=(n // tm,),
        in_specs=[
            pl.BlockSpec((CONV_HALO, CONV_CH), lambda i: (jnp.maximum(i * per - 1, 0), 0)),
            pl.BlockSpec((tm, CONV_CH), lambda i: (i, 0)),
            pl.BlockSpec((CONV_HALO, CONV_CH), lambda i: (jnp.minimum((i + 1) * per, last), 0)),
            pl.BlockSpec((CONV_K, CONV_CH), vec),
            pl.BlockSpec((1, CONV_CH), vec),
            pl.BlockSpec((1, CONV_CH), vec),
            pl.BlockSpec((1, CONV_CH), vec),
        ],
        out_specs=pl.BlockSpec((tm, CONV_CH), lambda i: (i, 0)),
        out_shape=jax.ShapeDtypeStruct((n, CONV_CH), bf16),
        scratch_shapes=[pltpu.VMEM((tm + 2 * CONV_HALO, CONV_CH), f32),
                        pltpu.VMEM((SUBLANES, tm + 2 * CONV_HALO - SUBLANES, CONV_CH), f32)],
        compiler_params=_cparams(("parallel",)),
        name="conv_module",
    )(z, z, z, w, b, ln_g, ln_b)


def _split_bf16(a):
    hi = a.astype(bf16)
    lo = (a - hi.astype(f32)).astype(bf16)
    return hi, lo


def _route(lt, bias_ref):
    s = [jax.nn.sigmoid(lt[e:e + 1, :]) for e in range(N_EXPERTS)]
    b = [s[e] + bias_ref[e] for e in range(N_EXPERTS)]
    grp = []
    for g in range(N_GROUPS):
        v = b[4 * g:4 * g + 4]
        best = None
        for a_i in range(4):
            for b_i in range(a_i + 1, 4):
                t = v[a_i] + v[b_i]
                best = t if best is None else jnp.maximum(best, t)
        grp.append(best)
    sel = jnp.zeros_like(lt[0:1, :], dtype=jnp.int32)
    gbest = grp[0]
    for g in range(1, N_GROUPS):
        better = grp[g] > gbest
        sel = jnp.where(better, g, sel)
        gbest = jnp.where(better, grp[g], gbest)

    def pick(rows, j):
        out = rows[j]
        for g in range(1, N_GROUPS):
            out = jnp.where(sel == g, rows[4 * g + j], out)
        return out

    vb = [pick(b, j) for j in range(4)]
    vs = [pick(s, j) for j in range(4)]
    i1 = jnp.zeros_like(sel)
    b1 = vb[0]
    for j in range(1, 4):
        better = vb[j] > b1
        i1 = jnp.where(better, j, i1)
        b1 = jnp.where(better, vb[j], b1)
    i2 = jnp.full_like(sel, -1)
    b2 = jnp.full_like(b1, -jnp.inf)
    for j in range(4):
        better = (i1 != j) & ((i2 < 0) | (vb[j] > b2))
        i2 = jnp.where(better, j, i2)
        b2 = jnp.where(better, vb[j], b2)
    s1 = vs[0]
    s2 = vs[0]
    for j in range(1, 4):
        s1 = jnp.where(i1 == j, vs[j], s1)
        s2 = jnp.where(i2 == j, vs[j], s2)
    tot = s1 + s2
    w1 = s1 / tot
    w2 = s2 / tot
    e1 = sel * EXPERTS_PER_GROUP + i1
    e2 = sel * EXPERTS_PER_GROUP + i2
    rows, asg = [], []
    zero = jnp.zeros_like(w1)
    one = jnp.ones_like(w1)
    for e in range(N_EXPERTS):
        rows.append(jnp.where(e1 == e, w1, zero) + jnp.where(e2 == e, w2, zero))
        asg.append(jnp.where((e1 == e) | (e2 == e), one, zero))
    return jnp.concatenate(rows, axis=0), jnp.concatenate(asg, axis=0)


def _outproj_kernel(bias_ref, x_ref, oa_ref, ob_ref, oc_ref, w_ref, gate_ref, sh_ref, sc_ref, g_ref,
                    rwt_ref, xo_ref, h2_ref, comb_ref, rank_ref, cnt_ref):
    y = jnp.dot(oa_ref[...], w_ref[0:WIDTH_A, :], preferred_element_type=f32)
    y = y + jnp.dot(ob_ref[...], w_ref[WIDTH_A:WIDTH_A + WIDTH_B, :], preferred_element_type=f32)
    y = y + jnp.dot(oc_ref[...], w_ref[WIDTH_A + WIDTH_B:, :], preferred_element_type=f32)
    xn = x_ref[...] + gate_ref[...] * y
    xo_ref[...] = xn
    h2 = _rmsnorm_mod(xn, g_ref[...], sh_ref[...], sc_ref[...])
    h2_ref[...] = h2.astype(bf16)
    nt = (((1,), (1,)), ((), ()))
    h_hi, h_lo = _split_bf16(h2)
    r_hi, r_lo = _split_bf16(rwt_ref[...])
    lt = (lax.dot_general(r_hi, h_hi, nt, preferred_element_type=f32)
          + lax.dot_general(r_hi, h_lo, nt, preferred_element_type=f32)
          + lax.dot_general(r_lo, h_hi, nt, preferred_element_type=f32))
    comb, asg = _route(lt, bias_ref)
    tm = lt.shape[1]
    upper = (lax.broadcasted_iota(jnp.int32, (tm, tm), 0) < lax.broadcasted_iota(jnp.int32, (tm, tm), 1))
    rank = jnp.dot(asg.astype(bf16), upper.astype(bf16), preferred_element_type=f32)
    rank = jnp.where(asg > 0.0, rank, -1.0)
    for e in range(N_EXPERTS):
        comb_ref[e] = comb[e:e + 1, :]
        rank_ref[e] = rank[e:e + 1, :]
    cnt = jnp.sum(asg, axis=1, keepdims=True).astype(jnp.int32)
    cnt_ref[...] = jnp.broadcast_to(cnt, cnt_ref.shape)


def _outproj(router_bias, x2, oa, ob, oc, w_bf, layer, gate, shift, scale, g, rwt, *, tm):
    n, d = x2.shape
    row = lambda i: (i, 0)
    const = lambda i: (0, 0)
    return pl.pallas_call(
        _outproj_kernel,
        grid=(n // tm,),
        in_specs=[
            pl.BlockSpec(memory_space=pltpu.SMEM),
            pl.BlockSpec((tm, d), row),
            pl.BlockSpec((tm, WIDTH_A), row),
            pl.BlockSpec((tm, WIDTH_B), row),
            pl.BlockSpec((tm, CONV_CH), row),
            pl.BlockSpec((None, d, d), lambda i: (layer, 0, 0)),
            pl.BlockSpec((1, d), const),
            pl.BlockSpec((1, d), const),
            pl.BlockSpec((1, d), const),
            pl.BlockSpec((1, d), const),
            pl.BlockSpec((N_EXPERTS, d), const),
        ],
        out_specs=(
            pl.BlockSpec((tm, d), row),
            pl.BlockSpec((tm, d), row),
            pl.BlockSpec((N_EXPERTS, 1, tm), lambda i: (0, 0, i)),
            pl.BlockSpec((N_EXPERTS, 1, tm), lambda i: (0, 0, i)),
            pl.BlockSpec((None, N_EXPERTS, LANES), lambda i: (i, 0, 0)),
        ),
        out_shape=(
            jax.ShapeDtypeStruct((n, d), f32),
            jax.ShapeDtypeStruct((n, d), bf16),
            jax.ShapeDtypeStruct((N_EXPERTS, 1, n), f32),
            jax.ShapeDtypeStruct((N_EXPERTS, 1, n), f32),
            jax.ShapeDtypeStruct((n // tm, N_EXPERTS, LANES), jnp.int32),
        ),
        compiler_params=_cparams(("parallel",)),
        name="outproj_router",
    )(router_bias, x2, oa, ob, oc, w_bf, gate, shift, scale, g, rwt)


def _moe_kernel(*refs, final, rows, max_blocks):
    if final:
        (nchunk_ref, blk_expert_ref, blk_slot_ref, x_ref, h_ref, comb_ref, rank_ref, w1_ref, w3_ref, w2_ref,
         gate_ref, gf_ref, o_ref, acc) = refs
    else:
        (nchunk_ref, blk_expert_ref, blk_slot_ref, x_ref, h_ref, comb_ref, rank_ref, w1_ref, w3_ref, w2_ref,
         gate_ref, o_ref, acc) = refs
    step = pl.program_id(1)
    flat_step = pl.program_id(0) * pl.num_programs(1) + step

    @pl.when(step == 0)
    def _():
        acc[...] = jnp.zeros(acc.shape, f32)

    win = comb_ref.shape[2]
    slot0 = lax.broadcasted_iota(jnp.int32, (rows, win), 0).astype(f32)

    def chunk(c, carry):
        onehots, weights, experts = [], [], []
        for i in range(MOE_CHUNK_BLOCKS):
            j = flat_step * max_blocks + c * MOE_CHUNK_BLOCKS + i
            k = blk_expert_ref[j]
            first = blk_slot_ref[j].astype(f32)
            match = rank_ref[k] == slot0 + first
            onehots.append(jnp.where(match, 1.0, 0.0).astype(bf16))
            weights.append(jnp.sum(jnp.where(match, comb_ref[k], 0.0), axis=1, keepdims=True))
            experts.append(k)
        onehot = jnp.concatenate(onehots, axis=0)
        xe = jnp.dot(onehot, h_ref[...], preferred_element_type=f32).astype(bf16)
        outs = []
        for i, k in enumerate(experts):
            xk = xe[i * rows:(i + 1) * rows]
            a = jnp.dot(xk, w1_ref[k], preferred_element_type=f32)
            g = jnp.dot(xk, w3_ref[k], preferred_element_type=f32)
            u = (a * jax.nn.sigmoid(a) * g).astype(bf16)
            y = jnp.dot(u, w2_ref[k], preferred_element_type=f32)
            outs.append((weights[i] * y).astype(bf16))
        yw = jnp.concatenate(outs, axis=0)
        acc[...] += lax.dot_general(onehot, yw, (((0,), (0,)), ((), ())), preferred_element_type=f32)
        return carry

    lax.fori_loop(0, nchunk_ref[flat_step], chunk, 0)

    @pl.when(step == pl.num_programs(1) - 1)
    def _():
        xn = x_ref[...] + gate_ref[...] * acc[...]
        if final:
            xn = xn * lax.rsqrt(jnp.mean(xn * xn, axis=-1, keepdims=True) + EPS) * gf_ref[...]
        o_ref[...] = xn


def _moe_block_tables(counts, rows, eps, max_blocks):
    n_win = counts.shape[0]
    nb = ((counts + (rows - 1)) // rows).reshape(n_win, N_EXPERTS // eps, eps)
    ends = jnp.cumsum(nb, axis=-1)
    starts = ends - nb
    total = ends[..., -1]
    j = jnp.arange(max_blocks, dtype=jnp.int32)
    owner = jnp.sum(j[None, None, :, None] >= ends[:, :, None, :], axis=-1)
    owner = jnp.minimum(owner, eps - 1).astype(jnp.int32)
    picks = owner[..., None] == jnp.arange(eps, dtype=jnp.int32)
    start_of_owner = jnp.sum(jnp.where(picks, starts[:, :, None, :], 0), axis=-1)
    slot = (j[None, None, :] - start_of_owner) * rows
    unused = j[None, None, :] >= total[..., None]
    slot = jnp.where(unused, jnp.int32(1 << 24), slot).astype(jnp.int32)
    nchunk = ((total + (MOE_CHUNK_BLOCKS - 1)) // MOE_CHUNK_BLOCKS).astype(jnp.int32)
    return nchunk.reshape(-1), owner.reshape(-1), slot.reshape(-1)


def _moe(x2, h2, comb, rank, counts, w1, w3, w2, layer, gate, g_final, *, win):
    n, d = x2.shape
    final = g_final is not None
    rows = MOE_BLOCK_ROWS
    eps = MOE_EXPERTS_PER_STEP
    max_blocks = -(-(2 * win // rows + eps) // MOE_CHUNK_BLOCKS) * MOE_CHUNK_BLOCKS
    nchunk, blk_expert, blk_slot = _moe_block_tables(counts, rows, eps, max_blocks)
    row = lambda i, e, *_: (i, 0)
    const = lambda i, e, *_: (0, 0)
    per_expert = lambda i, e, *_: (e, 0, i)
    weight = lambda i, e, *_: (layer, e, 0, 0)
    in_specs = [
        pl.BlockSpec((win, d), row),
        pl.BlockSpec((win, d), row),
        pl.BlockSpec((eps, 1, win), per_expert),
        pl.BlockSpec((eps, 1, win), per_expert),
        pl.BlockSpec((None, eps, d, EXPERT_FF), weight),
        pl.BlockSpec((None, eps, d, EXPERT_FF), weight),
        pl.BlockSpec((None, eps, EXPERT_FF, d), weight),
        pl.BlockSpec((1, d), const),
    ]
    args = [x2, h2, comb, rank, w1, w3, w2, gate]
    if final:
        in_specs.append(pl.BlockSpec((1, d), const))
        args.append(g_final)
    return pl.pallas_call(
        functools.partial(_moe_kernel, final=final, rows=rows, max_blocks=max_blocks),
        grid_spec=pltpu.PrefetchScalarGridSpec(
            num_scalar_prefetch=3,
            grid=(n // win, N_EXPERTS // eps),
            in_specs=in_specs,
            out_specs=pl.BlockSpec((win, d), row),
            scratch_shapes=[pltpu.VMEM((win, d), f32)],
        ),
        out_shape=jax.ShapeDtypeStruct((n, d), f32),
        compiler_params=_cparams(("parallel", "arbitrary")),
        name="moe_final" if final else "moe",
    )(nchunk, blk_expert, blk_slot, *args)


def _rope_tables(n_tok):
    rows = n_tok // GRID_W
    row = np.repeat(np.arange(rows, dtype=np.float64), GRID_W)
    col = np.tile(np.arange(GRID_W, dtype=np.float64), rows)
    n_freq = HD // 4
    inv = ROPE_BASE ** (-np.arange(n_freq, dtype=np.float64) / n_freq)
    lane = np.arange(LANES)
    j = lane % HD
    use_col = (j // 32) == 1
    freq = inv[j % n_freq]
    ang = np.where(use_col[None, :], col[:, None], row[:, None]) * freq[None, :]
    sign = np.where((lane % 32) < 16, -1.0, 1.0)
    return (jnp.asarray(np.cos(ang).astype(np.float32)),
            jnp.asarray((np.sin(ang) * sign[None, :]).astype(np.float32)))


def kernel(x, c, ctx, c_ctx, w_ada, b_ada, g_mix, w_in, diff_lambda, diff_norm_g, attn_sink, conv_w, conv_b,
           conv_ln_g, conv_ln_b, w_out, g_ffn, router_w, router_bias, w1, w3, w2, g_final):
    bsz, s_len, d = x.shape
    c_len = ctx.shape[1]
    depth = w_ada.shape[0]
    assert bsz == 1 and d == D_MODEL and s_len % 1024 == 0 and c_len % 256 == 0
    tm_lat = 256
    tm_ctx = min(c_len, 256)
    win_lat = MOE_WINDOW
    tq_a, tk_a = 256, 1024

    xl = x.reshape(s_len, d)
    xc = ctx.reshape(c_len, d)
    ct = jnp.zeros((d, LANES), f32).at[:, 0].set(c[0]).at[:, 1].set(c_ctx)
    mod_all = _modulation(ct, w_ada, b_ada)
    cos_t, sin_t = _rope_tables(s_len)
    rwt = router_w.T
    w_in_bf = w_in.astype(bf16)
    w_out_bf = w_out.astype(bf16)
    w1_bf, w3_bf, w2_bf = w1.astype(bf16), w3.astype(bf16), w2.astype(bf16)
    vec = lambda a: a.reshape(1, -1)

    for l in range(depth):
        last = l == depth - 1
        lambda_init = 0.8 - 0.6 * math.exp(-0.3 * l)
        ml = [mod_all[l, 0:1, k * d:(k + 1) * d] for k in range(6)]
        mc = [mod_all[l, 1:2, k * d:(k + 1) * d] for k in range(6)]
        g_mix_l, g_ffn_l = vec(g_mix[l]), vec(g_ffn[l])
        dl, dng = diff_lambda[l], diff_norm_g[l].reshape(-1, 1)

        qa, ka, va, qb, kb, vb, z = _inproj(xl, ml[0], ml[1], g_mix_l, w_in_bf, l, cos_t, sin_t, tm=2 * tm_lat,
                                            tk=tk_a)
        qa_c, ka_c, va_c, qb_c, kb_c, vb_c, z_c = _inproj(xc, mc[0], mc[1], g_mix_l, w_in_bf, l, None, None,
                                                          tm=tm_ctx, tk=c_len)

        o_a = _diff_attention(dl, dng, qa, ka, va, ka_c, va_c, lambda_init=lambda_init, tq=tq_a, tk=tk_a)
        o_b = _win_attention(attn_sink[l], qb, kb, vb, kb_c, vb_c, tq=256)
        o_c = _conv_module(z, conv_w[l], vec(conv_b[l]), vec(conv_ln_g[l]), vec(conv_ln_b[l]), tm=4 * tm_lat)
        xl, h2, comb, rank, cnt = _outproj(router_bias, xl, o_a, o_b, o_c, w_out_bf, l, ml[2], ml[3], ml[4],
                                           g_ffn_l, rwt, tm=win_lat)
        xl = _moe(xl, h2, comb, rank, cnt[:, :, 0], w1_bf, w3_bf, w2_bf, l, ml[5],
                  vec(g_final) if last else None, win=win_lat)

        if not last:
            o_ac = _diff_attention(dl, dng, qa_c, None, None, ka_c, va_c, lambda_init=lambda_init,
                                   tq=tm_ctx, tk=tk_a)
            o_bc = _win_attention(attn_sink[l], qb_c, None, None, kb_c, vb_c, tq=tm_ctx)
            o_cc = _conv_module(z_c, conv_w[l], vec(conv_b[l]), vec(conv_ln_g[l]), vec(conv_ln_b[l]), tm=tm_ctx)
            xc, h2c, comb_c, rank_c, cnt_c = _outproj(router_bias, xc, o_ac, o_bc, o_cc, w_out_bf, l, mc[2], mc[3],
                                                      mc[4], g_ffn_l, rwt, tm=tm_ctx)
            xc = _moe(xc, h2c, comb_c, rank_c, cnt_c[:, :, 0], w1_bf, w3_bf, w2_bf, l, mc[5], None,
                      win=tm_ctx)

    return xl.reshape(bsz, s_len, d)
```

```python
import functools
import math

import numpy as np
import jax
import jax.numpy as jnp
from jax import lax
from jax.experimental import pallas as pl
from jax.experimental.pallas import tpu as pltpu

f32 = jnp.float32
bf16 = jnp.bfloat16

D_MODEL = 1024
GRID_W = 64
HEADS_A = 4
HD = 64
VD_A = 2 * HD
BF16_SUBLANES = 16
VT_ROWS = VD_A + BF16_SUBLANES
WIDTH_A = HEADS_A * VD_A
HEADS_B = 4
KV_HEADS_B = 2
WIDTH_B = HEADS_B * HD
WINDOW = 128
CONV_CH = 256
CONV_K = 31
CONV_HALO = 16
IN_WIDTH = 2560
OFF_QA, OFF_KA, OFF_VA, OFF_QB, OFF_KB, OFF_VB, OFF_UC = 0, 512, 1024, 1536, 1792, 1920, 2048
N_EXPERTS = 16
N_GROUPS = 4
EXPERTS_PER_GROUP = 4
EXPERT_FF = 512
ROPE_BASE = 10000.0
EPS = 1e-6
NEG_INF = -1e30
LANES = 128
SUBLANES = 8
QK_SCALE = HD ** -0.5
LOG2_E = math.log2(math.e)
QK_SCALE_LOG2 = QK_SCALE * LOG2_E

MOE_WINDOW = 1024
MOE_EXPERTS_PER_STEP = 4
MOE_BLOCK_ROWS = 128
MOE_CHUNK_BLOCKS = 2
MOE_UNUSED_SLOT = 1 << 24
PIPE_UNROLL = 4
MOD_COLS_PER_STEP = 1536
VMEM_LIMIT = 56 * 1024 * 1024


def _cparams(sem):
    return pltpu.CompilerParams(dimension_semantics=sem, vmem_limit_bytes=VMEM_LIMIT)


def _mod_kernel(ct_ref, w_ref, b_ref, o_ref):
    tn = w_ref.shape[1]

    def body(i, carry):
        a0, a1 = carry
        r = pl.multiple_of(i * 8, 8)
        cv = ct_ref[pl.ds(r, 8), :]
        sv = cv * jax.nn.sigmoid(cv)
        w8 = w_ref[pl.ds(r, 8), :]
        return a0 + w8 * sv[:, 0:1], a1 + w8 * sv[:, 1:2]

    z = jnp.zeros((8, tn), f32)
    a0, a1 = lax.fori_loop(0, w_ref.shape[0] // 8, body, (z, z), unroll=8)
    r0 = jnp.sum(a0, axis=0, keepdims=True) + b_ref[...]
    r1 = jnp.sum(a1, axis=0, keepdims=True) + b_ref[...]
    o_ref[...] = jnp.concatenate([r0, r1, jnp.zeros((6, tn), f32)], axis=0)


def _modulation(ct, w_ada, b_ada):
    depth, d, n = w_ada.shape
    tn = MOD_COLS_PER_STEP
    return pl.pallas_call(
        _mod_kernel,
        grid=(depth, n // tn),
        in_specs=[
            pl.BlockSpec((d, LANES), lambda l, j: (0, 0)),
            pl.BlockSpec((None, d, tn), lambda l, j: (l, 0, j)),
            pl.BlockSpec((None, 1, tn), lambda l, j: (l, 0, j)),
        ],
        out_specs=pl.BlockSpec((None, 8, tn), lambda l, j: (l, 0, j)),
        out_shape=jax.ShapeDtypeStruct((depth, 8, n), f32),
        compiler_params=_cparams(("arbitrary", "arbitrary")),
        name="modulation",
    )(ct, w_ada, b_ada.reshape(depth, 1, n))


def _rmsnorm_mod(xf, g, shift, scale):
    y = xf * lax.rsqrt(jnp.mean(xf * xf, axis=-1, keepdims=True) + EPS) * g
    return y * (1.0 + scale) + shift


def _inproj_kernel(*refs, rope):
    if rope:
        x_ref, sh_ref, sc_ref, g_ref, w_ref, cos_ref, sin_ref = refs[:7]
        outs = refs[7:]
    else:
        x_ref, sh_ref, sc_ref, g_ref, w_ref = refs[:5]
        outs = refs[5:]
    qa_ref, ka_ref, va_ref, qb_ref, kb_ref, vb_ref, z_ref = outs

    h = _rmsnorm_mod(x_ref[...], g_ref[...], sh_ref[...], sc_ref[...])
    p = jnp.dot(h.astype(bf16), w_ref[...], preferred_element_type=f32)
    tm = p.shape[0]
    lane = lax.broadcasted_iota(jnp.int32, (tm, LANES), 1)
    lo_half = lane < HD

    def rot(xc, scale):
        if rope:
            first = (lane % 32) < 16
            partner = jnp.where(first, pltpu.roll(xc, LANES - 16, 1), pltpu.roll(xc, 16, 1))
            xc = xc * cos_ref[...] + partner * sin_ref[...]
        return xc * scale if scale != 1.0 else xc

    def chunk(off, j):
        return p[:, off + j * LANES: off + (j + 1) * LANES]

    ones = jnp.ones((VT_ROWS - VD_A, tm), f32)
    for hh in range(HEADS_A):
        qa_ref[hh] = rot(chunk(OFF_QA, hh), QK_SCALE_LOG2).T.astype(bf16)
        ka_ref[hh] = rot(chunk(OFF_KA, hh), 1.0).astype(bf16)
        va_ref[hh] = jnp.concatenate([chunk(OFF_VA, hh).T, ones], axis=0).astype(bf16)
    for g in range(KV_HEADS_B):
        qb_ref[g] = rot(chunk(OFF_QB, g), QK_SCALE_LOG2).astype(bf16)
    kb = rot(chunk(OFF_KB, 0), 1.0)
    kb_sw = pltpu.roll(kb, HD, 1)
    kb_ref[0] = jnp.where(lo_half, kb, kb_sw).astype(bf16)
    kb_ref[1] = jnp.where(lo_half, kb_sw, kb).astype(bf16)
    vb = chunk(OFF_VB, 0)
    vb_sw = pltpu.roll(vb, HD, 1)
    zero = jnp.zeros_like(vb)
    vb_ref[0, 0] = jnp.where(lo_half, vb, zero).astype(bf16)
    vb_ref[0, 1] = jnp.where(lo_half, zero, vb_sw).astype(bf16)
    vb_ref[1, 0] = jnp.where(lo_half, vb_sw, zero).astype(bf16)
    vb_ref[1, 1] = jnp.where(lo_half, zero, vb).astype(bf16)
    a = p[:, OFF_UC: OFF_UC + CONV_CH]
    gt = p[:, OFF_UC + CONV_CH: OFF_UC + 2 * CONV_CH]
    z_ref[...] = a * jax.nn.sigmoid(gt)


def _inproj(x2, shift, scale, g, w_bf, layer, cos_t, sin_t, *, tm, tk):
    n, d = x2.shape
    rope = cos_t is not None
    per = tk // tm
    row = lambda i: (i, 0)
    const = lambda i: (0, 0)
    in_specs = [
        pl.BlockSpec((tm, d), row),
        pl.BlockSpec((1, d), const),
        pl.BlockSpec((1, d), const),
        pl.BlockSpec((1, d), const),
        pl.BlockSpec((None, d, IN_WIDTH), lambda i: (layer, 0, 0)),
    ]
    args = [x2, shift, scale, g, w_bf]
    if rope:
        in_specs += [pl.BlockSpec((tm, LANES), row), pl.BlockSpec((tm, LANES), row)]
        args += [cos_t, sin_t]
    out_shape = (
        jax.ShapeDtypeStruct((HEADS_A, LANES, n), bf16),
        jax.ShapeDtypeStruct((HEADS_A, n, LANES), bf16),
        jax.ShapeDtypeStruct((HEADS_A, n // tk, VT_ROWS, tk), bf16),
        jax.ShapeDtypeStruct((KV_HEADS_B, n, LANES), bf16),
        jax.ShapeDtypeStruct((KV_HEADS_B, n, LANES), bf16),
        jax.ShapeDtypeStruct((KV_HEADS_B, 2, n, LANES), bf16),
        jax.ShapeDtypeStruct((n, CONV_CH), f32),
    )
    out_specs = (
        pl.BlockSpec((HEADS_A, LANES, tm), lambda i: (0, 0, i)),
        pl.BlockSpec((HEADS_A, tm, LANES), lambda i: (0, i, 0)),
        pl.BlockSpec((HEADS_A, None, VT_ROWS, tm), lambda i: (0, i // per, 0, i % per)),
        pl.BlockSpec((KV_HEADS_B, tm, LANES), lambda i: (0, i, 0)),
        pl.BlockSpec((KV_HEADS_B, tm, LANES), lambda i: (0, i, 0)),
        pl.BlockSpec((KV_HEADS_B, 2, tm, LANES), lambda i: (0, 0, i, 0)),
        pl.BlockSpec((tm, CONV_CH), row),
    )
    return pl.pallas_call(
        functools.partial(_inproj_kernel, rope=rope),
        grid=(n // tm,),
        in_specs=in_specs,
        out_specs=out_specs,
        out_shape=out_shape,
        compiler_params=_cparams(("parallel",)),
        name="inproj_rope" if rope else "inproj_ctx",
    )(*args)


def _diff_attn_kernel(*refs, tq, tk, n_main, lambda_init):
    if n_main:
        (dl_ref, g_ref, q_ref, qn_ref, kl_ref, vl_ref, kc_ref, vc_ref, o_ref,
         m_scr, acc_scr, s_buf0, s_buf1, s_bufc, mc_buf0, mc_buf1, mc_bufc) = refs
    else:
        dl_ref, g_ref, q_ref, kc_ref, vc_ref, o_ref, m_scr, acc_scr = refs

    def stacked_maps(qt):
        sub = lax.broadcasted_iota(jnp.int32, qt.shape, 0)
        zero = jnp.zeros_like(qt)
        return jnp.concatenate([jnp.where(sub < HD, qt, zero), jnp.where(sub < HD, zero, qt)], axis=1)

    qq = stacked_maps(q_ref[...])
    m_scr[...] = jnp.full(m_scr.shape, NEG_INF, f32)
    acc_scr[...] = jnp.zeros(acc_scr.shape, f32)

    def softmax_pv(s, m_chunk, vt):
        m_old = m_scr[...]
        m_new = jnp.maximum(m_old, m_chunk)
        alpha = jnp.exp2(m_old - m_new)
        p = jnp.exp2(s - m_new).astype(bf16)
        acc_scr[...] = alpha * acc_scr[...] + jnp.dot(vt, p, preferred_element_type=f32)
        m_scr[...] = m_new

    if not n_main:
        s_ctx = jnp.dot(kc_ref[...], qq, preferred_element_type=f32)
        softmax_pv(s_ctx, jnp.max(s_ctx, axis=0, keepdims=True), vc_ref[...])
    else:
        def stage_scores(keys, q_stacked, s_buf, mc_buf):
            s = jnp.dot(keys, q_stacked, preferred_element_type=f32)
            s_buf[...] = s
            mc_buf[...] = jnp.max(s, axis=0, keepdims=True)

        def latent_keys(c):
            return kl_ref[pl.ds(pl.multiple_of(c * tk, tk), tk), :]

        bufs = ((s_buf0, mc_buf0), (s_buf1, mc_buf1))
        unroll = PIPE_UNROLL

        @pl.when(pl.program_id(1) == 0)
        def _():
            stage_scores(latent_keys(0), qq, *bufs[0])

        def body(j, carry):
            c = unroll * j
            for u in range(unroll):
                stage_scores(latent_keys(c + u + 1), qq, *bufs[(u + 1) % 2])
                softmax_pv(bufs[u % 2][0][...], bufs[u % 2][1][...], vl_ref[c + u])
            return carry

        lax.fori_loop(0, n_main // unroll - 1, body, 0)
        for c in range(n_main - unroll, n_main):
            if c + 1 < n_main:
                stage_scores(latent_keys(c + 1), qq, *bufs[(c + 1) % 2])
            else:
                stage_scores(kc_ref[...], qq, s_bufc, mc_bufc)
            softmax_pv(bufs[c % 2][0][...], bufs[c % 2][1][...], vl_ref[c])
        stage_scores(latent_keys(0), stacked_maps(qn_ref[...]), *bufs[0])
        softmax_pv(s_bufc[...], mc_bufc[...], vc_ref[...])

    acc = acc_scr[...]
    o0 = acc[:VD_A, :tq] / acc[VD_A:VD_A + 1, :tq]
    o1 = acc[:VD_A, tq:] / acc[VD_A:VD_A + 1, tq:]
    dl = dl_ref[...]
    lam = (jnp.exp(jnp.sum(dl[0:1] * dl[1:2], axis=1, keepdims=True))
           - jnp.exp(jnp.sum(dl[2:3] * dl[3:4], axis=1, keepdims=True)) + lambda_init)
    o = o0 - lam * o1
    y = o * lax.rsqrt(jnp.mean(o * o, axis=0, keepdims=True) + EPS) * g_ref[...]
    o_ref[...] = (y * (1.0 - lambda_init)).T.astype(o_ref.dtype)


def _diff_attention(dl, g_col, qa_t, ka_lat, va_lat, ka_ctx, va_ctx, *, lambda_init, tq, tk):
    nq = qa_t.shape[2]
    c = ka_ctx.shape[1]
    assert va_ctx.shape[1] == 1
    n_main = 0 if ka_lat is None else ka_lat.shape[1] // tk
    in_specs = [
        pl.BlockSpec((4, HD), lambda h, i: (0, 0)),
        pl.BlockSpec((VD_A, 1), lambda h, i: (0, 0)),
        pl.BlockSpec((None, LANES, tq), lambda h, i: (h, 0, i)),
    ]
    args = [dl, g_col, qa_t]
    if n_main:
        s_keys = ka_lat.shape[1]
        assert va_lat.shape[1:] == (n_main, VT_ROWS, tk)
        last_tile = nq // tq - 1
        in_specs += [pl.BlockSpec((None, LANES, tq), lambda h, i: (h, 0, jnp.minimum(i + 1, last_tile))),
                     pl.BlockSpec((None, s_keys, LANES), lambda h, i: (h, 0, 0)),
                     pl.BlockSpec((None, n_main, VT_ROWS, tk), lambda h, i: (h, 0, 0, 0))]
        args += [qa_t, ka_lat, va_lat]
    in_specs += [pl.BlockSpec((None, c, LANES), lambda h, i: (h, 0, 0)),
                 pl.BlockSpec((None, None, VT_ROWS, c), lambda h, i: (h, 0, 0, 0))]
    args += [ka_ctx, va_ctx]
    scratch = [pltpu.VMEM((1, 2 * tq), f32), pltpu.VMEM((VT_ROWS, 2 * tq), f32)]
    if n_main:
        assert n_main % PIPE_UNROLL == 0
        scratch += [pltpu.VMEM((tk, 2 * tq), f32), pltpu.VMEM((tk, 2 * tq), f32), pltpu.VMEM((c, 2 * tq), f32),
                    pltpu.VMEM((1, 2 * tq), f32), pltpu.VMEM((1, 2 * tq), f32), pltpu.VMEM((1, 2 * tq), f32)]
    return pl.pallas_call(
        functools.partial(_diff_attn_kernel, tq=tq, tk=tk, n_main=n_main, lambda_init=lambda_init),
        grid=(HEADS_A, nq // tq),
        in_specs=in_specs,
        out_specs=pl.BlockSpec((tq, VD_A), lambda h, i: (i, h)),
        out_shape=jax.ShapeDtypeStruct((nq, WIDTH_A), bf16),
        scratch_shapes=scratch,
        compiler_params=_cparams(("arbitrary", "arbitrary")),
        name="diff_attn" if n_main else "diff_attn_ctx",
    )(*args)


def _win_attn_kernel(*refs, tq, band, s_len):
    if band:
        (sink_ref, q_ref, kp_ref, ks_ref, kn_ref, kc_ref,
         vp_ref, vs_ref, vn_ref, vc_ref, o_ref) = refs
    else:
        sink_ref, q_ref, kc_ref, vc_ref, o_ref = refs
    i = pl.program_id(0)
    lane = lax.broadcasted_iota(jnp.int32, (tq, LANES), 1)
    if band:
        nloc = tq + 2 * WINDOW
        qpos = i * tq + lax.broadcasted_iota(jnp.int32, (tq, nloc), 0)
        kpos = i * tq - WINDOW + lax.broadcasted_iota(jnp.int32, (tq, nloc), 1)
        valid = (jnp.abs(qpos - kpos) <= WINDOW) & (kpos >= 0) & (kpos < s_len)
    nt = (((1,), (1,)), ((), ()))
    for g in range(KV_HEADS_B):
        q = q_ref[g]
        kc = kc_ref[g]
        if band:
            kloc = jnp.concatenate([kp_ref[g], ks_ref[g], kn_ref[g]], axis=0)
        o = jnp.zeros((tq, LANES), f32)
        for r in range(2):
            sink = sink_ref[2 * g + r] * LOG2_E
            qm = jnp.where((lane < HD) if r == 0 else (lane >= HD), q, jnp.zeros_like(q))
            s_ctx = lax.dot_general(qm, kc, nt, preferred_element_type=f32)
            m = jnp.maximum(jnp.max(s_ctx, axis=1, keepdims=True), sink)
            if band:
                s_loc = lax.dot_general(qm, kloc, nt, preferred_element_type=f32)
                s_loc = jnp.where(valid, s_loc, NEG_INF)
                m = jnp.maximum(m, jnp.max(s_loc, axis=1, keepdims=True))
            e_ctx = jnp.exp2(s_ctx - m)
            l = jnp.sum(e_ctx, axis=1, keepdims=True) + jnp.exp2(sink - m)
            o_r = jnp.dot(e_ctx.astype(bf16), vc_ref[g, r], preferred_element_type=f32)
            if band:
                e_loc = jnp.exp2(s_loc - m)
                l = l + jnp.sum(e_loc, axis=1, keepdims=True)
                vloc = jnp.concatenate([vp_ref[g, r], vs_ref[g, r], vn_ref[g, r]], axis=0)
                o_r = o_r + jnp.dot(e_loc.astype(bf16), vloc, preferred_element_type=f32)
            o = o + o_r / l
        o_ref[:, g * LANES:(g + 1) * LANES] = o.astype(o_ref.dtype)


def _win_attention(sink, qb, kb_lat, vb_lat, kb_ctx, vb_ctx, *, tq):
    nq = qb.shape[1]
    c = kb_ctx.shape[1]
    band = kb_lat is not None
    smem = pl.BlockSpec(memory_space=pltpu.SMEM)
    q_spec = pl.BlockSpec((KV_HEADS_B, tq, LANES), lambda i: (0, i, 0))
    kc_spec = pl.BlockSpec((KV_HEADS_B, c, LANES), lambda i: (0, 0, 0))
    vc_spec = pl.BlockSpec((KV_HEADS_B, 2, c, LANES), lambda i: (0, 0, 0, 0))
    if band:
        per = tq // WINDOW
        last = nq // WINDOW - 1
        prev = lambda i: jnp.maximum(i * per - 1, 0)
        nxt = lambda i: jnp.minimum((i + 1) * per, last)
        in_specs = [
            smem, q_spec,
            pl.BlockSpec((KV_HEADS_B, WINDOW, LANES), lambda i: (0, prev(i), 0)),
            pl.BlockSpec((KV_HEADS_B, tq, LANES), lambda i: (0, i, 0)),
            pl.BlockSpec((KV_HEADS_B, WINDOW, LANES), lambda i: (0, nxt(i), 0)),
            kc_spec,
            pl.BlockSpec((KV_HEADS_B, 2, WINDOW, LANES), lambda i: (0, 0, prev(i), 0)),
            pl.BlockSpec((KV_HEADS_B, 2, tq, LANES), lambda i: (0, 0, i, 0)),
            pl.BlockSpec((KV_HEADS_B, 2, WINDOW, LANES), lambda i: (0, 0, nxt(i), 0)),
            vc_spec,
        ]
        args = [sink, qb, kb_lat, kb_lat, kb_lat, kb_ctx, vb_lat, vb_lat, vb_lat, vb_ctx]
    else:
        in_specs = [smem, q_spec, kc_spec, vc_spec]
        args = [sink, qb, kb_ctx, vb_ctx]
    return pl.pallas_call(
        functools.partial(_win_attn_kernel, tq=tq, band=band, s_len=nq),
        grid=(nq // tq,),
        in_specs=in_specs,
        out_specs=pl.BlockSpec((tq, WIDTH_B), lambda i: (i, 0)),
        out_shape=jax.ShapeDtypeStruct((nq, WIDTH_B), bf16),
        compiler_params=_cparams(("parallel",)),
        name="win_attn" if band else "win_attn_ctx",
    )(*args)


def _conv_kernel(zp_ref, zs_ref, zn_ref, w_ref, b_ref, lg_ref, lb_ref, o_ref, zbuf, zsh, *, tm, rc):
    i = pl.program_id(0)
    n = pl.num_programs(0)
    halo = CONV_HALO
    zero = jnp.zeros((halo, CONV_CH), f32)
    zbuf[0:halo, :] = jnp.where(i > 0, zp_ref[...], zero)
    zbuf[halo:halo + tm, :] = zs_ref[...]
    zbuf[halo + tm:2 * halo + tm, :] = jnp.where(i < n - 1, zn_ref[...], zero)
    w = w_ref[...]
    base = halo - CONV_K // 2
    span = zsh.shape[1]
    for b in range(SUBLANES):
        zsh[b] = zbuf[b:b + span, :]

    for r0 in range(0, tm, rc):
        acc = jnp.zeros((rc, CONV_CH), f32)
        for k in range(CONV_K):
            off = base + k
            start = r0 + SUBLANES * (off // SUBLANES)
            acc = acc + zsh[off % SUBLANES, start:start + rc, :] * w[k:k + 1, :]
        zc = acc + b_ref[...]
        mu = jnp.mean(zc, axis=-1, keepdims=True)
        dz = zc - mu
        var = jnp.mean(dz * dz, axis=-1, keepdims=True)
        zn = dz * lax.rsqrt(var + EPS) * lg_ref[...] + lb_ref[...]
        o_ref[r0:r0 + rc, :] = (zn * jax.nn.sigmoid(zn)).astype(o_ref.dtype)


def _conv_module(z, w, b, ln_g, ln_b, *, tm):
    n = z.shape[0]
    per = tm // CONV_HALO
    last = n // CONV_HALO - 1
    vec = lambda i: (0, 0)
    return pl.pallas_call(
        functools.partial(_conv_kernel, tm=tm, rc=32),
        grid=(n // tm,),
        in_specs=[
            pl.BlockSpec((CONV_HALO, CONV_CH), lambda i: (jnp.maximum(i * per - 1, 0), 0)),
            pl.BlockSpec((tm, CONV_CH), lambda i: (i, 0)),
            pl.BlockSpec((CONV_HALO, CONV_CH), lambda i: (jnp.minimum((i + 1) * per, last), 0)),
            pl.BlockSpec((CONV_K, CONV_CH), vec),
            pl.BlockSpec((1, CONV_CH), vec),
            pl.BlockSpec((1, CONV_CH), vec),
            pl.BlockSpec((1, CONV_CH), vec),
        ],
        out_specs=pl.BlockSpec((tm, CONV_CH), lambda i: (i, 0)),
        out_shape=jax.ShapeDtypeStruct((n, CONV_CH), bf16),
        scratch_shapes=[pltpu.VMEM((tm + 2 * CONV_HALO, CONV_CH), f32),
                        pltpu.VMEM((SUBLANES, tm + 2 * CONV_HALO - SUBLANES, CONV_CH), f32)],
        compiler_params=_cparams(("parallel",)),
        name="conv_module",
    )(z, z, z, w, b, ln_g, ln_b)


def _split_bf16(a):
    hi = a.astype(bf16)
    lo = (a - hi.astype(f32)).astype(bf16)
    return hi, lo


def _route(lt, bias_ref):
    s = [jax.nn.sigmoid(lt[e:e + 1, :]) for e in range(N_EXPERTS)]
    b = [s[e] + bias_ref[e] for e in range(N_EXPERTS)]
    grp = []
    for g in range(N_GROUPS):
        v = b[4 * g:4 * g + 4]
        best = None
        for a_i in range(4):
            for b_i in range(a_i + 1, 4):
                t = v[a_i] + v[b_i]
                best = t if best is None else jnp.maximum(best, t)
        grp.append(best)
    sel = jnp.zeros_like(lt[0:1, :], dtype=jnp.int32)
    gbest = grp[0]
    for g in range(1, N_GROUPS):
        better = grp[g] > gbest
        sel = jnp.where(better, g, sel)
        gbest = jnp.where(better, grp[g], gbest)

    def pick(rows, j):
        out = rows[j]
        for g in range(1, N_GROUPS):
            out = jnp.where(sel == g, rows[4 * g + j], out)
        return out

    vb = [pick(b, j) for j in range(4)]
    vs = [pick(s, j) for j in range(4)]
    i1 = jnp.zeros_like(sel)
    b1 = vb[0]
    for j in range(1, 4):
        better = vb[j] > b1
        i1 = jnp.where(better, j, i1)
        b1 = jnp.where(better, vb[j], b1)
    i2 = jnp.full_like(sel, -1)
    b2 = jnp.full_like(b1, -jnp.inf)
    for j in range(4):
        better = (i1 != j) & ((i2 < 0) | (vb[j] > b2))
        i2 = jnp.where(better, j, i2)
        b2 = jnp.where(better, vb[j], b2)
    s1 = vs[0]
    s2 = vs[0]
    for j in range(1, 4):
        s1 = jnp.where(i1 == j, vs[j], s1)
        s2 = jnp.where(i2 == j, vs[j], s2)
    tot = s1 + s2
    w1 = s1 / tot
    w2 = s2 / tot
    e1 = sel * EXPERTS_PER_GROUP + i1
    e2 = sel * EXPERTS_PER_GROUP + i2
    rows, asg = [], []
    zero = jnp.zeros_like(w1)
    one = jnp.ones_like(w1)
    for e in range(N_EXPERTS):
        rows.append(jnp.where(e1 == e, w1, zero) + jnp.where(e2 == e, w2, zero))
        asg.append(jnp.where((e1 == e) | (e2 == e), one, zero))
    return jnp.concatenate(rows, axis=0), jnp.concatenate(asg, axis=0)


def _outproj_kernel(bias_ref, x_ref, oa_ref, ob_ref, oc_ref, w_ref, gate_ref, sh_ref, sc_ref, g_ref,
                    rwt_ref, xo_ref, h2_ref, comb_ref, rank_ref, cnt_ref):
    y = jnp.dot(oa_ref[...], w_ref[0:WIDTH_A, :], preferred_element_type=f32)
    y = y + jnp.dot(ob_ref[...], w_ref[WIDTH_A:WIDTH_A + WIDTH_B, :], preferred_element_type=f32)
    y = y + jnp.dot(oc_ref[...], w_ref[WIDTH_A + WIDTH_B:, :], preferred_element_type=f32)
    xn = x_ref[...] + gate_ref[...] * y
    xo_ref[...] = xn
    h2 = _rmsnorm_mod(xn, g_ref[...], sh_ref[...], sc_ref[...])
    h2_ref[...] = h2.astype(bf16)
    nt = (((1,), (1,)), ((), ()))
    h_hi, h_lo = _split_bf16(h2)
    r_hi, r_lo = _split_bf16(rwt_ref[...])
    lt = (lax.dot_general(r_hi, h_hi, nt, preferred_element_type=f32)
          + lax.dot_general(r_hi, h_lo, nt, preferred_element_type=f32)
          + lax.dot_general(r_lo, h_hi, nt, preferred_element_type=f32))
    comb, asg = _route(lt, bias_ref)
    tm = lt.shape[1]
    upper = (lax.broadcasted_iota(jnp.int32, (tm, tm), 0) < lax.broadcasted_iota(jnp.int32, (tm, tm), 1))
    rank = jnp.dot(asg.astype(bf16), upper.astype(bf16), preferred_element_type=f32)
    rank = jnp.where(asg > 0.0, rank, -1.0)
    for e in range(N_EXPERTS):
        comb_ref[e] = comb[e:e + 1, :]
        rank_ref[e] = rank[e:e + 1, :]
    cnt = jnp.sum(asg, axis=1, keepdims=True).astype(jnp.int32)
    cnt_ref[...] = jnp.broadcast_to(cnt, cnt_ref.shape)


def _outproj(router_bias, x2, oa, ob, oc, w_bf, layer, gate, shift, scale, g, rwt, *, tm):
    n, d = x2.shape
    row = lambda i: (i, 0)
    const = lambda i: (0, 0)
    return pl.pallas_call(
        _outproj_kernel,
        grid=(n // tm,),
        in_specs=[
            pl.BlockSpec(memory_space=pltpu.SMEM),
            pl.BlockSpec((tm, d), row),
            pl.BlockSpec((tm, WIDTH_A), row),
            pl.BlockSpec((tm, WIDTH_B), row),
            pl.BlockSpec((tm, CONV_CH), row),
            pl.BlockSpec((None, d, d), lambda i: (layer, 0, 0)),
            pl.BlockSpec((1, d), const),
            pl.BlockSpec((1, d), const),
            pl.BlockSpec((1, d), const),
            pl.BlockSpec((1, d), const),
            pl.BlockSpec((N_EXPERTS, d), const),
        ],
        out_specs=(
            pl.BlockSpec((tm, d), row),
            pl.BlockSpec((tm, d), row),
            pl.BlockSpec((N_EXPERTS, 1, tm), lambda i: (0, 0, i)),
            pl.BlockSpec((N_EXPERTS, 1, tm), lambda i: (0, 0, i)),
            pl.BlockSpec((None, N_EXPERTS, LANES), lambda i: (i, 0, 0)),
        ),
        out_shape=(
            jax.ShapeDtypeStruct((n, d), f32),
            jax.ShapeDtypeStruct((n, d), bf16),
            jax.ShapeDtypeStruct((N_EXPERTS, 1, n), f32),
            jax.ShapeDtypeStruct((N_EXPERTS, 1, n), f32),
            jax.ShapeDtypeStruct((n // tm, N_EXPERTS, LANES), jnp.int32),
        ),
        compiler_params=_cparams(("parallel",)),
        name="outproj_router",
    )(router_bias, x2, oa, ob, oc, w_bf, gate, shift, scale, g, rwt)


def _moe_kernel(*refs, final, rows, max_blocks):
    if final:
        (nchunk_ref, blk_expert_ref, blk_slot_ref, x_ref, h_ref, comb_ref, rank_ref, w1_ref, w3_ref, w2_ref,
         gate_ref, gf_ref, o_ref, acc) = refs
    else:
        (nchunk_ref, blk_expert_ref, blk_slot_ref, x_ref, h_ref, comb_ref, rank_ref, w1_ref, w3_ref, w2_ref,
         gate_ref, o_ref, acc) = refs
    step = pl.program_id(1)
    flat_step = pl.program_id(0) * pl.num_programs(1) + step

    @pl.when(step == 0)
    def _():
        acc[...] = jnp.zeros(acc.shape, f32)

    win = comb_ref.shape[2]
    slot0 = lax.broadcasted_iota(jnp.int32, (rows, win), 0).astype(f32)

    def chunk(c, carry):
        onehots, weights, experts = [], [], []
        for i in range(MOE_CHUNK_BLOCKS):
            j = flat_step * max_blocks + c * MOE_CHUNK_BLOCKS + i
            k = blk_expert_ref[j]
            first = blk_slot_ref[j].astype(f32)
            match = rank_ref[k] == slot0 + first
            onehots.append(jnp.where(match, 1.0, 0.0).astype(bf16))
            weights.append(jnp.sum(jnp.where(match, comb_ref[k], 0.0), axis=1, keepdims=True))
            experts.append(k)
        onehot = jnp.concatenate(onehots, axis=0)
        xe = jnp.dot(onehot, h_ref[...], preferred_element_type=f32).astype(bf16)
        outs = []
        for i, k in enumerate(experts):
            xk = xe[i * rows:(i + 1) * rows]
            a = jnp.dot(xk, w1_ref[k], preferred_element_type=f32)
            g = jnp.dot(xk, w3_ref[k], preferred_element_type=f32)
            u = (a * jax.nn.sigmoid(a) * g).astype(bf16)
            y = jnp.dot(u, w2_ref[k], preferred_element_type=f32)
            outs.append((weights[i] * y).astype(bf16))
        yw = jnp.concatenate(outs, axis=0)
        acc[...] += lax.dot_general(onehot, yw, (((0,), (0,)), ((), ())), preferred_element_type=f32)
        return carry

    lax.fori_loop(0, nchunk_ref[flat_step], chunk, 0)

    @pl.when(step == pl.num_programs(1) - 1)
    def _():
        xn = x_ref[...] + gate_ref[...] * acc[...]
        if final:
            xn = xn * lax.rsqrt(jnp.mean(xn * xn, axis=-1, keepdims=True) + EPS) * gf_ref[...]
        o_ref[...] = xn


def _moe_block_tables(counts, rows, eps, max_blocks):
    n_win = counts.shape[0]
    nb = ((counts + (rows - 1)) // rows).reshape(n_win, N_EXPERTS // eps, eps)
    ends = jnp.cumsum(nb, axis=-1)
    starts = ends - nb
    total = ends[..., -1]
    j = jnp.arange(max_blocks, dtype=jnp.int32)
    owner = jnp.sum(j[None, None, :, None] >= ends[:, :, None, :], axis=-1)
    owner = jnp.minimum(owner, eps - 1).astype(jnp.int32)
    picks = owner[..., None] == jnp.arange(eps, dtype=jnp.int32)
    start_of_owner = jnp.sum(jnp.where(picks, starts[:, :, None, :], 0), axis=-1)
    slot = (j[None, None, :] - start_of_owner) * rows
    unused = j[None, None, :] >= total[..., None]
    slot = jnp.where(unused, jnp.int32(MOE_UNUSED_SLOT), slot).astype(jnp.int32)
    nchunk = ((total + (MOE_CHUNK_BLOCKS - 1)) // MOE_CHUNK_BLOCKS).astype(jnp.int32)
    return nchunk.reshape(-1), owner.reshape(-1), slot.reshape(-1)


def _moe(x2, h2, comb, rank, counts, w1, w3, w2, layer, gate, g_final, *, win):
    n, d = x2.shape
    final = g_final is not None
    rows = MOE_BLOCK_ROWS
    eps = MOE_EXPERTS_PER_STEP
    max_blocks = -(-(2 * win // rows + eps) // MOE_CHUNK_BLOCKS) * MOE_CHUNK_BLOCKS
    nchunk, blk_expert, blk_slot = _moe_block_tables(counts, rows, eps, max_blocks)
    row = lambda i, e, *_: (i, 0)
    const = lambda i, e, *_: (0, 0)
    per_expert = lambda i, e, *_: (e, 0, i)
    weight = lambda i, e, *_: (layer, e, 0, 0)
    in_specs = [
        pl.BlockSpec((win, d), row),
        pl.BlockSpec((win, d), row),
        pl.BlockSpec((eps, 1, win), per_expert),
        pl.BlockSpec((eps, 1, win), per_expert),
        pl.BlockSpec((None, eps, d, EXPERT_FF), weight),
        pl.BlockSpec((None, eps, d, EXPERT_FF), weight),
        pl.BlockSpec((None, eps, EXPERT_FF, d), weight),
        pl.BlockSpec((1, d), const),
    ]
    args = [x2, h2, comb, rank, w1, w3, w2, gate]
    if final:
        in_specs.append(pl.BlockSpec((1, d), const))
        args.append(g_final)
    return pl.pallas_call(
        functools.partial(_moe_kernel, final=final, rows=rows, max_blocks=max_blocks),
        grid_spec=pltpu.PrefetchScalarGridSpec(
            num_scalar_prefetch=3,
            grid=(n // win, N_EXPERTS // eps),
            in_specs=in_specs,
            out_specs=pl.BlockSpec((win, d), row),
            scratch_shapes=[pltpu.VMEM((win, d), f32)],
        ),
        out_shape=jax.ShapeDtypeStruct((n, d), f32),
        compiler_params=_cparams(("parallel", "arbitrary")),
        name="moe_final" if final else "moe",
    )(nchunk, blk_expert, blk_slot, *args)


def _rope_tables(n_tok):
    rows = n_tok // GRID_W
    row = np.repeat(np.arange(rows, dtype=np.float64), GRID_W)
    col = np.tile(np.arange(GRID_W, dtype=np.float64), rows)
    n_freq = HD // 4
    inv = ROPE_BASE ** (-np.arange(n_freq, dtype=np.float64) / n_freq)
    lane = np.arange(LANES)
    j = lane % HD
    use_col = (j // 32) == 1
    freq = inv[j % n_freq]
    ang = np.where(use_col[None, :], col[:, None], row[:, None]) * freq[None, :]
    sign = np.where((lane % 32) < 16, -1.0, 1.0)
    return (jnp.asarray(np.cos(ang).astype(np.float32)),
            jnp.asarray((np.sin(ang) * sign[None, :]).astype(np.float32)))


def kernel(x, c, ctx, c_ctx, w_ada, b_ada, g_mix, w_in, diff_lambda, diff_norm_g, attn_sink, conv_w, conv_b,
           conv_ln_g, conv_ln_b, w_out, g_ffn, router_w, router_bias, w1, w3, w2, g_final):
    bsz, s_len, d = x.shape
    c_len = ctx.shape[1]
    depth = w_ada.shape[0]
    assert bsz == 1 and d == D_MODEL and s_len % 1024 == 0 and c_len % 256 == 0
    tm_lat = 256
    tm_ctx = min(c_len, 256)
    win_lat = MOE_WINDOW
    tq_a, tk_a = 256, 1024

    xl = x.reshape(s_len, d)
    xc = ctx.reshape(c_len, d)
    ct = jnp.zeros((d, LANES), f32).at[:, 0].set(c[0]).at[:, 1].set(c_ctx)
    mod_all = _modulation(ct, w_ada, b_ada)
    cos_t, sin_t = _rope_tables(s_len)
    rwt = router_w.T
    w_in_bf = w_in.astype(bf16)
    w_out_bf = w_out.astype(bf16)
    w1_bf, w3_bf, w2_bf = w1.astype(bf16), w3.astype(bf16), w2.astype(bf16)
    vec = lambda a: a.reshape(1, -1)

    for l in range(depth):
        last = l == depth - 1
        lambda_init = 0.8 - 0.6 * math.exp(-0.3 * l)
        ml = [mod_all[l, 0:1, k * d:(k + 1) * d] for k in range(6)]
        mc = [mod_all[l, 1:2, k * d:(k + 1) * d] for k in range(6)]
        g_mix_l, g_ffn_l = vec(g_mix[l]), vec(g_ffn[l])
        dl, dng = diff_lambda[l], diff_norm_g[l].reshape(-1, 1)

        qa, ka, va, qb, kb, vb, z = _inproj(xl, ml[0], ml[1], g_mix_l, w_in_bf, l, cos_t, sin_t, tm=2 * tm_lat,
                                            tk=tk_a)
        qa_c, ka_c, va_c, qb_c, kb_c, vb_c, z_c = _inproj(xc, mc[0], mc[1], g_mix_l, w_in_bf, l, None, None,
                                                          tm=tm_ctx, tk=c_len)

        o_a = _diff_attention(dl, dng, qa, ka, va, ka_c, va_c, lambda_init=lambda_init, tq=tq_a, tk=tk_a)
        o_b = _win_attention(attn_sink[l], qb, kb, vb, kb_c, vb_c, tq=tm_lat)
        o_c = _conv_module(z, conv_w[l], vec(conv_b[l]), vec(conv_ln_g[l]), vec(conv_ln_b[l]), tm=4 * tm_lat)
        xl, h2, comb, rank, cnt = _outproj(router_bias, xl, o_a, o_b, o_c, w_out_bf, l, ml[2], ml[3], ml[4],
                                           g_ffn_l, rwt, tm=win_lat)
        xl = _moe(xl, h2, comb, rank, cnt[:, :, 0], w1_bf, w3_bf, w2_bf, l, ml[5],
                  vec(g_final) if last else None, win=win_lat)

        if not last:
            o_ac = _diff_attention(dl, dng, qa_c, None, None, ka_c, va_c, lambda_init=lambda_init,
                                   tq=tm_ctx, tk=tk_a)
            o_bc = _win_attention(attn_sink[l], qb_c, None, None, kb_c, vb_c, tq=tm_ctx)
            o_cc = _conv_module(z_c, conv_w[l], vec(conv_b[l]), vec(conv_ln_g[l]), vec(conv_ln_b[l]), tm=tm_ctx)
            xc, h2c, comb_c, rank_c, cnt_c = _outproj(router_bias, xc, o_ac, o_bc, o_cc, w_out_bf, l, mc[2], mc[3],
                                                      mc[4], g_ffn_l, rwt, tm=tm_ctx)
            xc = _moe(xc, h2c, comb_c, rank_c, cnt_c[:, :, 0], w1_bf, w3_bf, w2_bf, l, mc[5], None,
                      win=tm_ctx)

    return xl.reshape(bsz, s_len, d)
```

```python
import functools
import math

import numpy as np
import jax
import jax.numpy as jnp
from jax import lax
from jax.experimental import pallas as pl
from jax.experimental.pallas import tpu as pltpu

f32 = jnp.float32
bf16 = jnp.bfloat16

D_MODEL = 1024
GRID_W = 64
HEADS_A = 4
HD = 64
VD_A = 2 * HD
BF16_SUBLANES = 16
VT_ROWS = VD_A + BF16_SUBLANES
WIDTH_A = HEADS_A * VD_A
HEADS_B = 4
KV_HEADS_B = 2
WIDTH_B = HEADS_B * HD
WINDOW = 128
CONV_CH = 256
CONV_K = 31
CONV_HALO = 16
IN_WIDTH = 2560
OFF_QA, OFF_KA, OFF_VA, OFF_QB, OFF_KB, OFF_VB, OFF_UC = 0, 512, 1024, 1536, 1792, 1920, 2048
N_EXPERTS = 16
N_GROUPS = 4
EXPERTS_PER_GROUP = 4
EXPERT_FF = 512
ROPE_BASE = 10000.0
EPS = 1e-6
NEG_INF = -1e30
LANES = 128
SUBLANES = 8
QK_SCALE = HD ** -0.5
LOG2_E = math.log2(math.e)
QK_SCALE_LOG2 = QK_SCALE * LOG2_E

MOE_WINDOW = 1024
MOE_EXPERTS_PER_STEP = 4
MOE_BLOCK_ROWS = 128
MOE_CHUNK_BLOCKS = 2
MOE_UNUSED_SLOT = 1 << 24
PIPE_UNROLL = 4
MOD_COLS_PER_STEP = 1536
VMEM_LIMIT = 56 * 1024 * 1024


def _cparams(sem):
    return pltpu.CompilerParams(dimension_semantics=sem, vmem_limit_bytes=VMEM_LIMIT)


def _mod_kernel(ct_ref, w_ref, b_ref, o_ref):
    tn = w_ref.shape[1]

    def body(i, carry):
        a0, a1 = carry
        r = pl.multiple_of(i * 8, 8)
        cv = ct_ref[pl.ds(r, 8), :]
        sv = cv * jax.nn.sigmoid(cv)
        w8 = w_ref[pl.ds(r, 8), :]
        return a0 + w8 * sv[:, 0:1], a1 + w8 * sv[:, 1:2]

    z = jnp.zeros((8, tn), f32)
    a0, a1 = lax.fori_loop(0, w_ref.shape[0] // 8, body, (z, z), unroll=8)
    r0 = jnp.sum(a0, axis=0, keepdims=True) + b_ref[...]
    r1 = jnp.sum(a1, axis=0, keepdims=True) + b_ref[...]
    o_ref[...] = jnp.concatenate([r0, r1, jnp.zeros((6, tn), f32)], axis=0)


def _modulation(ct, w_ada, b_ada):
    depth, d, n = w_ada.shape
    tn = MOD_COLS_PER_STEP
    return pl.pallas_call(
        _mod_kernel,
        grid=(depth, n // tn),
        in_specs=[
            pl.BlockSpec((d, LANES), lambda l, j: (0, 0)),
            pl.BlockSpec((None, d, tn), lambda l, j: (l, 0, j)),
            pl.BlockSpec((None, 1, tn), lambda l, j: (l, 0, j)),
        ],
        out_specs=pl.BlockSpec((None, 8, tn), lambda l, j: (l, 0, j)),
        out_shape=jax.ShapeDtypeStruct((depth, 8, n), f32),
        compiler_params=_cparams(("arbitrary", "arbitrary")),
        name="modulation",
    )(ct, w_ada, b_ada.reshape(depth, 1, n))


def _rmsnorm_mod(xf, g, shift, scale):
    y = xf * lax.rsqrt(jnp.mean(xf * xf, axis=-1, keepdims=True) + EPS) * g
    return y * (1.0 + scale) + shift


def _inproj_kernel(*refs, rope):
    if rope:
        x_ref, sh_ref, sc_ref, g_ref, w_ref, cos_ref, sin_ref = refs[:7]
        outs = refs[7:]
    else:
        x_ref, sh_ref, sc_ref, g_ref, w_ref = refs[:5]
        outs = refs[5:]
    qa_ref, ka_ref, va_ref, qb_ref, kb_ref, vb_ref, z_ref = outs

    h = _rmsnorm_mod(x_ref[...], g_ref[...], sh_ref[...], sc_ref[...])
    p = jnp.dot(h.astype(bf16), w_ref[...], preferred_element_type=f32)
    tm = p.shape[0]
    lane = lax.broadcasted_iota(jnp.int32, (tm, LANES), 1)
    lo_half = lane < HD

    def rot(xc, scale):
        if rope:
            first = (lane % 32) < 16
            partner = jnp.where(first, pltpu.roll(xc, LANES - 16, 1), pltpu.roll(xc, 16, 1))
            xc = xc * cos_ref[...] + partner * sin_ref[...]
        return xc * scale if scale != 1.0 else xc

    def chunk(off, j):
        return p[:, off + j * LANES: off + (j + 1) * LANES]

    ones = jnp.ones((VT_ROWS - VD_A, tm), f32)
    for hh in range(HEADS_A):
        qa_ref[hh] = rot(chunk(OFF_QA, hh), QK_SCALE_LOG2).T.astype(bf16)
        ka_ref[hh] = rot(chunk(OFF_KA, hh), 1.0).astype(bf16)
        va_ref[hh] = jnp.concatenate([chunk(OFF_VA, hh).T, ones], axis=0).astype(bf16)
    for g in range(KV_HEADS_B):
        qb_ref[g] = rot(chunk(OFF_QB, g), QK_SCALE_LOG2).astype(bf16)
    kb = rot(chunk(OFF_KB, 0), 1.0)
    kb_sw = pltpu.roll(kb, HD, 1)
    kb_ref[0] = jnp.where(lo_half, kb, kb_sw).astype(bf16)
    kb_ref[1] = jnp.where(lo_half, kb_sw, kb).astype(bf16)
    vb = chunk(OFF_VB, 0)
    vb_sw = pltpu.roll(vb, HD, 1)
    zero = jnp.zeros_like(vb)
    vb_ref[0, 0] = jnp.where(lo_half, vb, zero).astype(bf16)
    vb_ref[0, 1] = jnp.where(lo_half, zero, vb_sw).astype(bf16)
    vb_ref[1, 0] = jnp.where(lo_half, vb_sw, zero).astype(bf16)
    vb_ref[1, 1] = jnp.where(lo_half, zero, vb).astype(bf16)
    a = p[:, OFF_UC: OFF_UC + CONV_CH]
    gt = p[:, OFF_UC + CONV_CH: OFF_UC + 2 * CONV_CH]
    z_ref[...] = a * jax.nn.sigmoid(gt)


def _inproj(x2, shift, scale, g, w_bf, layer, cos_t, sin_t, *, tm, tk):
    n, d = x2.shape
    rope = cos_t is not None
    per = tk // tm
    row = lambda i: (i, 0)
    const = lambda i: (0, 0)
    in_specs = [
        pl.BlockSpec((tm, d), row),
        pl.BlockSpec((1, d), const),
        pl.BlockSpec((1, d), const),
        pl.BlockSpec((1, d), const),
        pl.BlockSpec((None, d, IN_WIDTH), lambda i: (layer, 0, 0)),
    ]
    args = [x2, shift, scale, g, w_bf]
    if rope:
        in_specs += [pl.BlockSpec((tm, LANES), row), pl.BlockSpec((tm, LANES), row)]
        args += [cos_t, sin_t]
    out_shape = (
        jax.ShapeDtypeStruct((HEADS_A, LANES, n), bf16),
        jax.ShapeDtypeStruct((HEADS_A, n, LANES), bf16),
        jax.ShapeDtypeStruct((HEADS_A, n // tk, VT_ROWS, tk), bf16),
        jax.ShapeDtypeStruct((KV_HEADS_B, n, LANES), bf16),
        jax.ShapeDtypeStruct((KV_HEADS_B, n, LANES), bf16),
        jax.ShapeDtypeStruct((KV_HEADS_B, 2, n, LANES), bf16),
        jax.ShapeDtypeStruct((n, CONV_CH), f32),
    )
    out_specs = (
        pl.BlockSpec((HEADS_A, LANES, tm), lambda i: (0, 0, i)),
        pl.BlockSpec((HEADS_A, tm, LANES), lambda i: (0, i, 0)),
        pl.BlockSpec((HEADS_A, None, VT_ROWS, tm), lambda i: (0, i // per, 0, i % per)),
        pl.BlockSpec((KV_HEADS_B, tm, LANES), lambda i: (0, i, 0)),
        pl.BlockSpec((KV_HEADS_B, tm, LANES), lambda i: (0, i, 0)),
        pl.BlockSpec((KV_HEADS_B, 2, tm, LANES), lambda i: (0, 0, i, 0)),
        pl.BlockSpec((tm, CONV_CH), row),
    )
    return pl.pallas_call(
        functools.partial(_inproj_kernel, rope=rope),
        grid=(n // tm,),
        in_specs=in_specs,
        out_specs=out_specs,
        out_shape=out_shape,
        compiler_params=_cparams(("parallel",)),
        name="inproj_rope" if rope else "inproj_ctx",
    )(*args)


def _diff_attn_kernel(*refs, tq, tk, n_main, lambda_init):
    if n_main:
        (dl_ref, g_ref, q_ref, qn_ref, kl_ref, vl_ref, kc_ref, vc_ref, o_ref,
         m_scr, acc_scr, s_buf0, s_buf1, s_bufc, mc_buf0, mc_buf1, mc_bufc) = refs
    else:
        dl_ref, g_ref, q_ref, kc_ref, vc_ref, o_ref, m_scr, acc_scr = refs

    def stacked_maps(qt):
        sub = lax.broadcasted_iota(jnp.int32, qt.shape, 0)
        zero = jnp.zeros_like(qt)
        return jnp.concatenate([jnp.where(sub < HD, qt, zero), jnp.where(sub < HD, zero, qt)], axis=1)

    qq = stacked_maps(q_ref[...])
    m_scr[...] = jnp.full(m_scr.shape, NEG_INF, f32)
    acc_scr[...] = jnp.zeros(acc_scr.shape, f32)

    def softmax_pv(s, m_chunk, vt):
        m_old = m_scr[...]
        m_new = jnp.maximum(m_old, m_chunk)
        alpha = jnp.exp2(m_old - m_new)
        p = jnp.exp2(s - m_new).astype(bf16)
        acc_scr[...] = alpha * acc_scr[...] + jnp.dot(vt, p, preferred_element_type=f32)
        m_scr[...] = m_new

    if not n_main:
        s_ctx = jnp.dot(kc_ref[...], qq, preferred_element_type=f32)
        softmax_pv(s_ctx, jnp.max(s_ctx, axis=0, keepdims=True), vc_ref[...])
    else:
        def stage_scores(keys, q_stacked, s_buf, mc_buf):
            s = jnp.dot(keys, q_stacked, preferred_element_type=f32)
            s_buf[...] = s
            mc_buf[...] = jnp.max(s, axis=0, keepdims=True)

        def latent_keys(c):
            return kl_ref[pl.ds(pl.multiple_of(c * tk, tk), tk), :]

        bufs = ((s_buf0, mc_buf0), (s_buf1, mc_buf1))
        unroll = PIPE_UNROLL

        @pl.when(pl.program_id(1) == 0)
        def _():
            stage_scores(latent_keys(0), qq, *bufs[0])

        def body(j, carry):
            c = unroll * j
            for u in range(unroll):
                stage_scores(latent_keys(c + u + 1), qq, *bufs[(u + 1) % 2])
                softmax_pv(bufs[u % 2][0][...], bufs[u % 2][1][...], vl_ref[c + u])
            return carry

        lax.fori_loop(0, n_main // unroll - 1, body, 0)
        for c in range(n_main - unroll, n_main):
            if c + 1 < n_main:
                stage_scores(latent_keys(c + 1), qq, *bufs[(c + 1) % 2])
            else:
                stage_scores(kc_ref[...], qq, s_bufc, mc_bufc)
            softmax_pv(bufs[c % 2][0][...], bufs[c % 2][1][...], vl_ref[c])
        stage_scores(latent_keys(0), stacked_maps(qn_ref[...]), *bufs[0])
        softmax_pv(s_bufc[...], mc_bufc[...], vc_ref[...])

    acc = acc_scr[...]
    o0 = acc[:VD_A, :tq] / acc[VD_A:VD_A + 1, :tq]
    o1 = acc[:VD_A, tq:] / acc[VD_A:VD_A + 1, tq:]
    dl = dl_ref[...]
    lam = (jnp.exp(jnp.sum(dl[0:1] * dl[1:2], axis=1, keepdims=True))
           - jnp.exp(jnp.sum(dl[2:3] * dl[3:4], axis=1, keepdims=True)) + lambda_init)
    o = o0 - lam * o1
    y = o * lax.rsqrt(jnp.mean(o * o, axis=0, keepdims=True) + EPS) * g_ref[...]
    o_ref[...] = (y * (1.0 - lambda_init)).T.astype(o_ref.dtype)


def _diff_attention(dl, g_col, qa_t, ka_lat, va_lat, ka_ctx, va_ctx, *, lambda_init, tq, tk):
    nq = qa_t.shape[2]
    c = ka_ctx.shape[1]
    assert va_ctx.shape[1] == 1
    n_main = 0 if ka_lat is None else ka_lat.shape[1] // tk
    in_specs = [
        pl.BlockSpec((4, HD), lambda h, i: (0, 0)),
        pl.BlockSpec((VD_A, 1), lambda h, i: (0, 0)),
        pl.BlockSpec((None, LANES, tq), lambda h, i: (h, 0, i)),
    ]
    args = [dl, g_col, qa_t]
    if n_main:
        s_keys = ka_lat.shape[1]
        assert va_lat.shape[1:] == (n_main, VT_ROWS, tk)
        last_tile = nq // tq - 1
        in_specs += [pl.BlockSpec((None, LANES, tq), lambda h, i: (h, 0, jnp.minimum(i + 1, last_tile))),
                     pl.BlockSpec((None, s_keys, LANES), lambda h, i: (h, 0, 0)),
                     pl.BlockSpec((None, n_main, VT_ROWS, tk), lambda h, i: (h, 0, 0, 0))]
        args += [qa_t, ka_lat, va_lat]
    in_specs += [pl.BlockSpec((None, c, LANES), lambda h, i: (h, 0, 0)),
                 pl.BlockSpec((None, None, VT_ROWS, c), lambda h, i: (h, 0, 0, 0))]
    args += [ka_ctx, va_ctx]
    scratch = [pltpu.VMEM((1, 2 * tq), f32), pltpu.VMEM((VT_ROWS, 2 * tq), f32)]
    if n_main:
        assert n_main % PIPE_UNROLL == 0
        scratch += [pltpu.VMEM((tk, 2 * tq), f32), pltpu.VMEM((tk, 2 * tq), f32), pltpu.VMEM((c, 2 * tq), f32),
                    pltpu.VMEM((1, 2 * tq), f32), pltpu.VMEM((1, 2 * tq), f32), pltpu.VMEM((1, 2 * tq), f32)]
    return pl.pallas_call(
        functools.partial(_diff_attn_kernel, tq=tq, tk=tk, n_main=n_main, lambda_init=lambda_init),
        grid=(HEADS_A, nq // tq),
        in_specs=in_specs,
        out_specs=pl.BlockSpec((tq, VD_A), lambda h, i: (i, h)),
        out_shape=jax.ShapeDtypeStruct((nq, WIDTH_A), bf16),
        scratch_shapes=scratch,
        compiler_params=_cparams(("arbitrary", "arbitrary")),
        name="diff_attn" if n_main else "diff_attn_ctx",
    )(*args)


def _win_attn_kernel(*refs, tq, band, s_len):
    if band:
        (sink_ref, q_ref, kp_ref, ks_ref, kn_ref, kc_ref,
         vp_ref, vs_ref, vn_ref, vc_ref, o_ref) = refs
    else:
        sink_ref, q_ref, kc_ref, vc_ref, o_ref = refs
    i = pl.program_id(0)
    lane = lax.broadcasted_iota(jnp.int32, (tq, LANES), 1)
    if band:
        nloc = tq + 2 * WINDOW
        qpos = i * tq + lax.broadcasted_iota(jnp.int32, (tq, nloc), 0)
        kpos = i * tq - WINDOW + lax.broadcasted_iota(jnp.int32, (tq, nloc), 1)
        valid = (jnp.abs(qpos - kpos) <= WINDOW) & (kpos >= 0) & (kpos < s_len)
    nt = (((1,), (1,)), ((), ()))
    for g in range(KV_HEADS_B):
        q = q_ref[g]
        kc = kc_ref[g]
        if band:
            kloc = jnp.concatenate([kp_ref[g], ks_ref[g], kn_ref[g]], axis=0)
        o = jnp.zeros((tq, LANES), f32)
        for r in range(2):
            sink = sink_ref[2 * g + r] * LOG2_E
            qm = jnp.where((lane < HD) if r == 0 else (lane >= HD), q, jnp.zeros_like(q))
            s_ctx = lax.dot_general(qm, kc, nt, preferred_element_type=f32)
            m = jnp.maximum(jnp.max(s_ctx, axis=1, keepdims=True), sink)
            if band:
                s_loc = lax.dot_general(qm, kloc, nt, preferred_element_type=f32)
                s_loc = jnp.where(valid, s_loc, NEG_INF)
                m = jnp.maximum(m, jnp.max(s_loc, axis=1, keepdims=True))
            e_ctx = jnp.exp2(s_ctx - m)
            l = jnp.sum(e_ctx, axis=1, keepdims=True) + jnp.exp2(sink - m)
            o_r = jnp.dot(e_ctx.astype(bf16), vc_ref[g, r], preferred_element_type=f32)
            if band:
                e_loc = jnp.exp2(s_loc - m)
                l = l + jnp.sum(e_loc, axis=1, keepdims=True)
                vloc = jnp.concatenate([vp_ref[g, r], vs_ref[g, r], vn_ref[g, r]], axis=0)
                o_r = o_r + jnp.dot(e_loc.astype(bf16), vloc, preferred_element_type=f32)
            o = o + o_r / l
        o_ref[:, g * LANES:(g + 1) * LANES] = o.astype(o_ref.dtype)


def _win_attention(sink, qb, kb_lat, vb_lat, kb_ctx, vb_ctx, *, tq):
    nq = qb.shape[1]
    c = kb_ctx.shape[1]
    band = kb_lat is not None
    smem = pl.BlockSpec(memory_space=pltpu.SMEM)
    q_spec = pl.BlockSpec((KV_HEADS_B, tq, LANES), lambda i: (0, i, 0))
    kc_spec = pl.BlockSpec((KV_HEADS_B, c, LANES), lambda i: (0, 0, 0))
    vc_spec = pl.BlockSpec((KV_HEADS_B, 2, c, LANES), lambda i: (0, 0, 0, 0))
    if band:
        per = tq // WINDOW
        last = nq // WINDOW - 1
        prev = lambda i: jnp.maximum(i * per - 1, 0)
        nxt = lambda i: jnp.minimum((i + 1) * per, last)
        in_specs = [
            smem, q_spec,
            pl.BlockSpec((KV_HEADS_B, WINDOW, LANES), lambda i: (0, prev(i), 0)),
            pl.BlockSpec((KV_HEADS_B, tq, LANES), lambda i: (0, i, 0)),
            pl.BlockSpec((KV_HEADS_B, WINDOW, LANES), lambda i: (0, nxt(i), 0)),
            kc_spec,
            pl.BlockSpec((KV_HEADS_B, 2, WINDOW, LANES), lambda i: (0, 0, prev(i), 0)),
            pl.BlockSpec((KV_HEADS_B, 2, tq, LANES), lambda i: (0, 0, i, 0)),
            pl.BlockSpec((KV_HEADS_B, 2, WINDOW, LANES), lambda i: (0, 0, nxt(i), 0)),
            vc_spec,
        ]
        args = [sink, qb, kb_lat, kb_lat, kb_lat, kb_ctx, vb_lat, vb_lat, vb_lat, vb_ctx]
    else:
        in_specs = [smem, q_spec, kc_spec, vc_spec]
        args = [sink, qb, kb_ctx, vb_ctx]
    return pl.pallas_call(
        functools.partial(_win_attn_kernel, tq=tq, band=band, s_len=nq),
        grid=(nq // tq,),
        in_specs=in_specs,
        out_specs=pl.BlockSpec((tq, WIDTH_B), lambda i: (i, 0)),
        out_shape=jax.ShapeDtypeStruct((nq, WIDTH_B), bf16),
        compiler_params=_cparams(("parallel",)),
        name="win_attn" if band else "win_attn_ctx",
    )(*args)


def _conv_kernel(zp_ref, zs_ref, zn_ref, w_ref, b_ref, lg_ref, lb_ref, o_ref, zbuf, zsh, *, tm, rc):
    i = pl.program_id(0)
    n = pl.num_programs(0)
    halo = CONV_HALO
    zero = jnp.zeros((halo, CONV_CH), f32)
    zbuf[0:halo, :] = jnp.where(i > 0, zp_ref[...], zero)
    zbuf[halo:halo + tm, :] = zs_ref[...]
    zbuf[halo + tm:2 * halo + tm, :] = jnp.where(i < n - 1, zn_ref[...], zero)
    w = w_ref[...]
    base = halo - CONV_K // 2
    span = zsh.shape[1]
    for b in range(SUBLANES):
        zsh[b] = zbuf[b:b + span, :]

    for r0 in range(0, tm, rc):
        acc = jnp.zeros((rc, CONV_CH), f32)
        for k in range(CONV_K):
            off = base + k
            start = r0 + SUBLANES * (off // SUBLANES)
            acc = acc + zsh[off % SUBLANES, start:start + rc, :] * w[k:k + 1, :]
        zc = acc + b_ref[...]
        mu = jnp.mean(zc, axis=-1, keepdims=True)
        dz = zc - mu
        var = jnp.mean(dz * dz, axis=-1, keepdims=True)
        zn = dz * lax.rsqrt(var + EPS) * lg_ref[...] + lb_ref[...]
        o_ref[r0:r0 + rc, :] = (zn * jax.nn.sigmoid(zn)).astype(o_ref.dtype)


def _conv_module(z, w, b, ln_g, ln_b, *, tm):
    n = z.shape[0]
    per = tm // CONV_HALO
    last = n // CONV_HALO - 1
    vec = lambda i: (0, 0)
    return pl.pallas_call(
        functools.partial(_conv_kernel, tm=tm, rc=32),
        grid=(n // tm,),
        in_specs=[
            pl.BlockSpec((CONV_HALO, CONV_CH), lambda i: (jnp.maximum(i * per - 1, 0), 0)),
            pl.BlockSpec((tm, CONV_CH), lambda i: (i, 0)),
            pl.BlockSpec((CONV_HALO, CONV_CH), lambda i: (jnp.minimum((i + 1) * per, last), 0)),
            pl.BlockSpec((CONV_K, CONV_CH), vec),
            pl.BlockSpec((1, CONV_CH), vec),
            pl.BlockSpec((1, CONV_CH), vec),
            pl.BlockSpec((1, CONV_CH), vec),
        ],
        out_specs=pl.BlockSpec((tm, CONV_CH), lambda i: (i, 0)),
        out_shape=jax.ShapeDtypeStruct((n, CONV_CH), bf16),
        scratch_shapes=[pltpu.VMEM((tm + 2 * CONV_HALO, CONV_CH), f32),
                        pltpu.VMEM((SUBLANES, tm + 2 * CONV_HALO - SUBLANES, CONV_CH), f32)],
        compiler_params=_cparams(("parallel",)),
        name="conv_module",
    )(z, z, z, w, b, ln_g, ln_b)


def _split_bf16(a):
    hi = a.astype(bf16)
    lo = (a - hi.astype(f32)).astype(bf16)
    return hi, lo


def _route(lt, bias_ref):
    s = [jax.nn.sigmoid(lt[e:e + 1, :]) for e in range(N_EXPERTS)]
    b = [s[e] + bias_ref[e] for e in range(N_EXPERTS)]
    grp = []
    for g in range(N_GROUPS):
        v = b[4 * g:4 * g + 4]
        best = None
        for a_i in range(4):
            for b_i in range(a_i + 1, 4):
                t = v[a_i] + v[b_i]
                best = t if best is None else jnp.maximum(best, t)
        grp.append(best)
    sel = jnp.zeros_like(lt[0:1, :], dtype=jnp.int32)
    gbest = grp[0]
    for g in range(1, N_GROUPS):
        better = grp[g] > gbest
        sel = jnp.where(better, g, sel)
        gbest = jnp.where(better, grp[g], gbest)

    def pick(rows, j):
        out = rows[j]
        for g in range(1, N_GROUPS):
            out = jnp.where(sel == g, rows[4 * g + j], out)
        return out

    vb = [pick(b, j) for j in range(4)]
    vs = [pick(s, j) for j in range(4)]
    i1 = jnp.zeros_like(sel)
    b1 = vb[0]
    for j in range(1, 4):
        better = vb[j] > b1
        i1 = jnp.where(better, j, i1)
        b1 = jnp.where(better, vb[j], b1)
    i2 = jnp.full_like(sel, -1)
    b2 = jnp.full_like(b1, -jnp.inf)
    for j in range(4):
        better = (i1 != j) & ((i2 < 0) | (vb[j] > b2))
        i2 = jnp.where(better, j, i2)
        b2 = jnp.where(better, vb[j], b2)
    s1 = vs[0]
    s2 = vs[0]
    for j in range(1, 4):
        s1 = jnp.where(i1 == j, vs[j], s1)
        s2 = jnp.where(i2 == j, vs[j], s2)
    tot = s1 + s2
    w1 = s1 / tot
    w2 = s2 / tot
    e1 = sel * EXPERTS_PER_GROUP + i1
    e2 = sel * EXPERTS_PER_GROUP + i2
    rows, asg = [], []
    zero = jnp.zeros_like(w1)
    one = jnp.ones_like(w1)
    for e in range(N_EXPERTS):
        rows.append(jnp.where(e1 == e, w1, zero) + jnp.where(e2 == e, w2, zero))
        asg.append(jnp.where((e1 == e) | (e2 == e), one, zero))
    return jnp.concatenate(rows, axis=0), jnp.concatenate(asg, axis=0)


def _outproj_kernel(bias_ref, x_ref, oa_ref, ob_ref, oc_ref, w_ref, gate_ref, sh_ref, sc_ref, g_ref,
                    rwt_ref, xo_ref, h2_ref, comb_ref, rank_ref, cnt_ref):
    y = jnp.dot(oa_ref[...], w_ref[0:WIDTH_A, :], preferred_element_type=f32)
    y = y + jnp.dot(ob_ref[...], w_ref[WIDTH_A:WIDTH_A + WIDTH_B, :], preferred_element_type=f32)
    y = y + jnp.dot(oc_ref[...], w_ref[WIDTH_A + WIDTH_B:, :], preferred_element_type=f32)
    xn = x_ref[...] + gate_ref[...] * y
    xo_ref[...] = xn
    h2 = _rmsnorm_mod(xn, g_ref[...], sh_ref[...], sc_ref[...])
    h2_ref[...] = h2.astype(bf16)
    nt = (((1,), (1,)), ((), ()))
    h_hi, h_lo = _split_bf16(h2)
    r_hi, r_lo = _split_bf16(rwt_ref[...])
    lt = (lax.dot_general(r_hi, h_hi, nt, preferred_element_type=f32)
          + lax.dot_general(r_hi, h_lo, nt, preferred_element_type=f32)
          + lax.dot_general(r_lo, h_hi, nt, preferred_element_type=f32))
    comb, asg = _route(lt, bias_ref)
    tm = lt.shape[1]
    upper = (lax.broadcasted_iota(jnp.int32, (tm, tm), 0) < lax.broadcasted_iota(jnp.int32, (tm, tm), 1))
    rank = jnp.dot(asg.astype(bf16), upper.astype(bf16), preferred_element_type=f32)
    rank = jnp.where(asg > 0.0, rank, -1.0)
    for e in range(N_EXPERTS):
        comb_ref[e] = comb[e:e + 1, :]
        rank_ref[e] = rank[e:e + 1, :]
    cnt = jnp.sum(asg, axis=1, keepdims=True).astype(jnp.int32)
    cnt_ref[...] = jnp.broadcast_to(cnt, cnt_ref.shape)


def _outproj(router_bias, x2, oa, ob, oc, w_bf, layer, gate, shift, scale, g, rwt, *, tm):
    n, d = x2.shape
    row = lambda i: (i, 0)
    const = lambda i: (0, 0)
    return pl.pallas_call(
        _outproj_kernel,
        grid=(n // tm,),
        in_specs=[
            pl.BlockSpec(memory_space=pltpu.SMEM),
            pl.BlockSpec((tm, d), row),
            pl.BlockSpec((tm, WIDTH_A), row),
            pl.BlockSpec((tm, WIDTH_B), row),
            pl.BlockSpec((tm, CONV_CH), row),
            pl.BlockSpec((None, d, d), lambda i: (layer, 0, 0)),
            pl.BlockSpec((1, d), const),
            pl.BlockSpec((1, d), const),
            pl.BlockSpec((1, d), const),
            pl.BlockSpec((1, d), const),
            pl.BlockSpec((N_EXPERTS, d), const),
        ],
        out_specs=(
            pl.BlockSpec((tm, d), row),
            pl.BlockSpec((tm, d), row),
            pl.BlockSpec((N_EXPERTS, 1, tm), lambda i: (0, 0, i)),
            pl.BlockSpec((N_EXPERTS, 1, tm), lambda i: (0, 0, i)),
            pl.BlockSpec((None, N_EXPERTS, LANES), lambda i: (i, 0, 0)),
        ),
        out_shape=(
            jax.ShapeDtypeStruct((n, d), f32),
            jax.ShapeDtypeStruct((n, d), bf16),
            jax.ShapeDtypeStruct((N_EXPERTS, 1, n), f32),
            jax.ShapeDtypeStruct((N_EXPERTS, 1, n), f32),
            jax.ShapeDtypeStruct((n // tm, N_EXPERTS, LANES), jnp.int32),
        ),
        compiler_params=_cparams(("parallel",)),
        name="outproj_router",
    )(router_bias, x2, oa, ob, oc, w_bf, gate, shift, scale, g, rwt)


def _moe_kernel(*refs, final, rows, max_blocks):
    if final:
        (nblocks_ref, blk_expert_ref, blk_slot_ref, x_ref, h_ref, comb_ref, rank_ref, w1_ref, w3_ref, w2_ref,
         gate_ref, gf_ref, o_ref, acc) = refs
    else:
        (nblocks_ref, blk_expert_ref, blk_slot_ref, x_ref, h_ref, comb_ref, rank_ref, w1_ref, w3_ref, w2_ref,
         gate_ref, o_ref, acc) = refs
    step = pl.program_id(1)
    flat_step = pl.program_id(0) * pl.num_programs(1) + step

    @pl.when(step == 0)
    def _():
        acc[...] = jnp.zeros(acc.shape, f32)

    win = comb_ref.shape[2]
    slot0 = lax.broadcasted_iota(jnp.int32, (rows, win), 0).astype(f32)

    def run_blocks(first_block, n_blk):
        onehots, weights, experts = [], [], []
        for i in range(n_blk):
            j = flat_step * max_blocks + first_block + i
            k = blk_expert_ref[j]
            first = blk_slot_ref[j].astype(f32)
            match = rank_ref[k] == slot0 + first
            onehots.append(jnp.where(match, 1.0, 0.0).astype(bf16))
            weights.append(jnp.sum(jnp.where(match, comb_ref[k], 0.0), axis=1, keepdims=True))
            experts.append(k)
        onehot = jnp.concatenate(onehots, axis=0)
        xe = jnp.dot(onehot, h_ref[...], preferred_element_type=f32).astype(bf16)
        outs = []
        for i, k in enumerate(experts):
            xk = xe[i * rows:(i + 1) * rows]
            a = jnp.dot(xk, w1_ref[k], preferred_element_type=f32)
            g = jnp.dot(xk, w3_ref[k], preferred_element_type=f32)
            u = (a * jax.nn.sigmoid(a) * g).astype(bf16)
            y = jnp.dot(u, w2_ref[k], preferred_element_type=f32)
            outs.append((weights[i] * y).astype(bf16))
        yw = jnp.concatenate(outs, axis=0)
        acc[...] += lax.dot_general(onehot, yw, (((0,), (0,)), ((), ())), preferred_element_type=f32)

    def chunk(c, carry):
        run_blocks(c * MOE_CHUNK_BLOCKS, MOE_CHUNK_BLOCKS)
        return carry

    n_blocks = nblocks_ref[flat_step]
    n_chunks = n_blocks // MOE_CHUNK_BLOCKS
    lax.fori_loop(0, n_chunks, chunk, 0)
    for i in range(MOE_CHUNK_BLOCKS - 1):
        @pl.when(n_chunks * MOE_CHUNK_BLOCKS + i < n_blocks)
        def _(i=i):
            run_blocks(n_chunks * MOE_CHUNK_BLOCKS + i, 1)

    @pl.when(step == pl.num_programs(1) - 1)
    def _():
        xn = x_ref[...] + gate_ref[...] * acc[...]
        if final:
            xn = xn * lax.rsqrt(jnp.mean(xn * xn, axis=-1, keepdims=True) + EPS) * gf_ref[...]
        o_ref[...] = xn


def _moe_block_tables(counts, rows, eps, max_blocks):
    n_win = counts.shape[0]
    nb = ((counts + (rows - 1)) // rows).reshape(n_win, N_EXPERTS // eps, eps)
    ends = jnp.cumsum(nb, axis=-1)
    starts = ends - nb
    total = ends[..., -1]
    j = jnp.arange(max_blocks, dtype=jnp.int32)
    owner = jnp.sum(j[None, None, :, None] >= ends[:, :, None, :], axis=-1)
    owner = jnp.minimum(owner, eps - 1).astype(jnp.int32)
    picks = owner[..., None] == jnp.arange(eps, dtype=jnp.int32)
    start_of_owner = jnp.sum(jnp.where(picks, starts[:, :, None, :], 0), axis=-1)
    slot = (j[None, None, :] - start_of_owner) * rows
    unused = j[None, None, :] >= total[..., None]
    slot = jnp.where(unused, jnp.int32(MOE_UNUSED_SLOT), slot).astype(jnp.int32)
    return total.astype(jnp.int32).reshape(-1), owner.reshape(-1), slot.reshape(-1)


def _moe(x2, h2, comb, rank, counts, w1, w3, w2, layer, gate, g_final, *, win):
    n, d = x2.shape
    final = g_final is not None
    rows = MOE_BLOCK_ROWS
    eps = MOE_EXPERTS_PER_STEP
    max_blocks = -(-(2 * win // rows + eps) // MOE_CHUNK_BLOCKS) * MOE_CHUNK_BLOCKS
    n_blocks, blk_expert, blk_slot = _moe_block_tables(counts, rows, eps, max_blocks)
    row = lambda i, e, *_: (i, 0)
    const = lambda i, e, *_: (0, 0)
    per_expert = lambda i, e, *_: (e, 0, i)
    weight = lambda i, e, *_: (layer, e, 0, 0)
    in_specs = [
        pl.BlockSpec((win, d), row),
        pl.BlockSpec((win, d), row),
        pl.BlockSpec((eps, 1, win), per_expert),
        pl.BlockSpec((eps, 1, win), per_expert),
        pl.BlockSpec((None, eps, d, EXPERT_FF), weight),
        pl.BlockSpec((None, eps, d, EXPERT_FF), weight),
        pl.BlockSpec((None, eps, EXPERT_FF, d), weight),
        pl.BlockSpec((1, d), const),
    ]
    args = [x2, h2, comb, rank, w1, w3, w2, gate]
    if final:
        in_specs.append(pl.BlockSpec((1, d), const))
        args.append(g_final)
    return pl.pallas_call(
        functools.partial(_moe_kernel, final=final, rows=rows, max_blocks=max_blocks),
        grid_spec=pltpu.PrefetchScalarGridSpec(
            num_scalar_prefetch=3,
            grid=(n // win, N_EXPERTS // eps),
            in_specs=in_specs,
            out_specs=pl.BlockSpec((win, d), row),
            scratch_shapes=[pltpu.VMEM((win, d), f32)],
        ),
        out_shape=jax.ShapeDtypeStruct((n, d), f32),
        compiler_params=_cparams(("parallel", "arbitrary")),
        name="moe_final" if final else "moe",
    )(n_blocks, blk_expert, blk_slot, *args)


def _rope_tables(n_tok):
    rows = n_tok // GRID_W
    row = np.repeat(np.arange(rows, dtype=np.float64), GRID_W)
    col = np.tile(np.arange(GRID_W, dtype=np.float64), rows)
    n_freq = HD // 4
    inv = ROPE_BASE ** (-np.arange(n_freq, dtype=np.float64) / n_freq)
    lane = np.arange(LANES)
    j = lane % HD
    use_col = (j // 32) == 1
    freq = inv[j % n_freq]
    ang = np.where(use_col[None, :], col[:, None], row[:, None]) * freq[None, :]
    sign = np.where((lane % 32) < 16, -1.0, 1.0)
    return (jnp.asarray(np.cos(ang).astype(np.float32)),
            jnp.asarray((np.sin(ang) * sign[None, :]).astype(np.float32)))


def kernel(x, c, ctx, c_ctx, w_ada, b_ada, g_mix, w_in, diff_lambda, diff_norm_g, attn_sink, conv_w, conv_b,
           conv_ln_g, conv_ln_b, w_out, g_ffn, router_w, router_bias, w1, w3, w2, g_final):
    bsz, s_len, d = x.shape
    c_len = ctx.shape[1]
    depth = w_ada.shape[0]
    assert bsz == 1 and d == D_MODEL and s_len % 1024 == 0 and c_len % 256 == 0
    tm_lat = 256
    tm_ctx = min(c_len, 256)
    win_lat = MOE_WINDOW
    tq_a, tk_a = 256, 1024

    xl = x.reshape(s_len, d)
    xc = ctx.reshape(c_len, d)
    ct = jnp.zeros((d, LANES), f32).at[:, 0].set(c[0]).at[:, 1].set(c_ctx)
    mod_all = _modulation(ct, w_ada, b_ada)
    cos_t, sin_t = _rope_tables(s_len)
    rwt = router_w.T
    w_in_bf = w_in.astype(bf16)
    w_out_bf = w_out.astype(bf16)
    w1_bf, w3_bf, w2_bf = w1.astype(bf16), w3.astype(bf16), w2.astype(bf16)
    vec = lambda a: a.reshape(1, -1)

    for l in range(depth):
        last = l == depth - 1
        lambda_init = 0.8 - 0.6 * math.exp(-0.3 * l)
        ml = [mod_all[l, 0:1, k * d:(k + 1) * d] for k in range(6)]
        mc = [mod_all[l, 1:2, k * d:(k + 1) * d] for k in range(6)]
        g_mix_l, g_ffn_l = vec(g_mix[l]), vec(g_ffn[l])
        dl, dng = diff_lambda[l], diff_norm_g[l].reshape(-1, 1)

        qa, ka, va, qb, kb, vb, z = _inproj(xl, ml[0], ml[1], g_mix_l, w_in_bf, l, cos_t, sin_t, tm=2 * tm_lat,
                                            tk=tk_a)
        qa_c, ka_c, va_c, qb_c, kb_c, vb_c, z_c = _inproj(xc, mc[0], mc[1], g_mix_l, w_in_bf, l, None, None,
                                                          tm=tm_ctx, tk=c_len)

        o_a = _diff_attention(dl, dng, qa, ka, va, ka_c, va_c, lambda_init=lambda_init, tq=tq_a, tk=tk_a)
        o_b = _win_attention(attn_sink[l], qb, kb, vb, kb_c, vb_c, tq=tm_lat)
        o_c = _conv_module(z, conv_w[l], vec(conv_b[l]), vec(conv_ln_g[l]), vec(conv_ln_b[l]), tm=4 * tm_lat)
        xl, h2, comb, rank, cnt = _outproj(router_bias, xl, o_a, o_b, o_c, w_out_bf, l, ml[2], ml[3], ml[4],
                                           g_ffn_l, rwt, tm=win_lat)
        xl = _moe(xl, h2, comb, rank, cnt[:, :, 0], w1_bf, w3_bf, w2_bf, l, ml[5],
                  vec(g_final) if last else None, win=win_lat)

        if not last:
            o_ac = _diff_attention(dl, dng, qa_c, None, None, ka_c, va_c, lambda_init=lambda_init,
                                   tq=tm_ctx, tk=tk_a)
            o_bc = _win_attention(attn_sink[l], qb_c, None, None, kb_c, vb_c, tq=tm_ctx)
            o_cc = _conv_module(z_c, conv_w[l], vec(conv_b[l]), vec(conv_ln_g[l]), vec(conv_ln_b[l]), tm=tm_ctx)
            xc, h2c, comb_c, rank_c, cnt_c = _outproj(router_bias, xc, o_ac, o_bc, o_cc, w_out_bf, l, mc[2], mc[3],
                                                      mc[4], g_ffn_l, rwt, tm=tm_ctx)
            xc = _moe(xc, h2c, comb_c, rank_c, cnt_c[:, :, 0], w1_bf, w3_bf, w2_bf, l, mc[5], None,
                      win=tm_ctx)

    return xl.reshape(bsz, s_len, d)
```
